```python
import math
import jax, jax.numpy as jnp
from jax import lax
import numpy as np

D_MODEL = 1024
BATCH = 4
SEQ = 4096
DEPTH = 1

MIX_WIDTH = D_MODEL
POOL_WIDTH = MIX_WIDTH // 2
POOL_WINDOWS = (2, 4, 8, 16)
N_POOL_GROUPS = len(POOL_WINDOWS)
POOL_GROUP_DIM = POOL_WIDTH // N_POOL_GROUPS
ATTN_WIDTH = MIX_WIDTH - POOL_WIDTH
HEAD_DIM = 64
N_ATTN_HEADS = ATTN_WIDTH // HEAD_DIM
DILATED_BRANCHES = ((128, 1), (512, 4), (2048, 16))
ATTN_BLOCK = 128
IN_WIDTH = POOL_WIDTH + 3 * ATTN_WIDTH
N_REL_BUCKETS = 32
REL_MAX_EXACT = N_REL_BUCKETS // 2
REL_MAX_DISTANCE = 2048
N_EXPERTS = 32
TOP_K = 4
D_EXPERT = D_MODEL
SWIGLU_LIMIT = 7.0
SWIGLU_ALPHA = 1.702
MOE_BLOCK = 128
PLE_DIM = 256
NORM_EPS = 1e-6
NEG_INF = -1e30

kernel_name = "hybrid_pool_dilated_attn_moe_ple"


def rmsnorm(x, g):
    xf = x.astype(jnp.float32)
    y = xf * lax.rsqrt(jnp.mean(xf * xf, axis=-1, keepdims=True) + NORM_EPS)
    return (y * g.astype(jnp.float32)).astype(x.dtype)


def pool_mixer(u, w_pool, pool_scale):
    B, S, _ = u.shape
    ug = u.reshape(B, S, N_POOL_GROUPS, POOL_GROUP_DIM)
    t = jnp.arange(S)
    outs = []
    for gi, w in enumerate(POOL_WINDOWS):
        ch = ug[:, :, gi].astype(jnp.float32)
        cs = jnp.cumsum(ch, axis=1)
        prev = jnp.pad(cs, ((0, 0), (w, 0), (0, 0)))[:, :S]
        cnt = jnp.minimum(t + 1, w).astype(jnp.float32)[None, :, None]
        outs.append(((cs - prev) / cnt - ch).astype(u.dtype))
    pooled = jnp.stack(outs, axis=2)
    mixed = jnp.einsum('bsgc,gcd->bsgd', pooled, w_pool)
    return mixed.reshape(B, S, POOL_WIDTH) * pool_scale


def t5_bucket(dist):
    n = jnp.maximum(dist, 1).astype(jnp.float32)
    large = REL_MAX_EXACT + (jnp.log(n / REL_MAX_EXACT) / math.log(REL_MAX_DISTANCE / REL_MAX_EXACT)
                             * (N_REL_BUCKETS - REL_MAX_EXACT)).astype(jnp.int32)
    large = jnp.minimum(large, N_REL_BUCKETS - 1)
    return jnp.where(dist < REL_MAX_EXACT, dist, large)


def dilated_branch(q, k, v, rel_bias, window, dil):
    B, S, H, Dh = q.shape
    w_sub = window // dil
    blk = ATTN_BLOCK
    span = dil * blk
    Sp = -(-S // span) * span
    L = Sp // dil
    nb = L // blk

    def to_sub(t):
        t = jnp.pad(t, ((0, 0), (0, Sp - S), (0, 0), (0, 0)))
        t = t.reshape(B, nb, blk, dil, H, Dh)
        return t.transpose(0, 3, 4, 1, 2, 5)

    qs, ks, vs = to_sub(q), to_sub(k), to_sub(v)
    shift = lambda t: jnp.concatenate([jnp.zeros_like(t[:, :, :, :1]), t[:, :, :, :-1]], axis=3)
    keys = jnp.concatenate([shift(ks), ks], axis=4)
    vals = jnp.concatenate([shift(vs), vs], axis=4)

    s = jnp.einsum('brhnqc,brhnkc->brhnqk', qs, keys).astype(jnp.float32) * (1.0 / math.sqrt(Dh))
    qi = jnp.arange(blk)[:, None]
    ki = jnp.arange(2 * blk)[None, :]
    rel = blk + qi - ki
    bias = rel_bias[t5_bucket(jnp.maximum(rel, 0) * dil)]
    s = s + bias.transpose(2, 0, 1)[:, None].astype(jnp.float32)
    band = (rel >= 0) & (rel <= w_sub)
    first_ok = (jnp.arange(nb)[:, None, None] > 0) | (ki[None] >= blk)
    mask = band[None] & first_ok
    s = jnp.where(mask, s, NEG_INF)
    m = jnp.max(s, axis=-1)
    e = jnp.exp(s - m[..., None])
    l = jnp.sum(e, axis=-1)
    o = jnp.einsum('brhnqk,brhnkc->brhnqc', e, vals.astype(jnp.float32)) / l[..., None]
    lse = m + jnp.log(l)
    o = o.transpose(0, 3, 4, 1, 2, 5).reshape(B, Sp, H, Dh)[:, :S]
    lse = lse.transpose(0, 3, 4, 1, 2).reshape(B, Sp, H)[:, :S]
    return o, lse


def dilated_attention(q, k, v, rel_bias):
    outs, lses = [], []
    for window, dil in DILATED_BRANCHES:
        o, lse = dilated_branch(q, k, v, rel_bias, window, dil)
        outs.append(o)
        lses.append(lse)
    w = jax.nn.softmax(jnp.stack(lses, axis=0), axis=0)
    out = jnp.sum(w[..., None] * jnp.stack(outs, axis=0), axis=0)
    return out.astype(q.dtype)


def moe(h, w_router, b_router, w_gate_up, b_gate_up, w_down, b_down):
    B, S, D = h.shape
    T = B * S
    hf = h.reshape(T, D)
    logits = (hf @ w_router + b_router).astype(jnp.float32)
    topv, topi = lax.top_k(logits, TOP_K)
    gates = jax.nn.softmax(topv, axis=-1)

    expert_ids = topi.reshape(-1)
    token_ids = jnp.repeat(jnp.arange(T), TOP_K)
    gate_flat = gates.reshape(-1)
    order = jnp.argsort(expert_ids)
    sorted_e = expert_ids[order]
    group_sizes = jnp.bincount(expert_ids, length=N_EXPERTS)
    group_starts = jnp.cumsum(group_sizes) - group_sizes
    padded_sizes = (group_sizes + MOE_BLOCK - 1) // MOE_BLOCK * MOE_BLOCK
    padded_ends = jnp.cumsum(padded_sizes)
    padded_starts = padded_ends - padded_sizes
    rank = jnp.arange(T * TOP_K) - group_starts[sorted_e]
    dest = padded_starts[sorted_e] + rank

    n_rows = T * TOP_K + N_EXPERTS * MOE_BLOCK
    n_blocks = n_rows // MOE_BLOCK
    row_token = jnp.zeros((n_rows,), jnp.int32).at[dest].set(token_ids[order])
    row_gate = jnp.zeros((n_rows,), h.dtype).at[dest].set(gate_flat[order].astype(h.dtype))
    block_expert = jnp.minimum(
        jnp.searchsorted(padded_ends, jnp.arange(n_blocks) * MOE_BLOCK, side='right'),
        N_EXPERTS - 1)
    xs = hf[row_token].reshape(n_blocks, MOE_BLOCK, D)

    def expert_block(args):
        xb, e = args
        gu = xb @ w_gate_up[e] + b_gate_up[e]
        g, u = gu[:, :D_EXPERT], gu[:, D_EXPERT:]
        g = jnp.minimum(g, SWIGLU_LIMIT)
        u = jnp.clip(u, -SWIGLU_LIMIT, SWIGLU_LIMIT)
        glu = g * jax.nn.sigmoid(SWIGLU_ALPHA * g)
        return ((u + 1.0) * glu) @ w_down[e] + b_down[e]

    ys = lax.map(expert_block, (xs, block_expert)).reshape(n_rows, D)
    out = jnp.zeros((T, D), h.dtype).at[row_token].add(ys * row_gate[:, None])
    return out.reshape(B, S, D)


def setup_inputs(seed: int = 0) -> dict:
    key = jax.random.key(seed)
    ks = jax.random.split(key, 24)
    f32 = jnp.float32
    nrm = lambda k, shape, s: jax.random.normal(k, shape, f32) * s
    gain = lambda k, shape: 1.0 + 0.05 * jax.random.normal(k, shape, f32)
    return {
        "x": nrm(ks[0], (BATCH, SEQ, D_MODEL), 1.0),
        "p": nrm(ks[1], (DEPTH, BATCH, SEQ, PLE_DIM), 1.0),
        "g_mix": gain(ks[2], (DEPTH, D_MODEL)),
        "w_in": nrm(ks[3], (DEPTH, D_MODEL, IN_WIDTH), D_MODEL ** -0.5),
        "w_pool": nrm(ks[4], (DEPTH, N_POOL_GROUPS, POOL_GROUP_DIM, POOL_GROUP_DIM), POOL_GROUP_DIM ** -0.5),
        "pool_scale": gain(ks[5], (DEPTH, POOL_WIDTH)),
        "rel_bias": nrm(ks[6], (N_REL_BUCKETS, N_ATTN_HEADS), 0.5),
        "w_out": nrm(ks[7], (DEPTH, MIX_WIDTH, D_MODEL), MIX_WIDTH ** -0.5),
        "g_ffn": gain(ks[8], (DEPTH, D_MODEL)),
        "w_router": nrm(ks[9], (DEPTH, D_MODEL, N_EXPERTS), D_MODEL ** -0.5),
        "b_router": nrm(ks[10], (DEPTH, N_EXPERTS), 0.01),
        "w_gate_up": nrm(ks[11], (DEPTH, N_EXPERTS, D_MODEL, 2 * D_EXPERT), D_MODEL ** -0.5),
        "b_gate_up": nrm(ks[12], (DEPTH, N_EXPERTS, 2 * D_EXPERT), 0.01),
        "w_down": nrm(ks[13], (DEPTH, N_EXPERTS, D_EXPERT, D_MODEL), D_EXPERT ** -0.5),
        "b_down": nrm(ks[14], (DEPTH, N_EXPERTS, D_MODEL), 0.01),
        "g_ple": gain(ks[15], (DEPTH, D_MODEL)),
        "w_ple_gate": nrm(ks[16], (DEPTH, D_MODEL, D_MODEL), D_MODEL ** -0.5),
        "w_ple_proj": nrm(ks[17], (DEPTH, PLE_DIM, D_MODEL), PLE_DIM ** -0.5),
        "g_final": gain(ks[18], (D_MODEL,)),
    }


def reference(x, p, g_mix, w_in, w_pool, pool_scale, rel_bias, w_out, g_ffn,
              w_router, b_router, w_gate_up, b_gate_up, w_down, b_down,
              g_ple, w_ple_gate, w_ple_proj, g_final):
    B, S, _ = x.shape
    for i in range(DEPTH):
        h = rmsnorm(x, g_mix[i])
        z = jnp.einsum('bsd,de->bse', h, w_in[i])
        u = z[..., :POOL_WIDTH]
        qkv = z[..., POOL_WIDTH:].reshape(B, S, 3, N_ATTN_HEADS, HEAD_DIM)
        q, k, v = qkv[:, :, 0], qkv[:, :, 1], qkv[:, :, 2]
        a_out = pool_mixer(u, w_pool[i], pool_scale[i])
        b_out = dilated_attention(q, k, v, rel_bias).reshape(B, S, ATTN_WIDTH)
        mix = jnp.concatenate([a_out, b_out], axis=-1)
        x = x + jnp.einsum('bse,ed->bsd', mix, w_out[i])
        x = x + moe(rmsnorm(x, g_ffn[i]), w_router[i], b_router[i], w_gate_up[i],
                    b_gate_up[i], w_down[i], b_down[i])
        gate = jax.nn.sigmoid(jnp.einsum('bsd,de->bse', rmsnorm(x, g_ple[i]), w_ple_gate[i]))
        x = x + jnp.einsum('bsp,pd->bsd', p[i], w_ple_proj[i]) * gate
    return rmsnorm(x, g_final)
```

```python
import functools
import math

import jax
import jax.numpy as jnp
from jax import lax
from jax.experimental import pallas as pl
from jax.experimental.pallas import tpu as pltpu

F32 = jnp.float32
BF16 = jnp.bfloat16
I32 = jnp.int32

D_MODEL = 1024
POOL_WIDTH = 512
POOL_WINDOWS = (2, 4, 8, 16)
POOL_GROUP_DIM = 128
ATTN_WIDTH = 512
HEAD_DIM = 64
N_HEADS = 8
DILATED_BRANCHES = ((128, 1), (512, 4), (2048, 16))
ATTN_BLOCK = 128
N_REL_BUCKETS = 32
REL_MAX_EXACT = 16
REL_MAX_DISTANCE = 2048
N_EXPERTS = 32
TOP_K = 4
D_EXPERT = 1024
SWIGLU_LIMIT = 7.0
SWIGLU_ALPHA = 1.702
PLE_DIM = 256
NORM_EPS = 1e-6
NEG_INF = -1e30

POOL_HALO = 16
TOKEN_TILE = 512
ROW_ALIGN = 16
EXPERT_TILE = 512
LOCAL_ROWS = TOKEN_TILE * TOP_K + N_EXPERTS * ROW_ALIGN
SORT_BLOCK = 256
GATHER_BLOCK = 512
VMEM_LIMIT = 56 * 1024 * 1024


def _rms(x, g):
    return x * lax.rsqrt(jnp.mean(x * x, axis=-1, keepdims=True) + NORM_EPS) * g


def _inproj_kernel(x_ref, g_ref, w_ref, u_ref, qkv_ref):
    h = _rms(x_ref[...], g_ref[...]).astype(BF16)
    z = jnp.dot(h, w_ref[...], preferred_element_type=F32)
    u_ref[...] = z[:, :POOL_WIDTH]
    qkv_ref[...] = z[:, POOL_WIDTH:].astype(BF16)


def _inproj(x2d, g, w_bf16):
    t = x2d.shape[0]
    in_w = w_bf16.shape[1]
    return pl.pallas_call(
        _inproj_kernel,
        grid=(t // TOKEN_TILE,),
        in_specs=[
            pl.BlockSpec((TOKEN_TILE, D_MODEL), lambda i: (i, 0)),
            pl.BlockSpec((1, D_MODEL), lambda i: (0, 0)),
            pl.BlockSpec((D_MODEL, in_w), lambda i: (0, 0)),
        ],
        out_specs=[
            pl.BlockSpec((TOKEN_TILE, POOL_WIDTH), lambda i: (i, 0)),
            pl.BlockSpec((TOKEN_TILE, in_w - POOL_WIDTH), lambda i: (i, 0)),
        ],
        out_shape=[
            jax.ShapeDtypeStruct((t, POOL_WIDTH), F32),
            jax.ShapeDtypeStruct((t, in_w - POOL_WIDTH), BF16),
        ],
        compiler_params=pltpu.CompilerParams(
            dimension_semantics=("parallel",), vmem_limit_bytes=VMEM_LIMIT),
        name="inproj",
    )(x2d, g, w_bf16)


def _attn_kernel(q_ref, kp_ref, kc_ref, vp_ref, vc_ref, tab_ref, o_ref, lse_ref, *, blocks_per_sub):
    n = pl.program_id(1)
    pen = jnp.where((n % blocks_per_sub) == 0, NEG_INF, 0.0).astype(F32)
    nt = (((1,), (1,)), ((), ()))
    for h in range(N_HEADS):
        sl = pl.ds(h * HEAD_DIM, HEAD_DIM)
        q = q_ref[:, sl] * 0.125
        s_p = lax.dot_general(q, kp_ref[:, sl], nt, preferred_element_type=F32)
        s_c = lax.dot_general(q, kc_ref[:, sl], nt, preferred_element_type=F32)
        s_p = s_p + tab_ref[h, :, :ATTN_BLOCK] + pen
        s_c = s_c + tab_ref[h, :, ATTN_BLOCK:]
        m = jnp.maximum(jnp.max(s_p, axis=-1, keepdims=True), jnp.max(s_c, axis=-1, keepdims=True))
        p_p = jnp.exp(s_p - m)
        p_c = jnp.exp(s_c - m)
        l = jnp.sum(p_p, axis=-1, keepdims=True) + jnp.sum(p_c, axis=-1, keepdims=True)
        acc = jnp.dot(p_p.astype(BF16), vp_ref[:, sl], preferred_element_type=F32)
        acc = acc + jnp.dot(p_c.astype(BF16), vc_ref[:, sl], preferred_element_type=F32)
        o_ref[:, sl] = (acc / l).astype(BF16)
        lse_ref[:, pl.ds(h, 1)] = m + jnp.log(l)


def _attn_branch(qkv_sub, tab, blocks_per_sub):
    b, s, _ = qkv_sub.shape
    nb = s // ATTN_BLOCK
    blk = (None, ATTN_BLOCK, ATTN_WIDTH)
    return pl.pallas_call(
        functools.partial(_attn_kernel, blocks_per_sub=blocks_per_sub),
        grid=(b, nb),
        in_specs=[
            pl.BlockSpec(blk, lambda bi, n: (bi, n, 0)),
            pl.BlockSpec(blk, lambda bi, n: (bi, jnp.maximum(n - 1, 0), 1)),
            pl.BlockSpec(blk, lambda bi, n: (bi, n, 1)),
            pl.BlockSpec(blk, lambda bi, n: (bi, jnp.maximum(n - 1, 0), 2)),
            pl.BlockSpec(blk, lambda bi, n: (bi, n, 2)),
            pl.BlockSpec((N_HEADS, ATTN_BLOCK, 2 * ATTN_BLOCK), lambda bi, n: (0, 0, 0)),
        ],
        out_specs=[
            pl.BlockSpec(blk, lambda bi, n: (bi, n, 0)),
            pl.BlockSpec((None, ATTN_BLOCK, N_HEADS), lambda bi, n: (bi, n, 0)),
        ],
        out_shape=[
            jax.ShapeDtypeStruct((b, s, ATTN_WIDTH), BF16),
            jax.ShapeDtypeStruct((b, s, N_HEADS), F32),
        ],
        compiler_params=pltpu.CompilerParams(
            dimension_semantics=("parallel", "parallel"), vmem_limit_bytes=VMEM_LIMIT),
        name="attn",
    )(qkv_sub, qkv_sub, qkv_sub, qkv_sub, qkv_sub, tab)


def _t5_bucket(dist):
    n = jnp.maximum(dist, 1).astype(F32)
    large = REL_MAX_EXACT + (jnp.log(n / REL_MAX_EXACT) / math.log(REL_MAX_DISTANCE / REL_MAX_EXACT)
                             * (N_REL_BUCKETS - REL_MAX_EXACT)).astype(I32)
    large = jnp.minimum(large, N_REL_BUCKETS - 1)
    return jnp.where(dist < REL_MAX_EXACT, dist, large)


def _bias_table(rel_bias, window, dil):
    qi = jnp.arange(ATTN_BLOCK)[:, None]
    ki = jnp.arange(2 * ATTN_BLOCK)[None, :]
    rel = ATTN_BLOCK + qi - ki
    bias = rel_bias[_t5_bucket(jnp.maximum(rel, 0) * dil)]
    band = (rel >= 0) & (rel <= window // dil)
    return jnp.where(band[None], bias.transpose(2, 0, 1).astype(F32), NEG_INF)


def _mixout_kernel(x_ref, u_ref, uh_ref, o1_ref, o4_ref, o16_ref, l1_ref, l4_ref, l16_ref,
                   wpool_ref, pscale_ref, wout_ref, gffn_ref, wrt_ref, brt_ref,
                   x1_ref, h2_ref, pos_ref, gate_ref, pc_ref,
                   ext_scr, mix_scr, *, tiles_per_seq):
    i = pl.program_id(0)
    tt = TOKEN_TILE
    seq_tile = i % tiles_per_seq

    halo = uh_ref[...]
    ext_scr[0:POOL_HALO, :] = jnp.where(seq_tile == 0, jnp.zeros_like(halo), halo)
    ext_scr[POOL_HALO:, :] = u_ref[...]
    tpos = seq_tile * tt + lax.broadcasted_iota(I32, (tt, 1), 0)
    for gi, w in enumerate(POOL_WINDOWS):
        cols = pl.ds(gi * POOL_GROUP_DIM, POOL_GROUP_DIM)
        tok = ext_scr[pl.ds(POOL_HALO, tt), cols]
        acc = tok
        for j in range(1, w):
            acc = acc + ext_scr[pl.ds(POOL_HALO - j, tt), cols]
        cnt = jnp.minimum(tpos + 1, w).astype(F32)
        pooled = (acc / cnt - tok).astype(BF16)
        mixed = jnp.dot(pooled, wpool_ref[gi], preferred_element_type=F32) * pscale_ref[:, cols]
        mix_scr[:, cols] = mixed.astype(BF16)

    l1, l4, l16 = l1_ref[...], l4_ref[...], l16_ref[...]
    lm = jnp.maximum(jnp.maximum(l1, l4), l16)
    e1, e4, e16 = jnp.exp(l1 - lm), jnp.exp(l4 - lm), jnp.exp(l16 - lm)
    inv = 1.0 / (e1 + e4 + e16)
    w1, w4, w16 = e1 * inv, e4 * inv, e16 * inv
    for h in range(N_HEADS):
        sl = pl.ds(h * HEAD_DIM, HEAD_DIM)
        o = (w1[:, h:h + 1] * o1_ref[:, sl].astype(F32) + w4[:, h:h + 1] * o4_ref[:, sl].astype(F32)
             + w16[:, h:h + 1] * o16_ref[:, sl].astype(F32))
        mix_scr[:, pl.ds(POOL_WIDTH + h * HEAD_DIM, HEAD_DIM)] = o.astype(BF16)

    x1 = x_ref[...] + jnp.dot(mix_scr[...], wout_ref[...], preferred_element_type=F32)
    x1_ref[...] = x1

    h2 = _rms(x1, gffn_ref[...])
    h2_ref[...] = h2.astype(BF16)
    logits = lax.dot_general(wrt_ref[...], h2, (((1,), (1,)), ((), ())),
                             precision=lax.Precision.HIGHEST, preferred_element_type=F32)
    logits = logits + brt_ref[:, 0:1]
    eio = lax.broadcasted_iota(I32, (N_EXPERTS, tt), 0)
    sel, val = [], []
    for _ in range(TOP_K):
        m = jnp.max(logits, axis=0, keepdims=True)
        idx = jnp.min(jnp.where(logits == m, eio, N_EXPERTS), axis=0, keepdims=True)
        sel.append(idx)
        val.append(m)
        logits = jnp.where(eio == idx, -jnp.inf, logits)
    ex = [jnp.exp(v - val[0]) for v in val]
    den = ex[0] + ex[1] + ex[2] + ex[3]
    gates = [e / den for e in ex]

    hot = [(eio == s) for s in sel]
    onehot = (hot[0] | hot[1] | hot[2] | hot[3]).astype(F32)
    ti = lax.broadcasted_iota(I32, (tt, tt), 0)
    tj = lax.broadcasted_iota(I32, (tt, tt), 1)
    before = (ti < tj).astype(BF16)
    rank = jnp.dot(onehot.astype(BF16), before, preferred_element_type=F32)
    cnt = jnp.sum(onehot, axis=1, keepdims=True)
    pc = jnp.floor((cnt + (ROW_ALIGN - 1)) * (1.0 / ROW_ALIGN))
    pcb = jnp.broadcast_to(pc, (N_EXPERTS, 128))
    pc_ref[0] = pcb
    pcm = jnp.concatenate([pcb, jnp.zeros((128 - N_EXPERTS, 128), F32)], axis=0).astype(BF16)
    li = lax.broadcasted_iota(I32, (N_EXPERTS, 128), 0)
    lj = lax.broadcasted_iota(I32, (N_EXPERTS, 128), 1)
    lower = (lj < li).astype(BF16)
    off = jnp.dot(lower, pcm, preferred_element_type=F32)[:, 0:1] * float(ROW_ALIGN)
    where_to = off + rank
    zero_i = jnp.zeros((1, tt), I32)
    zero_f = jnp.zeros((1, tt), F32)
    pos_rows = [jnp.sum(jnp.where(hk, where_to, 0.0), axis=0, keepdims=True).astype(I32) for hk in hot]
    pos_ref[0] = jnp.concatenate(pos_rows + [zero_i] * (8 - TOP_K), axis=0)
    gate_ref[0] = jnp.concatenate(gates + [zero_f] * (8 - TOP_K), axis=0)


def _mixout(x2d, u, o_list, lse_list, wpool, pscale, wout, gffn, wrt, brt, seq):
    t = x2d.shape[0]
    tt = TOKEN_TILE
    n_tt = t // tt
    halo_blocks = tt // POOL_HALO
    row = lambda i: (i, 0)
    const2 = lambda i: (0, 0)
    return pl.pallas_call(
        functools.partial(_mixout_kernel, tiles_per_seq=seq // tt),
        grid=(n_tt,),
        in_specs=[
            pl.BlockSpec((tt, D_MODEL), row),
            pl.BlockSpec((tt, POOL_WIDTH), row),
            pl.BlockSpec((POOL_HALO, POOL_WIDTH), lambda i: (jnp.maximum(i * halo_blocks - 1, 0), 0)),
            pl.BlockSpec((tt, ATTN_WIDTH), row), pl.BlockSpec((tt, ATTN_WIDTH), row), pl.BlockSpec((tt, ATTN_WIDTH), row),
            pl.BlockSpec((tt, N_HEADS), row), pl.BlockSpec((tt, N_HEADS), row), pl.BlockSpec((tt, N_HEADS), row),
            pl.BlockSpec((len(POOL_WINDOWS), POOL_GROUP_DIM, POOL_GROUP_DIM), lambda i: (0, 0, 0)),
            pl.BlockSpec((1, POOL_WIDTH), const2),
            pl.BlockSpec((D_MODEL, D_MODEL), const2),
            pl.BlockSpec((1, D_MODEL), const2),
            pl.BlockSpec((N_EXPERTS, D_MODEL), const2),
            pl.BlockSpec((N_EXPERTS, 128), const2),
        ],
        out_specs=[
            pl.BlockSpec((tt, D_MODEL), row),
            pl.BlockSpec((tt, D_MODEL), row),
            pl.BlockSpec((1, 8, tt), lambda i: (i, 0, 0)),
            pl.BlockSpec((1, 8, tt), lambda i: (i, 0, 0)),
            pl.BlockSpec((1, N_EXPERTS, 128), lambda i: (i, 0, 0)),
        ],
        out_shape=[
            jax.ShapeDtypeStruct((t, D_MODEL), F32),
            jax.ShapeDtypeStruct((t, D_MODEL), BF16),
            jax.ShapeDtypeStruct((n_tt, 8, tt), I32),
            jax.ShapeDtypeStruct((n_tt, 8, tt), F32),
            jax.ShapeDtypeStruct((n_tt, N_EXPERTS, 128), F32),
        ],
        scratch_shapes=[
            pltpu.VMEM((tt + POOL_HALO, POOL_WIDTH), F32),
            pltpu.VMEM((tt, D_MODEL), BF16),
        ],
        compiler_params=pltpu.CompilerParams(
            dimension_semantics=("parallel",), vmem_limit_bytes=VMEM_LIMIT),
        name="mixout",
    )(x2d, u, u, *o_list, *lse_list, wpool, pscale, wout, gffn, wrt, brt)


def _chunk_copies(loff_s, pcnt_s, gstart_s, i, local_ref, global_ref, sem, to_global):
    out = []
    for e in range(N_EXPERTS):
        n = pl.multiple_of(pcnt_s[i, e], ROW_ALIGN)
        lo =pl.multiple_of(loff_s[i, e], ROW_ALIGN)
        go = pl.multiple_of(gstart_s[i, e], ROW_ALIGN)
        loc = local_ref.at[pl.ds(lo, n)]
        glo = global_ref.at[pl.ds(go, n)]
        src, dst = (loc, glo) if to_global else (glo, loc)
        out.append((n > 0, pltpu.make_async_copy(src, dst, sem)))
    return out


def _dispatch_kernel(loff_s, pcnt_s, gstart_s, h2_ref, pos_ref, xs_in_ref, xs_ref, loc_scr, sem):
    del xs_in_ref
    i = pl.program_id(0)
    tt = TOKEN_TILE
    pos = pos_ref[0]
    h2 = h2_ref[...]
    for jb in range(LOCAL_ROWS // SORT_BLOCK):
        jio = lax.broadcasted_iota(I32, (SORT_BLOCK, tt), 0) + jb * SORT_BLOCK
        hit = (jio == pos[0:1]) | (jio == pos[1:2]) | (jio == pos[2:3]) | (jio == pos[3:4])
        onehot = jnp.where(hit, 1.0, 0.0).astype(BF16)
        loc_scr[pl.ds(jb * SORT_BLOCK, SORT_BLOCK), :] = jnp.dot(
            onehot, h2, preferred_element_type=F32).astype(BF16)
    copies = _chunk_copies(loff_s, pcnt_s, gstart_s, i, loc_scr, xs_ref, sem, True)
    for cond, cp in copies:
        pl.when(cond)(cp.start)
    for cond, cp in copies:
        pl.when(cond)(cp.wait)


def _dispatch(loff, pcnt, gstart, h2, pos, xs_init):
    t = h2.shape[0]
    tt = TOKEN_TILE
    grid_spec = pltpu.PrefetchScalarGridSpec(
        num_scalar_prefetch=3,
        grid=(t // tt,),
        in_specs=[
            pl.BlockSpec((tt, D_MODEL), lambda i, *_: (i, 0)),
            pl.BlockSpec((1, 8, tt), lambda i, *_: (i, 0, 0)),
            pl.BlockSpec(memory_space=pl.ANY),
        ],
        out_specs=pl.BlockSpec(memory_space=pl.ANY),
        scratch_shapes=[pltpu.VMEM((LOCAL_ROWS, D_MODEL), BF16), pltpu.SemaphoreType.DMA(())],
    )
    return pl.pallas_call(
        _dispatch_kernel,
        grid_spec=grid_spec,
        out_shape=jax.ShapeDtypeStruct(xs_init.shape, BF16),
        input_output_aliases={5: 0},
        compiler_params=pltpu.CompilerParams(
            dimension_semantics=("arbitrary",), vmem_limit_bytes=VMEM_LIMIT),
        name="dispatch",
    )(loff, pcnt, gstart, h2, pos, xs_init)


def _expert_kernel(te_s, nu_s, xs_ref, wgu_ref, bgu_ref, wd_ref, bd_ref, ys_ref):
    i = pl.program_id(0)

    @pl.when(i < nu_s[0])
    def _():
        gu = jnp.dot(xs_ref[...], wgu_ref[0], preferred_element_type=F32) + bgu_ref[0]
        g = jnp.minimum(gu[:, :D_EXPERT], SWIGLU_LIMIT)
        u = jnp.clip(gu[:, D_EXPERT:], -SWIGLU_LIMIT, SWIGLU_LIMIT)
        glu = g * jax.nn.sigmoid(SWIGLU_ALPHA * g)
        a = ((u + 1.0) * glu).astype(BF16)
        y = jnp.dot(a, wd_ref[0], preferred_element_type=F32) + bd_ref[0]
        ys_ref[...] = y.astype(BF16)

    @pl.when(i >= nu_s[0])
    def _():
        ys_ref[...] = jnp.zeros_like(ys_ref)


def _experts(tile_expert, n_used, xs, wgu, bgu, wd, bd):
    n_rows = xs.shape[0]
    tm = EXPERT_TILE
    live = lambda i, te, nu: jnp.minimum(i, nu[0] - 1)
    grid_spec = pltpu.PrefetchScalarGridSpec(
        num_scalar_prefetch=2,
        grid=(n_rows // tm,),
        in_specs=[
            pl.BlockSpec((tm, D_MODEL), lambda i, te, nu: (live(i, te, nu), 0)),
            pl.BlockSpec((1, D_MODEL, 2 * D_EXPERT), lambda i, te, nu: (te[live(i, te, nu)], 0, 0)),
            pl.BlockSpec((1, 1, 2 * D_EXPERT), lambda i, te, nu: (te[live(i, te, nu)], 0, 0)),
            pl.BlockSpec((1, D_EXPERT, D_MODEL), lambda i, te, nu: (te[live(i, te, nu)], 0, 0)),
            pl.BlockSpec((1, 1, D_MODEL), lambda i, te, nu: (te[live(i, te, nu)], 0, 0)),
        ],
        out_specs=pl.BlockSpec((tm, D_MODEL), lambda i, te, nu: (i, 0)),
    )
    return pl.pallas_call(
        _expert_kernel,
        grid_spec=grid_spec,
        out_shape=jax.ShapeDtypeStruct((n_rows, D_MODEL), BF16),
        compiler_params=pltpu.CompilerParams(
            dimension_semantics=("arbitrary",), vmem_limit_bytes=VMEM_LIMIT),
        name="experts",
    )(tile_expert, n_used, xs, wgu, bgu, wd, bd)


def _combine_kernel(loff_s, pcnt_s, gstart_s, x1_ref, post_ref, gatet_ref, p_ref, gple_ref, wg_ref, wp_ref,
                    gfin_ref, ys_ref, out_ref, loc_scr, sem):
    i = pl.program_id(0)
    tt = TOKEN_TILE

    @pl.when(i == 0)
    def _():
        loc_scr[...] = jnp.zeros_like(loc_scr)

    copies = _chunk_copies(loff_s, pcnt_s, gstart_s, i, loc_scr, ys_ref, sem, False)
    for cond, cp in copies:
        pl.when(cond)(cp.start)
    for cond, cp in copies:
        pl.when(cond)(cp.wait)

    post = post_ref[...]
    gatet = gatet_ref[...]
    moe = jnp.zeros((tt, D_MODEL), F32)
    for jb in range(LOCAL_ROWS // GATHER_BLOCK):
        jio = lax.broadcasted_iota(I32, (tt, GATHER_BLOCK), 1) + jb * GATHER_BLOCK
        w = jnp.zeros((tt, GATHER_BLOCK), F32)
        for k in range(TOP_K):
            w = w + jnp.where(jio == post[:, k:k + 1], gatet[:, k:k + 1], 0.0)
        hi = w.astype(BF16)
        lo = (w - hi.astype(F32)).astype(BF16)
        y = loc_scr[pl.ds(jb * GATHER_BLOCK, GATHER_BLOCK), :]
        moe = moe + jnp.dot(hi, y, preferred_element_type=F32) + jnp.dot(lo, y, preferred_element_type=F32)

    x2 = x1_ref[...] + moe
    gate = jax.nn.sigmoid(jnp.dot(_rms(x2, gple_ref[...]).astype(BF16), wg_ref[...], preferred_element_type=F32))
    emb = jnp.dot(p_ref[...].astype(BF16), wp_ref[...], preferred_element_type=F32)
    out_ref[...] = _rms(x2 + emb * gate, gfin_ref[...])


def _combine(loff, pcnt, gstart, x1, post, gatet, p2d, gple, wg, wp, gfin, ys):
    t = x1.shape[0]
    tt = TOKEN_TILE
    row = lambda i, *_: (i, 0)
    const2 = lambda i, *_: (0, 0)
    grid_spec = pltpu.PrefetchScalarGridSpec(
        num_scalar_prefetch=3,
        grid=(t // tt,),
        in_specs=[
            pl.BlockSpec((tt, D_MODEL), row),
            pl.BlockSpec((tt, 8), row),
            pl.BlockSpec((tt, 8), row),
            pl.BlockSpec((tt, PLE_DIM), row),
            pl.BlockSpec((1, D_MODEL), const2),
            pl.BlockSpec((D_MODEL, D_MODEL), const2),
            pl.BlockSpec((PLE_DIM, D_MODEL), const2),
            pl.BlockSpec((1, D_MODEL), const2),
            pl.BlockSpec(memory_space=pl.ANY),
        ],
        out_specs=pl.BlockSpec((tt, D_MODEL), row),
        scratch_shapes=[pltpu.VMEM((LOCAL_ROWS, D_MODEL), BF16), pltpu.SemaphoreType.DMA(())],
    )
    return pl.pallas_call(
        _combine_kernel,
        grid_spec=grid_spec,
        out_shape=jax.ShapeDtypeStruct((t, D_MODEL), F32),
        compiler_params=pltpu.CompilerParams(
            dimension_semantics=("arbitrary",), vmem_limit_bytes=VMEM_LIMIT),
        name="combine",
    )(loff, pcnt, gstart, x1, post, gatet, p2d, gple, wg, wp, gfin, ys)


def _to_sub(a, dil):
    if dil == 1:
        return a
    b, s, c = a.shape
    return a.reshape(b, s // dil, dil, c).transpose(0, 2, 1, 3).reshape(b, s, c)


def _from_sub(a, dil):
    if dil == 1:
        return a
    b, s, c = a.shape
    return a.reshape(b, dil, s // dil, c).transpose(0, 2, 1, 3).reshape(b, s, c)


def _routing_tables(pc16):
    pcnt = pc16 * ROW_ALIGN
    loff = jnp.cumsum(pcnt, axis=1) - pcnt
    seg = jnp.sum(pcnt, axis=0)
    segpad = (seg + EXPERT_TILE - 1) // EXPERT_TILE * EXPERT_TILE
    seg_end = jnp.cumsum(segpad)
    ebase = seg_end - segpad
    gstart = ebase[None, :] + jnp.cumsum(pcnt, axis=0) - pcnt
    return pcnt, loff, gstart, seg_end


def kernel(x, p, g_mix, w_in, w_pool, pool_scale, rel_bias, w_out, g_ffn, w_router, b_router, w_gate_up,
           b_gate_up, w_down, b_down, g_ple, w_ple_gate, w_ple_proj, g_final):
    b, s, d = x.shape
    t = b * s
    x2d = x.reshape(t, d)
    for layer in range(w_in.shape[0]):
        u, qkv = _inproj(x2d, g_mix[layer][None], w_in[layer].astype(BF16))
        qkv = qkv.reshape(b, s, -1)
        o_list, lse_list = [], []
        for window, dil in DILATED_BRANCHES:
            tab = _bias_table(rel_bias, window, dil)
            o, lse = _attn_branch(_to_sub(qkv, dil), tab, s // dil // ATTN_BLOCK)
            o_list.append(_from_sub(o, dil).reshape(t, ATTN_WIDTH))
            lse_list.append(_from_sub(lse, dil).reshape(t, N_HEADS))
        brt = jnp.broadcast_to(b_router[layer][:, None], (N_EXPERTS, 128))
        x1, h2, pos, gates, pc = _mixout(
            x2d, u, o_list, lse_list, w_pool[layer].astype(BF16), pool_scale[layer][None],
            w_out[layer].astype(BF16), g_ffn[layer][None], w_router[layer].T, brt, s)

        pc16 = pc[:, :, 0].astype(I32)
        pcnt, loff, gstart, seg_end = _routing_tables(pc16)
        n_rows = t * TOP_K + N_EXPERTS * (t // TOKEN_TILE) * (ROW_ALIGN - 1) + N_EXPERTS * (EXPERT_TILE - 1)
        n_rows = (n_rows + EXPERT_TILE - 1) // EXPERT_TILE * EXPERT_TILE
        n_tiles = n_rows // EXPERT_TILE
        tile_expert = jnp.minimum(
            jnp.searchsorted(seg_end, jnp.arange(n_tiles, dtype=I32) * EXPERT_TILE, side="right"),
            N_EXPERTS - 1).astype(I32)
        n_used = (seg_end[-1] // EXPERT_TILE).astype(I32)[None]
        xs = _dispatch(loff, pcnt, gstart, h2, pos, jnp.zeros((n_rows, d), BF16))
        ys = _experts(tile_expert, n_used, xs, w_gate_up[layer].astype(BF16), b_gate_up[layer][:, None, :],
                      w_down[layer].astype(BF16), b_down[layer][:, None, :])

        post = pos.transpose(0, 2, 1).reshape(t, 8)
        gatet = gates.transpose(0, 2, 1).reshape(t, 8)
        assert layer == w_in.shape[0] - 1, "single-layer pipeline: the final norm is fused into combine"
        x2d = _combine(loff, pcnt, gstart, x1, post, gatet, p[layer].reshape(t, PLE_DIM), g_ple[layer][None],
                       w_ple_gate[layer].astype(BF16), w_ple_proj[layer].astype(BF16), g_final[None], ys)
    return x2d.reshape(b, s, d)
```

```python
import functools
import math

import jax
import jax.numpy as jnp
from jax import lax
from jax.experimental import pallas as pl
from jax.experimental.pallas import tpu as pltpu

F32 = jnp.float32
BF16 = jnp.bfloat16
I32 = jnp.int32

D_MODEL = 1024
POOL_WIDTH = 512
POOL_WINDOWS = (2, 4, 8, 16)
POOL_GROUP_DIM = 128
ATTN_WIDTH = 512
QKV_WIDTH = 3 * ATTN_WIDTH
HEAD_DIM = 64
N_HEADS = 8
DILATED_BRANCHES = ((128, 1), (512, 4), (2048, 16))
ATTN_BLOCK = 128
N_REL_BUCKETS = 32
REL_MAX_EXACT = 16
REL_MAX_DISTANCE = 2048
N_EXPERTS = 32
TOP_K = 4
D_EXPERT = 1024
SWIGLU_LIMIT = 7.0
SWIGLU_ALPHA = 1.702
PLE_DIM = 256
NORM_EPS = 1e-6
NEG_INF = -1e30

LANES = 128
POOL_HALO = 16
TOKEN_TILE = 512
ROW_ALIGN = 16
EXPERT_TILE = 512
LOCAL_ROWS = TOKEN_TILE * TOP_K + N_EXPERTS * ROW_ALIGN
SORT_BLOCK = 256
GATHER_BLOCK = 512
ATTN_STEP = ((1, 4), (1, 4), (2, 2))
VMEM_LIMIT = 56 * 1024 * 1024


def _rms(x, g):
    return x * lax.rsqrt(jnp.mean(x * x, axis=-1, keepdims=True) + NORM_EPS) * g


def _inproj_kernel(x_ref, g_ref, w_ref, u_ref, nat_ref, d4_ref, d16_ref, z_scr):
    h = _rms(x_ref[...], g_ref[...]).astype(BF16)
    z = jnp.dot(h, w_ref[...], preferred_element_type=F32)
    u_ref[...] = z[:, :POOL_WIDTH]
    nat_ref[...] = z[:, POOL_WIDTH:].astype(BF16)
    n_slab = QKV_WIDTH // LANES
    for c in range(n_slab):
        z_scr[c] = z[:, POOL_WIDTH + c * LANES:POOL_WIDTH + (c + 1) * LANES]
    for dil, ref in ((4, d4_ref), (16, d16_ref)):
        rows = TOKEN_TILE // dil
        for r in range(dil):
            for c in range(n_slab):
                ref[:, pl.ds(r * QKV_WIDTH + c * LANES, LANES)] = (
                    z_scr[c, pl.ds(r, rows, stride=dil), :].astype(BF16))


def _inproj(x2d, g, w_bf16):
    t = x2d.shape[0]
    in_w = w_bf16.shape[1]
    tt = TOKEN_TILE
    return pl.pallas_call(
        _inproj_kernel,
        grid=(t // tt,),
        in_specs=[
            pl.BlockSpec((tt, D_MODEL), lambda i: (i, 0)),
            pl.BlockSpec((1, D_MODEL), lambda i: (0, 0)),
            pl.BlockSpec((D_MODEL, in_w), lambda i: (0, 0)),
        ],
        out_specs=[
            pl.BlockSpec((tt, POOL_WIDTH), lambda i: (i, 0)),
            pl.BlockSpec((tt, QKV_WIDTH), lambda i: (i, 0)),
            pl.BlockSpec((tt // 4, 4 * QKV_WIDTH), lambda i: (i, 0)),
            pl.BlockSpec((tt // 16, 16 * QKV_WIDTH), lambda i: (i, 0)),
        ],
        out_shape=[
            jax.ShapeDtypeStruct((t, POOL_WIDTH), F32),
            jax.ShapeDtypeStruct((t, QKV_WIDTH), BF16),
            jax.ShapeDtypeStruct((t // 4, 4 * QKV_WIDTH), BF16),
            jax.ShapeDtypeStruct((t // 16, 16 * QKV_WIDTH), BF16),
        ],
        scratch_shapes=[pltpu.VMEM((QKV_WIDTH // LANES, tt, LANES), F32)],
        compiler_params=pltpu.CompilerParams(
            dimension_semantics=("parallel",), vmem_limit_bytes=VMEM_LIMIT),
        name="inproj",
    )(x2d, g, w_bf16)


def _attn_kernel(main_ref, prev_ref, tab_ref, o_ref, lse_ref, k_scr, vt_scr, *, rq, nq):
    blk = ATTN_BLOCK
    first_group = pl.program_id(2) == 0
    nt = (((1,), (1,)), ((), ()))

    for ri in range(rq):
        base = ri * QKV_WIDTH
        k_scr[0:blk, :] = prev_ref[:, pl.ds(base + ATTN_WIDTH, ATTN_WIDTH)]
        k_scr[blk:, :] = main_ref[:, pl.ds(base + ATTN_WIDTH, ATTN_WIDTH)]
        vt_scr[:, 0:blk] = prev_ref[:, pl.ds(base + 2 * ATTN_WIDTH, ATTN_WIDTH)].astype(F32).T.astype(BF16)
        for j in range(nq):
            vt_scr[:, pl.ds((j + 1) * blk, blk)] = (
                main_ref[pl.ds(j * blk, blk), pl.ds(base + 2 * ATTN_WIDTH, ATTN_WIDTH)].astype(F32).T.astype(BF16))
        for j in range(nq):
            rows = pl.ds(j * blk, blk)
            keys = pl.ds(j * blk, 2 * blk)
            scores = []
            for h in range(N_HEADS):
                q = main_ref[rows, pl.ds(base + h * HEAD_DIM, HEAD_DIM)] * 0.125
                k2 = k_scr[keys, pl.ds(h * HEAD_DIM, HEAD_DIM)]
                scores.append(lax.dot_general(k2, q, nt, preferred_element_type=F32))
            probs, inv_l, lse_parts = [], [], []
            for h in range(N_HEADS):
                s = scores[h] + tab_ref[h]
                if j == 0:
                    pen = jnp.where(first_group, NEG_INF, 0.0).astype(F32)
                    s = jnp.concatenate([s[:blk] + pen, s[blk:]], axis=0)
                m = jnp.max(s, axis=0, keepdims=True)
                p = jnp.exp(s - m)
                l = jnp.sum(p, axis=0, keepdims=True)
                probs.append(p.astype(BF16))
                inv_l.append(1.0 / l)
                lse_parts.append(m + jnp.log(l))
            o_parts = []
            for h in range(N_HEADS):
                vt2 = vt_scr[pl.ds(h * HEAD_DIM, HEAD_DIM), keys]
                o_parts.append(jnp.dot(vt2, probs[h], preferred_element_type=F32) * inv_l[h])
            o_t = jnp.concatenate(o_parts, axis=0)
            o_ref[rows, pl.ds(ri * ATTN_WIDTH, ATTN_WIDTH)] = o_t.T.astype(BF16)
            lse_ref[ri, :, rows] = jnp.concatenate(lse_parts, axis=0)


def _attn_branch(qkv_d, tab, batch, seq, dil, rq, nq):
    sub = seq // dil
    nb = sub // ATTN_BLOCK
    groups = nb // nq
    return pl.pallas_call(
        functools.partial(_attn_kernel, rq=rq, nq=nq),
        grid=(batch, dil // rq, groups),
        in_specs=[
            pl.BlockSpec((nq * ATTN_BLOCK, rq * QKV_WIDTH), lambda b, r, g: (b * groups + g, r)),
            pl.BlockSpec((ATTN_BLOCK, rq * QKV_WIDTH), lambda b, r, g: (b * nb + jnp.maximum(g * nq - 1, 0), r)),
            pl.BlockSpec((N_HEADS, 2 * ATTN_BLOCK, ATTN_BLOCK), lambda b, r, g: (0, 0, 0)),
        ],
        out_specs=[
            pl.BlockSpec((nq * ATTN_BLOCK, rq * ATTN_WIDTH), lambda b, r, g: (b * groups + g, r)),
            pl.BlockSpec((None, rq, N_HEADS, nq * ATTN_BLOCK), lambda b, r, g: (b, r, 0, g)),
        ],
        out_shape=[
            jax.ShapeDtypeStruct((batch * sub, dil * ATTN_WIDTH), BF16),
            jax.ShapeDtypeStruct((batch, dil, N_HEADS, sub), F32),
        ],
        scratch_shapes=[
            pltpu.VMEM(((nq + 1) * ATTN_BLOCK, ATTN_WIDTH), BF16),
            pltpu.VMEM((ATTN_WIDTH, (nq + 1) * ATTN_BLOCK), BF16),
        ],
        compiler_params=pltpu.CompilerParams(
            dimension_semantics=("parallel", "parallel", "parallel"), vmem_limit_bytes=VMEM_LIMIT),
        name="attn",
    )(qkv_d, qkv_d, tab)


def _t5_bucket(dist):
    n = jnp.maximum(dist, 1).astype(F32)
    large = REL_MAX_EXACT + (jnp.log(n / REL_MAX_EXACT) / math.log(REL_MAX_DISTANCE / REL_MAX_EXACT)
                             * (N_REL_BUCKETS - REL_MAX_EXACT)).astype(I32)
    large = jnp.minimum(large, N_REL_BUCKETS - 1)
    return jnp.where(dist < REL_MAX_EXACT, dist, large)


def _shifted_rows(w, n):
    lead = w.shape[:-1]
    width = w.shape[-1]
    flat = jnp.tile(w, (1,) * len(lead) + (n + 1,))[..., :n * (width + 1)]
    return flat.reshape(lead + (n, width + 1))[..., :n]


def _bias_tables(rel_bias, window, dil):
    blk = ATTN_BLOCK
    assert window // dil == blk
    f = rel_bias[_t5_bucket(jnp.arange(blk + 1) * dil)].T.astype(F32)
    neg = jnp.full((N_HEADS, blk - 1), NEG_INF, F32)
    neg1 = jnp.full((N_HEADS, 1), NEG_INF, F32)
    w_prev = jnp.concatenate([neg, f[:, :0:-1], neg1], axis=1)
    prev = _shifted_rows(w_prev, blk)[:, :, ::-1]
    w_cur = jnp.concatenate([neg, f[:, :blk], neg1], axis=1)
    cur = _shifted_rows(w_cur, blk)[:, ::-1, :]
    return jnp.concatenate([prev, cur], axis=1)


def _mixout_kernel(x_ref, u_ref, uh_ref, o1_ref, o4_ref, o16_ref, l1_ref, l4_ref, l16_ref,
                   wpool_ref, pscale_ref, wout_ref, gffn_ref, wrt_ref, brt_ref,
                   x1_ref, h2_ref, pos_ref, gate_ref, pc_ref,
                   ext_scr, mix_scr, *, tiles_per_seq):
    i = pl.program_id(0)
    tt = TOKEN_TILE
    seq_tile = i % tiles_per_seq

    halo = uh_ref[...]
    ext_scr[0:POOL_HALO, :] = jnp.where(seq_tile == 0, jnp.zeros_like(halo), halo)
    ext_scr[POOL_HALO:, :] = u_ref[...]
    tpos = seq_tile * tt + lax.broadcasted_iota(I32, (tt, 1), 0)
    for gi, w in enumerate(POOL_WINDOWS):
        cols = pl.ds(gi * POOL_GROUP_DIM, POOL_GROUP_DIM)
        tok = ext_scr[pl.ds(POOL_HALO, tt), cols]
        acc = tok
        for j in range(1, w):
            acc = acc + ext_scr[pl.ds(POOL_HALO - j, tt), cols]
        cnt = jnp.minimum(tpos + 1, w).astype(F32)
        pooled = (acc / cnt - tok).astype(BF16)
        mixed = jnp.dot(pooled, wpool_ref[gi], preferred_element_type=F32) * pscale_ref[:, cols]
        mix_scr[:, cols] = mixed.astype(BF16)

    l1, l4, l16 = l1_ref[...], l4_ref[...], l16_ref[...]
    lm = jnp.maximum(jnp.maximum(l1, l4), l16)
    e1, e4, e16 = jnp.exp(l1 - lm), jnp.exp(l4 - lm), jnp.exp(l16 - lm)
    inv = 1.0 / (e1 + e4 + e16)
    w1, w4, w16 = e1 * inv, e4 * inv, e16 * inv
    for h in range(N_HEADS):
        sl = pl.ds(h * HEAD_DIM, HEAD_DIM)
        o = (w1[:, h:h + 1] * o1_ref[:, sl].astype(F32) + w4[:, h:h + 1] * o4_ref[:, sl].astype(F32)
             + w16[:, h:h + 1] * o16_ref[:, sl].astype(F32))
        mix_scr[:, pl.ds(POOL_WIDTH + h * HEAD_DIM, HEAD_DIM)] = o.astype(BF16)

    x1 = x_ref[...] + jnp.dot(mix_scr[...], wout_ref[...], preferred_element_type=F32)
    x1_ref[...] = x1

    h2 = _rms(x1, gffn_ref[...])
    h2_ref[...] = h2.astype(BF16)
    logits = lax.dot_general(wrt_ref[...], h2, (((1,), (1,)), ((), ())),
                             precision=lax.Precision.HIGHEST, preferred_element_type=F32)
    logits = logits + brt_ref[:, 0:1]
    eio = lax.broadcasted_iota(I32, (N_EXPERTS, tt), 0)
    sel, val = [], []
    for _ in range(TOP_K):
        m = jnp.max(logits, axis=0, keepdims=True)
        idx = jnp.min(jnp.where(logits == m, eio, N_EXPERTS), axis=0, keepdims=True)
        sel.append(idx)
        val.append(m)
        logits = jnp.where(eio == idx, -jnp.inf, logits)
    ex = [jnp.exp(v - val[0]) for v in val]
    den = ex[0] + ex[1] + ex[2] + ex[3]
    gates = [e / den for e in ex]

    hot = [(eio == s) for s in sel]
    onehot = (hot[0] | hot[1] | hot[2] | hot[3]).astype(F32)
    ti = lax.broadcasted_iota(I32, (tt, tt), 0)
    tj = lax.broadcasted_iota(I32, (tt, tt), 1)
    before = (ti < tj).astype(BF16)
    rank = jnp.dot(onehot.astype(BF16), before, preferred_element_type=F32)
    cnt = jnp.sum(onehot, axis=1, keepdims=True)
    pc = jnp.floor((cnt + (ROW_ALIGN - 1)) * (1.0 / ROW_ALIGN))
    pcb = jnp.broadcast_to(pc, (N_EXPERTS, 128))
    pc_ref[0] = pcb
    pcm = jnp.concatenate([pcb, jnp.zeros((128 - N_EXPERTS, 128), F32)], axis=0).astype(BF16)
    li = lax.broadcasted_iota(I32, (N_EXPERTS, 128), 0)
    lj = lax.broadcasted_iota(I32, (N_EXPERTS, 128), 1)
    lower = (lj < li).astype(BF16)
    off = jnp.dot(lower, pcm, preferred_element_type=F32)[:, 0:1] * float(ROW_ALIGN)
    where_to = off + rank
    zero_i = jnp.zeros((1, tt), I32)
    zero_f = jnp.zeros((1, tt), F32)
    pos_rows = [jnp.sum(jnp.where(hk, where_to, 0.0), axis=0, keepdims=True).astype(I32) for hk in hot]
    pos_ref[0] = jnp.concatenate(pos_rows + [zero_i] * (8 - TOP_K), axis=0)
    gate_ref[0] = jnp.concatenate(gates + [zero_f] * (8 - TOP_K), axis=0)


def _mixout(x2d, u, o_list, lse_list, wpool, pscale, wout, gffn, wrt, brt, seq):
    t = x2d.shape[0]
    tt = TOKEN_TILE
    n_tt = t // tt
    halo_blocks = tt // POOL_HALO
    row = lambda i: (i, 0)
    const2 = lambda i: (0, 0)
    return pl.pallas_call(
        functools.partial(_mixout_kernel, tiles_per_seq=seq // tt),
        grid=(n_tt,),
        in_specs=[
            pl.BlockSpec((tt, D_MODEL), row),
            pl.BlockSpec((tt, POOL_WIDTH), row),
            pl.BlockSpec((POOL_HALO, POOL_WIDTH), lambda i: (jnp.maximum(i * halo_blocks - 1, 0), 0)),
            pl.BlockSpec((tt, ATTN_WIDTH), row), pl.BlockSpec((tt, ATTN_WIDTH), row), pl.BlockSpec((tt, ATTN_WIDTH), row),
            pl.BlockSpec((tt, N_HEADS), row), pl.BlockSpec((tt, N_HEADS), row), pl.BlockSpec((tt, N_HEADS), row),
            pl.BlockSpec((len(POOL_WINDOWS), POOL_GROUP_DIM, POOL_GROUP_DIM), lambda i: (0, 0, 0)),
            pl.BlockSpec((1, POOL_WIDTH), const2),
            pl.BlockSpec((D_MODEL, D_MODEL), const2),
            pl.BlockSpec((1, D_MODEL), const2),
            pl.BlockSpec((N_EXPERTS, D_MODEL), const2),
            pl.BlockSpec((N_EXPERTS, 128), const2),
        ],
        out_specs=[
            pl.BlockSpec((tt, D_MODEL), row),
            pl.BlockSpec((tt, D_MODEL), row),
            pl.BlockSpec((1, 8, tt), lambda i: (i, 0, 0)),
            pl.BlockSpec((1, 8, tt), lambda i: (i, 0, 0)),
            pl.BlockSpec((1, N_EXPERTS, 128), lambda i: (i, 0, 0)),
        ],
        out_shape=[
            jax.ShapeDtypeStruct((t, D_MODEL), F32),
            jax.ShapeDtypeStruct((t, D_MODEL), BF16),
            jax.ShapeDtypeStruct((n_tt, 8, tt), I32),
            jax.ShapeDtypeStruct((n_tt, 8, tt), F32),
            jax.ShapeDtypeStruct((n_tt, N_EXPERTS, 128), F32),
        ],
        scratch_shapes=[
            pltpu.VMEM((tt + POOL_HALO, POOL_WIDTH), F32),
            pltpu.VMEM((tt, D_MODEL), BF16),
        ],
        compiler_params=pltpu.CompilerParams(
            dimension_semantics=("parallel",), vmem_limit_bytes=VMEM_LIMIT),
        name="mixout",
    )(x2d, u, u, *o_list, *lse_list, wpool, pscale, wout, gffn, wrt, brt)


def _chunk_copies(loff_s, pcnt_s, gstart_s, i, local_ref, global_ref, sem, to_global):
    out = []
    for e in range(N_EXPERTS):
        n = pl.multiple_of(pcnt_s[i, e], ROW_ALIGN)
        lo = pl.multiple_of(loff_s[i, e], ROW_ALIGN)
        go = pl.multiple_of(gstart_s[i, e], ROW_ALIGN)
        loc = local_ref.at[pl.ds(lo, n)]
        glo = global_ref.at[pl.ds(go, n)]
        src, dst = (loc, glo) if to_global else (glo, loc)
        out.append((n > 0, pltpu.make_async_copy(src, dst, sem)))
    return out


def _dispatch_kernel(loff_s, pcnt_s, gstart_s, h2_ref, pos_ref, xs_in_ref, xs_ref, loc_scr, sem):
    del xs_in_ref
    i = pl.program_id(0)
    tt = TOKEN_TILE
    pos = pos_ref[0]
    h2 = h2_ref[...]
    for jb in range(LOCAL_ROWS // SORT_BLOCK):
        jio = lax.broadcasted_iota(I32, (SORT_BLOCK, tt), 0) + jb * SORT_BLOCK
        hit = (jio == pos[0:1]) | (jio == pos[1:2]) | (jio == pos[2:3]) | (jio == pos[3:4])
        onehot = jnp.where(hit, 1.0, 0.0).astype(BF16)
        loc_scr[pl.ds(jb * SORT_BLOCK, SORT_BLOCK), :] = jnp.dot(
            onehot, h2, preferred_element_type=F32).astype(BF16)
    copies = _chunk_copies(loff_s, pcnt_s, gstart_s, i, loc_scr, xs_ref, sem, True)
    for cond, cp in copies:
        pl.when(cond)(cp.start)
    for cond, cp in copies:
        pl.when(cond)(cp.wait)


def _dispatch(loff, pcnt, gstart, h2, pos, xs_init):
    t = h2.shape[0]
    tt = TOKEN_TILE
    grid_spec = pltpu.PrefetchScalarGridSpec(
        num_scalar_prefetch=3,
        grid=(t // tt,),
        in_specs=[
            pl.BlockSpec((tt, D_MODEL), lambda i, *_: (i, 0)),
            pl.BlockSpec((1, 8, tt), lambda i, *_: (i, 0, 0)),
            pl.BlockSpec(memory_space=pl.ANY),
        ],
        out_specs=pl.BlockSpec(memory_space=pl.ANY),
        scratch_shapes=[pltpu.VMEM((LOCAL_ROWS, D_MODEL), BF16), pltpu.SemaphoreType.DMA(())],
    )
    return pl.pallas_call(
        _dispatch_kernel,
        grid_spec=grid_spec,
        out_shape=jax.ShapeDtypeStruct(xs_init.shape, BF16),
        input_output_aliases={5: 0},
        compiler_params=pltpu.CompilerParams(
            dimension_semantics=("arbitrary",), vmem_limit_bytes=VMEM_LIMIT),
        name="dispatch",
    )(loff, pcnt, gstart, h2, pos, xs_init)


def _expert_kernel(te_s, nu_s, xs_ref, wgu_ref, bgu_ref, wd_ref, bd_ref, ys_ref, wgu_bf, wd_bf):
    i = pl.program_id(0)
    live = i < nu_s[0]
    new_expert = (i == 0) | (te_s[i] != te_s[jnp.maximum(i - 1, 0)])

    @pl.when(live & new_expert)
    def _():
        wgu_bf[...] = wgu_ref[0].astype(BF16)
        wd_bf[...] = wd_ref[0].astype(BF16)

    @pl.when(live)
    def _():
        gu = jnp.dot(xs_ref[...], wgu_bf[...], preferred_element_type=F32) + bgu_ref[0]
        g = jnp.minimum(gu[:, :D_EXPERT], SWIGLU_LIMIT)
        u = jnp.clip(gu[:, D_EXPERT:], -SWIGLU_LIMIT, SWIGLU_LIMIT)
        glu = g * jax.nn.sigmoid(SWIGLU_ALPHA * g)
        a = ((u + 1.0) * glu).astype(BF16)
        y = jnp.dot(a, wd_bf[...], preferred_element_type=F32) + bd_ref[0]
        ys_ref[...] = y.astype(BF16)

    @pl.when(jnp.logical_not(live))
    def _():
        ys_ref[...] = jnp.zeros_like(ys_ref)


def _experts(tile_expert, n_used, xs, wgu, bgu, wd, bd):
    n_rows = xs.shape[0]
    tm = EXPERT_TILE
    live = lambda i, te, nu: jnp.minimum(i, nu[0] - 1)
    grid_spec = pltpu.PrefetchScalarGridSpec(
        num_scalar_prefetch=2,
        grid=(n_rows // tm,),
        in_specs=[
            pl.BlockSpec((tm, D_MODEL), lambda i, te, nu: (live(i, te, nu), 0)),
            pl.BlockSpec((1, D_MODEL, 2 * D_EXPERT), lambda i, te, nu: (te[live(i, te, nu)], 0, 0)),
            pl.BlockSpec((1, 1, 2 * D_EXPERT), lambda i, te, nu: (te[live(i, te, nu)], 0, 0)),
            pl.BlockSpec((1, D_EXPERT, D_MODEL), lambda i, te, nu: (te[live(i, te, nu)], 0, 0)),
            pl.BlockSpec((1, 1, D_MODEL), lambda i, te, nu: (te[live(i, te, nu)], 0, 0)),
        ],
        out_specs=pl.BlockSpec((tm, D_MODEL), lambda i, te, nu: (i, 0)),
        scratch_shapes=[pltpu.VMEM((D_MODEL, 2 * D_EXPERT), BF16), pltpu.VMEM((D_EXPERT, D_MODEL), BF16)],
    )
    return pl.pallas_call(
        _expert_kernel,
        grid_spec=grid_spec,
        out_shape=jax.ShapeDtypeStruct((n_rows, D_MODEL), BF16),
        compiler_params=pltpu.CompilerParams(
            dimension_semantics=("arbitrary",), vmem_limit_bytes=VMEM_LIMIT),
        name="experts",
    )(tile_expert, n_used, xs, wgu, bgu, wd, bd)


def _combine_kernel(loff_s, pcnt_s, gstart_s, x1_ref, post_ref, gatet_ref, p_ref, gple_ref, wg_ref, wp_ref,
                    gfin_ref, ys_ref, out_ref, loc_scr, sem):
    i = pl.program_id(0)
    tt = TOKEN_TILE

    @pl.when(i == 0)
    def _():
        loc_scr[...] = jnp.zeros_like(loc_scr)

    copies = _chunk_copies(loff_s, pcnt_s, gstart_s, i, loc_scr, ys_ref, sem, False)
    for cond, cp in copies:
        pl.when(cond)(cp.start)
    for cond, cp in copies:
        pl.when(cond)(cp.wait)

    post = post_ref[...]
    gatet = gatet_ref[...]
    moe = jnp.zeros((tt, D_MODEL), F32)
    for jb in range(LOCAL_ROWS // GATHER_BLOCK):
        jio = lax.broadcasted_iota(I32, (tt, GATHER_BLOCK), 1) + jb * GATHER_BLOCK
        w = jnp.zeros((tt, GATHER_BLOCK), F32)
        for k in range(TOP_K):
            w = w + jnp.where(jio == post[:, k:k + 1], gatet[:, k:k + 1], 0.0)
        hi = w.astype(BF16)
        lo = (w - hi.astype(F32)).astype(BF16)
        y = loc_scr[pl.ds(jb * GATHER_BLOCK, GATHER_BLOCK), :]
        moe = moe + jnp.dot(hi, y, preferred_element_type=F32) + jnp.dot(lo, y, preferred_element_type=F32)

    x2 = x1_ref[...] + moe
    gate = jax.nn.sigmoid(jnp.dot(_rms(x2, gple_ref[...]).astype(BF16), wg_ref[...], preferred_element_type=F32))
    emb = jnp.dot(p_ref[...].astype(BF16), wp_ref[...], preferred_element_type=F32)
    out_ref[...] = _rms(x2 + emb * gate, gfin_ref[...])


def _combine(loff, pcnt, gstart, x1, post, gatet, p2d, gple, wg, wp, gfin, ys):
    t = x1.shape[0]
    tt = TOKEN_TILE
    row = lambda i, *_: (i, 0)
    const2 = lambda i, *_: (0, 0)
    grid_spec = pltpu.PrefetchScalarGridSpec(
        num_scalar_prefetch=3,
        grid=(t // tt,),
        in_specs=[
            pl.BlockSpec((tt, D_MODEL), row),
            pl.BlockSpec((tt, 8), row),
            pl.BlockSpec((tt, 8), row),
            pl.BlockSpec((tt, PLE_DIM), row),
            pl.BlockSpec((1, D_MODEL), const2),
            pl.BlockSpec((D_MODEL, D_MODEL), const2),
            pl.BlockSpec((PLE_DIM, D_MODEL), const2),
            pl.BlockSpec((1, D_MODEL), const2),
            pl.BlockSpec(memory_space=pl.ANY),
        ],
        out_specs=pl.BlockSpec((tt, D_MODEL), row),
        scratch_shapes=[pltpu.VMEM((LOCAL_ROWS, D_MODEL), BF16), pltpu.SemaphoreType.DMA(())],
    )
    return pl.pallas_call(
        _combine_kernel,
        grid_spec=grid_spec,
        out_shape=jax.ShapeDtypeStruct((t, D_MODEL), F32),
        compiler_params=pltpu.CompilerParams(
            dimension_semantics=("arbitrary",), vmem_limit_bytes=VMEM_LIMIT),
        name="combine",
    )(loff, pcnt, gstart, x1, post, gatet, p2d, gple, wg, wp, gfin, ys)


def _routing_tables(pc16):
    pcnt = pc16 * ROW_ALIGN
    loff = jnp.cumsum(pcnt, axis=1) - pcnt
    seg = jnp.sum(pcnt, axis=0)
    segpad = (seg + EXPERT_TILE - 1) // EXPERT_TILE * EXPERT_TILE
    seg_end = jnp.cumsum(segpad)
    ebase = seg_end - segpad
    gstart = ebase[None, :] + jnp.cumsum(pcnt, axis=0) - pcnt
    return pcnt, loff, gstart, seg_end


def kernel(x, p, g_mix, w_in, w_pool, pool_scale, rel_bias, w_out, g_ffn, w_router, b_router, w_gate_up,
           b_gate_up, w_down, b_down, g_ple, w_ple_gate, w_ple_proj, g_final):
    b, s, d = x.shape
    t = b * s
    x2d = x.reshape(t, d)
    for layer in range(w_in.shape[0]):
        u, *qkv_by_dil = _inproj(x2d, g_mix[layer][None], w_in[layer].astype(BF16))
        o_list, lse_list = [], []
        for (window, dil), qkv_d, (rq, nq) in zip(DILATED_BRANCHES, qkv_by_dil, ATTN_STEP):
            o_d, lse_d = _attn_branch(qkv_d, _bias_tables(rel_bias, window, dil), b, s, dil, rq, nq)
            o_list.append(o_d.reshape(t, ATTN_WIDTH))
            lse_list.append(lse_d.transpose(0, 3, 1, 2).reshape(t, N_HEADS))
        brt = jnp.broadcast_to(b_router[layer][:, None], (N_EXPERTS, 128))
        x1, h2, pos, gates, pc = _mixout(
            x2d, u, o_list, lse_list, w_pool[layer].astype(BF16), pool_scale[layer][None],
            w_out[layer].astype(BF16), g_ffn[layer][None], w_router[layer].T, brt, s)

        pc16 = pc[:, :, 0].astype(I32)
        pcnt, loff, gstart, seg_end = _routing_tables(pc16)
        n_rows = t * TOP_K + N_EXPERTS * (t // TOKEN_TILE) * (ROW_ALIGN - 1) + N_EXPERTS * (EXPERT_TILE - 1)
        n_rows = (n_rows + EXPERT_TILE - 1) // EXPERT_TILE * EXPERT_TILE
        n_tiles = n_rows // EXPERT_TILE
        tile_start = jnp.arange(n_tiles, dtype=I32) * EXPERT_TILE
        tile_expert = jnp.minimum(jnp.sum(seg_end[None, :] <= tile_start[:, None], axis=1), N_EXPERTS - 1).astype(I32)
        n_used = (seg_end[-1] // EXPERT_TILE).astype(I32)[None]
        xs = _dispatch(loff, pcnt, gstart, h2, pos, jnp.zeros((n_rows, d), BF16))
        ys = _experts(tile_expert, n_used, xs, w_gate_up[layer], b_gate_up[layer][:, None, :],
                      w_down[layer], b_down[layer][:, None, :])

        post = pos.transpose(0, 2, 1).reshape(t, 8)
        gatet = gates.transpose(0, 2, 1).reshape(t, 8)
        assert layer == w_in.shape[0] - 1, "single-layer pipeline: the final norm is fused into combine"
        x2d = _combine(loff, pcnt, gstart, x1, post, gatet, p[layer].reshape(t, PLE_DIM), g_ple[layer][None],
                       w_ple_gate[layer].astype(BF16), w_ple_proj[layer].astype(BF16), g_final[None], ys)
    return x2d.reshape(b, s, d)
```

```python
import functools
import math

import jax
import jax.numpy as jnp
from jax import lax
from jax.experimental import pallas as pl
from jax.experimental.pallas import tpu as pltpu

F32 = jnp.float32
BF16 = jnp.bfloat16
I32 = jnp.int32

D_MODEL = 1024
POOL_WIDTH = 512
POOL_WINDOWS = (2, 4, 8, 16)
POOL_GROUP_DIM = 128
ATTN_WIDTH = 512
QKV_WIDTH = 3 * ATTN_WIDTH
HEAD_DIM = 64
N_HEADS = 8
DILATED_BRANCHES = ((128, 1), (512, 4), (2048, 16))
ATTN_BLOCK = 128
N_REL_BUCKETS = 32
REL_MAX_EXACT = 16
REL_MAX_DISTANCE = 2048
N_EXPERTS = 32
TOP_K = 4
D_EXPERT = 1024
SWIGLU_LIMIT = 7.0
SWIGLU_ALPHA = 1.702
PLE_DIM = 256
NORM_EPS = 1e-6
NEG_INF = -1e30

LANES = 128
POOL_HALO = 16
TOKEN_TILE = 512
ROW_ALIGN = 16
EXPERT_TILE = 512
EXPERT_CHUNK = 256
LOCAL_ROWS = TOKEN_TILE * TOP_K + N_EXPERTS * ROW_ALIGN
SORT_BLOCK = 256
GATHER_BLOCK = 512
ATTN_STEP = ((1, 4), (1, 4), (2, 2))
VMEM_LIMIT = 56 * 1024 * 1024


def _rms(x, g):
    return x * lax.rsqrt(jnp.mean(x * x, axis=-1, keepdims=True) + NORM_EPS) * g


def _inproj_kernel(x_ref, g_ref, w_ref, u_ref, nat_ref, d4_ref, d16_ref, z_scr):
    h = _rms(x_ref[...], g_ref[...]).astype(BF16)
    z = jnp.dot(h, w_ref[...], preferred_element_type=F32)
    u_ref[...] = z[:, :POOL_WIDTH]
    nat_ref[...] = z[:, POOL_WIDTH:].astype(BF16)
    n_slab = QKV_WIDTH // LANES
    for c in range(n_slab):
        z_scr[c] = z[:, POOL_WIDTH + c * LANES:POOL_WIDTH + (c + 1) * LANES]
    for dil, ref in ((4, d4_ref), (16, d16_ref)):
        rows = TOKEN_TILE // dil
        for r in range(dil):
            for c in range(n_slab):
                ref[:, pl.ds(r * QKV_WIDTH + c * LANES, LANES)] = (
                    z_scr[c, pl.ds(r, rows, stride=dil), :].astype(BF16))


def _inproj(x2d, g, w_bf16):
    t = x2d.shape[0]
    in_w = w_bf16.shape[1]
    tt = TOKEN_TILE
    return pl.pallas_call(
        _inproj_kernel,
        grid=(t // tt,),
        in_specs=[
            pl.BlockSpec((tt, D_MODEL), lambda i: (i, 0)),
            pl.BlockSpec((1, D_MODEL), lambda i: (0, 0)),
            pl.BlockSpec((D_MODEL, in_w), lambda i: (0, 0)),
        ],
        out_specs=[
            pl.BlockSpec((tt, POOL_WIDTH), lambda i: (i, 0)),
            pl.BlockSpec((tt, QKV_WIDTH), lambda i: (i, 0)),
            pl.BlockSpec((tt // 4, 4 * QKV_WIDTH), lambda i: (i, 0)),
            pl.BlockSpec((tt // 16, 16 * QKV_WIDTH), lambda i: (i, 0)),
        ],
        out_shape=[
            jax.ShapeDtypeStruct((t, POOL_WIDTH), F32),
            jax.ShapeDtypeStruct((t, QKV_WIDTH), BF16),
            jax.ShapeDtypeStruct((t // 4, 4 * QKV_WIDTH), BF16),
            jax.ShapeDtypeStruct((t // 16, 16 * QKV_WIDTH), BF16),
        ],
        scratch_shapes=[pltpu.VMEM((QKV_WIDTH // LANES, tt, LANES), F32)],
        compiler_params=pltpu.CompilerParams(
            dimension_semantics=("parallel",), vmem_limit_bytes=VMEM_LIMIT),
        name="inproj",
    )(x2d, g, w_bf16)


def _attn_kernel(main_ref, prev_ref, tab_ref, o_ref, lse_ref, k_scr, vt_scr, *, rq, nq):
    blk = ATTN_BLOCK
    first_group = pl.program_id(2) == 0
    nt = (((1,), (1,)), ((), ()))

    for ri in range(rq):
        base = ri * QKV_WIDTH
        k_scr[0:blk, :] = prev_ref[:, pl.ds(base + ATTN_WIDTH, ATTN_WIDTH)]
        k_scr[blk:, :] = main_ref[:, pl.ds(base + ATTN_WIDTH, ATTN_WIDTH)]
        vt_scr[:, 0:blk] = prev_ref[:, pl.ds(base + 2 * ATTN_WIDTH, ATTN_WIDTH)].astype(F32).T.astype(BF16)
        for j in range(nq):
            vt_scr[:, pl.ds((j + 1) * blk, blk)] = (
                main_ref[pl.ds(j * blk, blk), pl.ds(base + 2 * ATTN_WIDTH, ATTN_WIDTH)].astype(F32).T.astype(BF16))
        for j in range(nq):
            rows = pl.ds(j * blk, blk)
            keys = pl.ds(j * blk, 2 * blk)
            scores = []
            for h in range(N_HEADS):
                q = main_ref[rows, pl.ds(base + h * HEAD_DIM, HEAD_DIM)] * 0.125
                k2 = k_scr[keys, pl.ds(h * HEAD_DIM, HEAD_DIM)]
                scores.append(lax.dot_general(k2, q, nt, preferred_element_type=F32))
            probs, inv_l, lse_parts = [], [], []
            for h in range(N_HEADS):
                s = scores[h] + tab_ref[h]
                if j == 0:
                    pen = jnp.where(first_group, NEG_INF, 0.0).astype(F32)
                    s = jnp.concatenate([s[:blk] + pen, s[blk:]], axis=0)
                m = jnp.max(s, axis=0, keepdims=True)
                p = jnp.exp(s - m)
                l = jnp.sum(p, axis=0, keepdims=True)
                probs.append(p.astype(BF16))
                inv_l.append(1.0 / l)
                lse_parts.append(m + jnp.log(l))
            o_parts = []
            for h in range(N_HEADS):
                vt2 = vt_scr[pl.ds(h * HEAD_DIM, HEAD_DIM), keys]
                o_parts.append(jnp.dot(vt2, probs[h], preferred_element_type=F32) * inv_l[h])
            o_t = jnp.concatenate(o_parts, axis=0)
            o_ref[rows, pl.ds(ri * ATTN_WIDTH, ATTN_WIDTH)] = o_t.T.astype(BF16)
            lse_ref[ri, :, rows] = jnp.concatenate(lse_parts, axis=0)


def _attn_branch(qkv_d, tab, batch, seq, dil, rq, nq):
    sub = seq // dil
    nb = sub // ATTN_BLOCK
    groups = nb // nq
    return pl.pallas_call(
        functools.partial(_attn_kernel, rq=rq, nq=nq),
        grid=(batch, dil // rq, groups),
        in_specs=[
            pl.BlockSpec((nq * ATTN_BLOCK, rq * QKV_WIDTH), lambda b, r, g: (b * groups + g, r)),
            pl.BlockSpec((ATTN_BLOCK, rq * QKV_WIDTH), lambda b, r, g: (b * nb + jnp.maximum(g * nq - 1, 0), r)),
            pl.BlockSpec((N_HEADS, 2 * ATTN_BLOCK, ATTN_BLOCK), lambda b, r, g: (0, 0, 0)),
        ],
        out_specs=[
            pl.BlockSpec((nq * ATTN_BLOCK, rq * ATTN_WIDTH), lambda b, r, g: (b * groups + g, r)),
            pl.BlockSpec((None, rq, N_HEADS, nq * ATTN_BLOCK), lambda b, r, g: (b, r, 0, g)),
        ],
        out_shape=[
            jax.ShapeDtypeStruct((batch * sub, dil * ATTN_WIDTH), BF16),
            jax.ShapeDtypeStruct((batch, dil, N_HEADS, sub), F32),
        ],
        scratch_shapes=[
            pltpu.VMEM(((nq + 1) * ATTN_BLOCK, ATTN_WIDTH), BF16),
            pltpu.VMEM((ATTN_WIDTH, (nq + 1) * ATTN_BLOCK), BF16),
        ],
        compiler_params=pltpu.CompilerParams(
            dimension_semantics=("parallel", "parallel", "parallel"), vmem_limit_bytes=VMEM_LIMIT),
        name="attn",
    )(qkv_d, qkv_d, tab)


def _t5_bucket(dist):
    n = jnp.maximum(dist, 1).astype(F32)
    large = REL_MAX_EXACT + (jnp.log(n / REL_MAX_EXACT) / math.log(REL_MAX_DISTANCE / REL_MAX_EXACT)
                             * (N_REL_BUCKETS - REL_MAX_EXACT)).astype(I32)
    large = jnp.minimum(large, N_REL_BUCKETS - 1)
    return jnp.where(dist < REL_MAX_EXACT, dist, large)


def _shifted_rows(w, n):
    lead = w.shape[:-1]
    width = w.shape[-1]
    flat = jnp.tile(w, (1,) * len(lead) + (n + 1,))[..., :n * (width + 1)]
    return flat.reshape(lead + (n, width + 1))[..., :n]


def _bias_tables(rel_bias, window, dil):
    blk = ATTN_BLOCK
    assert window // dil == blk
    f = rel_bias[_t5_bucket(jnp.arange(blk + 1) * dil)].T.astype(F32)
    neg = jnp.full((N_HEADS, blk - 1), NEG_INF, F32)
    neg1 = jnp.full((N_HEADS, 1), NEG_INF, F32)
    w_prev = jnp.concatenate([neg, f[:, :0:-1], neg1], axis=1)
    prev = _shifted_rows(w_prev, blk)[:, :, ::-1]
    w_cur = jnp.concatenate([neg, f[:, :blk], neg1], axis=1)
    cur = _shifted_rows(w_cur, blk)[:, ::-1, :]
    return jnp.concatenate([prev, cur], axis=1)


def _mixout_kernel(x_ref, u_ref, uh_ref, o1_ref, o4_ref, o16_ref, l1_ref, l4_ref, l16_ref,
                   wpool_ref, pscale_ref, wout_ref, gffn_ref, wrt_ref, brt_ref,
                   x1_ref, h2_ref, pos_ref, gate_ref, pc_ref,
                   ext_scr, mix_scr, *, tiles_per_seq):
    i = pl.program_id(0)
    tt = TOKEN_TILE
    seq_tile = i % tiles_per_seq

    halo = uh_ref[...]
    ext_scr[0:POOL_HALO, :] = jnp.where(seq_tile == 0, jnp.zeros_like(halo), halo)
    ext_scr[POOL_HALO:, :] = u_ref[...]
    tpos = seq_tile * tt + lax.broadcasted_iota(I32, (tt, 1), 0)
    for gi, w in enumerate(POOL_WINDOWS):
        cols = pl.ds(gi * POOL_GROUP_DIM, POOL_GROUP_DIM)
        tok = ext_scr[pl.ds(POOL_HALO, tt), cols]
        acc = tok
        for j in range(1, w):
            acc = acc + ext_scr[pl.ds(POOL_HALO - j, tt), cols]
        cnt = jnp.minimum(tpos + 1, w).astype(F32)
        pooled = (acc / cnt - tok).astype(BF16)
        mixed = jnp.dot(pooled, wpool_ref[gi], preferred_element_type=F32) * pscale_ref[:, cols]
        mix_scr[:, cols] = mixed.astype(BF16)

    l1, l4, l16 = l1_ref[...], l4_ref[...], l16_ref[...]
    lm = jnp.maximum(jnp.maximum(l1, l4), l16)
    e1, e4, e16 = jnp.exp(l1 - lm), jnp.exp(l4 - lm), jnp.exp(l16 - lm)
    inv = 1.0 / (e1 + e4 + e16)
    w1, w4, w16 = e1 * inv, e4 * inv, e16 * inv
    for h in range(N_HEADS):
        sl = pl.ds(h * HEAD_DIM, HEAD_DIM)
        o = (w1[:, h:h + 1] * o1_ref[:, sl].astype(F32) + w4[:, h:h + 1] * o4_ref[:, sl].astype(F32)
             + w16[:, h:h + 1] * o16_ref[:, sl].astype(F32))
        mix_scr[:, pl.ds(POOL_WIDTH + h * HEAD_DIM, HEAD_DIM)] = o.astype(BF16)

    x1 = x_ref[...] + jnp.dot(mix_scr[...], wout_ref[...], preferred_element_type=F32)
    x1_ref[...] = x1

    h2 = _rms(x1, gffn_ref[...])
    h2_ref[...] = h2.astype(BF16)
    logits = lax.dot_general(wrt_ref[...], h2, (((1,), (1,)), ((), ())),
                             precision=lax.Precision.HIGHEST, preferred_element_type=F32)
    logits = logits + brt_ref[:, 0:1]
    eio = lax.broadcasted_iota(I32, (N_EXPERTS, tt), 0)
    sel, val = [], []
    for _ in range(TOP_K):
        m = jnp.max(logits, axis=0, keepdims=True)
        idx = jnp.min(jnp.where(logits == m, eio, N_EXPERTS), axis=0, keepdims=True)
        sel.append(idx)
        val.append(m)
        logits = jnp.where(eio == idx, -jnp.inf, logits)
    ex = [jnp.exp(v - val[0]) for v in val]
    den = ex[0] + ex[1] + ex[2] + ex[3]
    gates = [e / den for e in ex]

    hot = [(eio == s) for s in sel]
    onehot = (hot[0] | hot[1] | hot[2] | hot[3]).astype(F32)
    ti = lax.broadcasted_iota(I32, (tt, tt), 0)
    tj = lax.broadcasted_iota(I32, (tt, tt), 1)
    before = (ti < tj).astype(BF16)
    rank = jnp.dot(onehot.astype(BF16), before, preferred_element_type=F32)
    cnt = jnp.sum(onehot, axis=1, keepdims=True)
    pc = jnp.floor((cnt + (ROW_ALIGN - 1)) * (1.0 / ROW_ALIGN))
    pcb = jnp.broadcast_to(pc, (N_EXPERTS, 128))
    pc_ref[0] = pcb
    pcm = jnp.concatenate([pcb, jnp.zeros((128 - N_EXPERTS, 128), F32)], axis=0).astype(BF16)
    li = lax.broadcasted_iota(I32, (N_EXPERTS, 128), 0)
    lj = lax.broadcasted_iota(I32, (N_EXPERTS, 128), 1)
    lower = (lj < li).astype(BF16)
    off = jnp.dot(lower, pcm, preferred_element_type=F32)[:, 0:1] * float(ROW_ALIGN)
    where_to = off + rank
    zero_i = jnp.zeros((1, tt), I32)
    zero_f = jnp.zeros((1, tt), F32)
    pos_rows = [jnp.sum(jnp.where(hk, where_to, 0.0), axis=0, keepdims=True).astype(I32) for hk in hot]
    pos_ref[0] = jnp.concatenate(pos_rows + [zero_i] * (8 - TOP_K), axis=0)
    gate_ref[0] = jnp.concatenate(gates + [zero_f] * (8 - TOP_K), axis=0)


def _mixout(x2d, u, o_list, lse_list, wpool, pscale, wout, gffn, wrt, brt, seq):
    t = x2d.shape[0]
    tt = TOKEN_TILE
    n_tt = t // tt
    halo_blocks = tt // POOL_HALO
    row = lambda i: (i, 0)
    const2 = lambda i: (0, 0)
    return pl.pallas_call(
        functools.partial(_mixout_kernel, tiles_per_seq=seq // tt),
        grid=(n_tt,),
        in_specs=[
            pl.BlockSpec((tt, D_MODEL), row),
            pl.BlockSpec((tt, POOL_WIDTH), row),
            pl.BlockSpec((POOL_HALO, POOL_WIDTH), lambda i: (jnp.maximum(i * halo_blocks - 1, 0), 0)),
            pl.BlockSpec((tt, ATTN_WIDTH), row), pl.BlockSpec((tt, ATTN_WIDTH), row), pl.BlockSpec((tt, ATTN_WIDTH), row),
            pl.BlockSpec((tt, N_HEADS), row), pl.BlockSpec((tt, N_HEADS), row), pl.BlockSpec((tt, N_HEADS), row),
            pl.BlockSpec((len(POOL_WINDOWS), POOL_GROUP_DIM, POOL_GROUP_DIM), lambda i: (0, 0, 0)),
            pl.BlockSpec((1, POOL_WIDTH), const2),
            pl.BlockSpec((D_MODEL, D_MODEL), const2),
            pl.BlockSpec((1, D_MODEL), const2),
            pl.BlockSpec((N_EXPERTS, D_MODEL), const2),
            pl.BlockSpec((N_EXPERTS, 128), const2),
        ],
        out_specs=[
            pl.BlockSpec((tt, D_MODEL), row),
            pl.BlockSpec((tt, D_MODEL), row),
            pl.BlockSpec((1, 8, tt), lambda i: (i, 0, 0)),
            pl.BlockSpec((1, 8, tt), lambda i: (i, 0, 0)),
            pl.BlockSpec((1, N_EXPERTS, 128), lambda i: (i, 0, 0)),
        ],
        out_shape=[
            jax.ShapeDtypeStruct((t, D_MODEL), F32),
            jax.ShapeDtypeStruct((t, D_MODEL), BF16),
            jax.ShapeDtypeStruct((n_tt, 8, tt), I32),
            jax.ShapeDtypeStruct((n_tt, 8, tt), F32),
            jax.ShapeDtypeStruct((n_tt, N_EXPERTS, 128), F32),
        ],
        scratch_shapes=[
            pltpu.VMEM((tt + POOL_HALO, POOL_WIDTH), F32),
            pltpu.VMEM((tt, D_MODEL), BF16),
        ],
        compiler_params=pltpu.CompilerParams(
            dimension_semantics=("parallel",), vmem_limit_bytes=VMEM_LIMIT),
        name="mixout",
    )(x2d, u, u, *o_list, *lse_list, wpool, pscale, wout, gffn, wrt, brt)


def _chunk_copies(loff_s, pcnt_s, gstart_s, i, local_ref, global_ref, sem, to_global):
    out = []
    for e in range(N_EXPERTS):
        n = pl.multiple_of(pcnt_s[i, e], ROW_ALIGN)
        lo = pl.multiple_of(loff_s[i, e], ROW_ALIGN)
        go = pl.multiple_of(gstart_s[i, e], ROW_ALIGN)
        loc = local_ref.at[pl.ds(lo, n)]
        glo = global_ref.at[pl.ds(go, n)]
        src, dst = (loc, glo) if to_global else (glo, loc)
        out.append((n > 0, pltpu.make_async_copy(src, dst, sem)))
    return out


def _dispatch_kernel(loff_s, pcnt_s, gstart_s, tail_s, h2_ref, pos_ref, xs_ref, loc_scr, zero_scr, sem, tail_sem,
                     spare_sem):
    i = pl.program_id(0)
    tt = TOKEN_TILE

    @pl.when(i == 0)
    def _():
        zero_scr[...] = jnp.zeros_like(zero_scr)
        tails = []
        for e in range(N_EXPERTS):
            n = pl.multiple_of(tail_s[1, e], ROW_ALIGN)
            start = pl.multiple_of(tail_s[0, e], ROW_ALIGN)
            tails.append((n > 0, pltpu.make_async_copy(zero_scr.at[pl.ds(0, n)], xs_ref.at[pl.ds(start, n)], tail_sem)))
        for cond, cp in tails:
            pl.when(cond)(cp.start)
        for cond, cp in tails:
            pl.when(cond)(cp.wait)

    def spare_tile(j):
        return pltpu.make_async_copy(
            zero_scr, xs_ref.at[pl.ds(pl.multiple_of(j * EXPERT_TILE, EXPERT_TILE), EXPERT_TILE)], spare_sem)

    n_tiles = xs_ref.shape[0] // EXPERT_TILE

    @pl.when(i == 0)
    def _():
        lax.fori_loop(tail_s[2, 0], n_tiles, lambda j, c: (spare_tile(j).start(), c)[1], 0)

    @pl.when(i == pl.num_programs(0) - 1)
    def _():
        lax.fori_loop(tail_s[2, 0], n_tiles, lambda j, c: (spare_tile(j).wait(), c)[1], 0)

    pos = pos_ref[0]
    h2 = h2_ref[...]
    for jb in range(LOCAL_ROWS // SORT_BLOCK):
        jio = lax.broadcasted_iota(I32, (SORT_BLOCK, tt), 0) + jb * SORT_BLOCK
        hit = (jio == pos[0:1]) | (jio == pos[1:2]) | (jio == pos[2:3]) | (jio == pos[3:4])
        onehot = jnp.where(hit, 1.0, 0.0).astype(BF16)
        loc_scr[pl.ds(jb * SORT_BLOCK, SORT_BLOCK), :] = jnp.dot(
            onehot, h2, preferred_element_type=F32).astype(BF16)
    copies = _chunk_copies(loff_s, pcnt_s, gstart_s, i, loc_scr, xs_ref, sem, True)
    for cond, cp in copies:
        pl.when(cond)(cp.start)
    for cond, cp in copies:
        pl.when(cond)(cp.wait)


def _dispatch(loff, pcnt, gstart, tails, h2, pos, n_rows):
    t = h2.shape[0]
    tt = TOKEN_TILE
    grid_spec = pltpu.PrefetchScalarGridSpec(
        num_scalar_prefetch=4,
        grid=(t // tt,),
        in_specs=[
            pl.BlockSpec((tt, D_MODEL), lambda i, *_: (i, 0)),
            pl.BlockSpec((1, 8, tt), lambda i, *_: (i, 0, 0)),
        ],
        out_specs=pl.BlockSpec(memory_space=pl.ANY),
        scratch_shapes=[
            pltpu.VMEM((LOCAL_ROWS, D_MODEL), BF16),
            pltpu.VMEM((EXPERT_TILE, D_MODEL), BF16),
            pltpu.SemaphoreType.DMA(()),
            pltpu.SemaphoreType.DMA(()),
            pltpu.SemaphoreType.DMA(()),
        ],
    )
    return pl.pallas_call(
        _dispatch_kernel,
        grid_spec=grid_spec,
        out_shape=jax.ShapeDtypeStruct((n_rows, D_MODEL), BF16),
        compiler_params=pltpu.CompilerParams(
            dimension_semantics=("arbitrary",), vmem_limit_bytes=VMEM_LIMIT),
        name="dispatch",
    )(loff, pcnt, gstart, tails, h2, pos)


def _expert_kernel(te_s, nu_s, xs_ref, wgu_ref, bgu_ref, wd_ref, bd_ref, ys_ref, wgu_bf, wd_bf):
    i = pl.program_id(0)
    live = i < nu_s[0]
    new_expert = (i == 0) | (te_s[i] != te_s[jnp.maximum(i - 1, 0)])

    @pl.when(live & new_expert)
    def _():
        wgu_bf[...] = wgu_ref[0].astype(BF16)
        wd_bf[...] = wd_ref[0].astype(BF16)

    @pl.when(live)
    def _():
        x = xs_ref[...]
        n_chunks = D_EXPERT // EXPERT_CHUNK

        def gate_up(c):
            gc = pl.ds(c * EXPERT_CHUNK, EXPERT_CHUNK)
            uc = pl.ds(D_EXPERT + c * EXPERT_CHUNK, EXPERT_CHUNK)
            return (jnp.dot(x, wgu_bf[:, gc], preferred_element_type=F32) + bgu_ref[0, :, gc],
                    jnp.dot(x, wgu_bf[:, uc], preferred_element_type=F32) + bgu_ref[0, :, uc])

        y = jnp.broadcast_to(bd_ref[0], (EXPERT_TILE, D_MODEL))
        g, u = gate_up(0)
        for c in range(n_chunks):
            nxt = gate_up(c + 1) if c + 1 < n_chunks else None
            g = jnp.minimum(g, SWIGLU_LIMIT)
            u = jnp.clip(u, -SWIGLU_LIMIT, SWIGLU_LIMIT)
            a = ((u + 1.0) * (g * jax.nn.sigmoid(SWIGLU_ALPHA * g))).astype(BF16)
            y = y + jnp.dot(a, wd_bf[pl.ds(c * EXPERT_CHUNK, EXPERT_CHUNK), :], preferred_element_type=F32)
            if nxt is not None:
                g, u = nxt
        ys_ref[...] = y.astype(BF16)

    @pl.when(jnp.logical_not(live))
    def _():
        ys_ref[...] = jnp.zeros_like(ys_ref)


def _experts(tile_expert, n_used, xs, wgu, bgu, wd, bd):
    n_rows = xs.shape[0]
    tm = EXPERT_TILE
    live = lambda i, te, nu: jnp.minimum(i, nu[0] - 1)
    grid_spec = pltpu.PrefetchScalarGridSpec(
        num_scalar_prefetch=2,
        grid=(n_rows // tm,),
        in_specs=[
            pl.BlockSpec((tm, D_MODEL), lambda i, te, nu: (live(i, te, nu), 0)),
            pl.BlockSpec((1, D_MODEL, 2 * D_EXPERT), lambda i, te, nu: (te[live(i, te, nu)], 0, 0)),
            pl.BlockSpec((1, 1, 2 * D_EXPERT), lambda i, te, nu: (te[live(i, te, nu)], 0, 0)),
            pl.BlockSpec((1, D_EXPERT, D_MODEL), lambda i, te, nu: (te[live(i, te, nu)], 0, 0)),
            pl.BlockSpec((1, 1, D_MODEL), lambda i, te, nu: (te[live(i, te, nu)], 0, 0)),
        ],
        out_specs=pl.BlockSpec((tm, D_MODEL), lambda i, te, nu: (i, 0)),
        scratch_shapes=[pltpu.VMEM((D_MODEL, 2 * D_EXPERT), BF16), pltpu.VMEM((D_EXPERT, D_MODEL), BF16)],
    )
    return pl.pallas_call(
        _expert_kernel,
        grid_spec=grid_spec,
        out_shape=jax.ShapeDtypeStruct((n_rows, D_MODEL), BF16),
        compiler_params=pltpu.CompilerParams(
            dimension_semantics=("arbitrary",), vmem_limit_bytes=VMEM_LIMIT),
        name="experts",
    )(tile_expert, n_used, xs, wgu, bgu, wd, bd)


def _combine_kernel(loff_s, pcnt_s, gstart_s, x1_ref, post_ref, gatet_ref, p_ref, gple_ref, wg_ref, wp_ref,
                    gfin_ref, ys_ref, out_ref, loc_scr, sem):
    i = pl.program_id(0)
    tt = TOKEN_TILE

    @pl.when(i == 0)
    def _():
        loc_scr[...] = jnp.zeros_like(loc_scr)

    copies = _chunk_copies(loff_s, pcnt_s, gstart_s, i, loc_scr, ys_ref, sem, False)
    for cond, cp in copies:
        pl.when(cond)(cp.start)
    for cond, cp in copies:
        pl.when(cond)(cp.wait)

    post = post_ref[...]
    gatet = gatet_ref[...]
    moe = jnp.zeros((tt, D_MODEL), F32)
    for jb in range(LOCAL_ROWS // GATHER_BLOCK):
        jio = lax.broadcasted_iota(I32, (tt, GATHER_BLOCK), 1) + jb * GATHER_BLOCK
        w = jnp.zeros((tt, GATHER_BLOCK), F32)
        for k in range(TOP_K):
            w = w + jnp.where(jio == post[:, k:k + 1], gatet[:, k:k + 1], 0.0)
        y = loc_scr[pl.ds(jb * GATHER_BLOCK, GATHER_BLOCK), :]
        moe = moe + jnp.dot(w.astype(BF16), y, preferred_element_type=F32)

    x2 = x1_ref[...] + moe
    gate = jax.nn.sigmoid(jnp.dot(_rms(x2, gple_ref[...]).astype(BF16), wg_ref[...], preferred_element_type=F32))
    emb = jnp.dot(p_ref[...].astype(BF16), wp_ref[...], preferred_element_type=F32)
    out_ref[...] = _rms(x2 + emb * gate, gfin_ref[...])


def _combine(loff, pcnt, gstart, x1, post, gatet, p2d, gple, wg, wp, gfin, ys):
    t = x1.shape[0]
    tt = TOKEN_TILE
    row = lambda i, *_: (i, 0)
    const2 = lambda i, *_: (0, 0)
    grid_spec = pltpu.PrefetchScalarGridSpec(
        num_scalar_prefetch=3,
        grid=(t // tt,),
        in_specs=[
            pl.BlockSpec((tt, D_MODEL), row),
            pl.BlockSpec((tt, 8), row),
            pl.BlockSpec((tt, 8), row),
            pl.BlockSpec((tt, PLE_DIM), row),
            pl.BlockSpec((1, D_MODEL), const2),
            pl.BlockSpec((D_MODEL, D_MODEL), const2),
            pl.BlockSpec((PLE_DIM, D_MODEL), const2),
            pl.BlockSpec((1, D_MODEL), const2),
            pl.BlockSpec(memory_space=pl.ANY),
        ],
        out_specs=pl.BlockSpec((tt, D_MODEL), row),
        scratch_shapes=[pltpu.VMEM((LOCAL_ROWS, D_MODEL), BF16), pltpu.SemaphoreType.DMA(())],
    )
    return pl.pallas_call(
        _combine_kernel,
        grid_spec=grid_spec,
        out_shape=jax.ShapeDtypeStruct((t, D_MODEL), F32),
        compiler_params=pltpu.CompilerParams(
            dimension_semantics=("arbitrary",), vmem_limit_bytes=VMEM_LIMIT),
        name="combine",
    )(loff, pcnt, gstart, x1, post, gatet, p2d, gple, wg, wp, gfin, ys)


def _routing_tables(pc16):
    pcnt = pc16 * ROW_ALIGN
    loff = jnp.cumsum(pcnt, axis=1) - pcnt
    seg = jnp.sum(pcnt, axis=0)
    segpad = (seg + EXPERT_TILE - 1) // EXPERT_TILE * EXPERT_TILE
    seg_end = jnp.cumsum(segpad)
    ebase = seg_end - segpad
    gstart = ebase[None, :] + jnp.cumsum(pcnt, axis=0) - pcnt
    tails = jnp.stack([ebase + seg, segpad - seg, jnp.broadcast_to(seg_end[-1] // EXPERT_TILE, seg.shape)])
    return pcnt, loff, gstart, tails, seg_end


def kernel(x, p, g_mix, w_in, w_pool, pool_scale, rel_bias, w_out, g_ffn, w_router, b_router, w_gate_up,
           b_gate_up, w_down, b_down, g_ple, w_ple_gate, w_ple_proj, g_final):
    b, s, d = x.shape
    t = b * s
    x2d = x.reshape(t, d)
    for layer in range(w_in.shape[0]):
        u, *qkv_by_dil = _inproj(x2d, g_mix[layer][None], w_in[layer].astype(BF16))
        o_list, lse_list = [], []
        for (window, dil), qkv_d, (rq, nq) in zip(DILATED_BRANCHES, qkv_by_dil, ATTN_STEP):
            o_d, lse_d = _attn_branch(qkv_d, _bias_tables(rel_bias, window, dil), b, s, dil, rq, nq)
            o_list.append(o_d.reshape(t, ATTN_WIDTH))
            lse_list.append(lse_d.transpose(0, 3, 1, 2).reshape(t, N_HEADS))
        brt = jnp.broadcast_to(b_router[layer][:, None], (N_EXPERTS, 128))
        x1, h2, pos, gates, pc = _mixout(
            x2d, u, o_list, lse_list, w_pool[layer].astype(BF16), pool_scale[layer][None],
            w_out[layer].astype(BF16), g_ffn[layer][None], w_router[layer].T, brt, s)

        pc16 = pc[:, :, 0].astype(I32)
        pcnt, loff, gstart, tails, seg_end = _routing_tables(pc16)
        n_rows = t * TOP_K + N_EXPERTS * (t // TOKEN_TILE) * (ROW_ALIGN - 1) + N_EXPERTS * (EXPERT_TILE - 1)
        n_rows = (n_rows + EXPERT_TILE - 1) // EXPERT_TILE * EXPERT_TILE
        n_tiles = n_rows // EXPERT_TILE
        tile_start = jnp.arange(n_tiles, dtype=I32) * EXPERT_TILE
        tile_expert = jnp.minimum(jnp.sum(seg_end[None, :] <= tile_start[:, None], axis=1), N_EXPERTS - 1).astype(I32)
        n_used = (seg_end[-1] // EXPERT_TILE).astype(I32)[None]
        xs = _dispatch(loff, pcnt, gstart, tails, h2, pos, n_rows)
        ys = _experts(tile_expert, n_used, xs, w_gate_up[layer], b_gate_up[layer][:, None, :],
                      w_down[layer], b_down[layer][:, None, :])

        post = pos.transpose(0, 2, 1).reshape(t, 8)
        gatet = gates.transpose(0, 2, 1).reshape(t, 8)
        assert layer == w_in.shape[0] - 1, "single-layer pipeline: the final norm is fused into combine"
        x2d = _combine(loff, pcnt, gstart, x1, post, gatet, p[layer].reshape(t, PLE_DIM), g_ple[layer][None],
                       w_ple_gate[layer].astype(BF16), w_ple_proj[layer].astype(BF16), g_final[None], ys)
    return x2d.reshape(b, s, d)
```

```python
import functools
import math

import jax
import jax.numpy as jnp
from jax import lax
from jax.experimental import pallas as pl
from jax.experimental.pallas import tpu as pltpu

F32 = jnp.float32
BF16 = jnp.bfloat16
I32 = jnp.int32

D_MODEL = 1024
POOL_WIDTH = 512
POOL_WINDOWS = (2, 4, 8, 16)
POOL_GROUP_DIM = 128
ATTN_WIDTH = 512
QKV_WIDTH = 3 * ATTN_WIDTH
HEAD_DIM = 64
N_HEADS = 8
DILATED_BRANCHES = ((128, 1), (512, 4), (2048, 16))
ATTN_BLOCK = 128
N_REL_BUCKETS = 32
REL_MAX_EXACT = 16
REL_MAX_DISTANCE = 2048
N_EXPERTS = 32
TOP_K = 4
D_EXPERT = 1024
SWIGLU_LIMIT = 7.0
SWIGLU_ALPHA = 1.702
PLE_DIM = 256
NORM_EPS = 1e-6
NEG_INF = -1e30

LANES = 128
POOL_HALO = 16
TOKEN_TILE = 512
ROW_ALIGN = 16
EXPERT_TILE = 512
EXPERT_CHUNK = 256
LOCAL_ROWS = TOKEN_TILE * TOP_K + N_EXPERTS * ROW_ALIGN
SORT_BLOCK = 256
GATHER_BLOCK = 512
ATTN_STEP = ((1, 4), (1, 4), (2, 2))
VMEM_LIMIT = 56 * 1024 * 1024


def _rms(x, g):
    return x * lax.rsqrt(jnp.mean(x * x, axis=-1, keepdims=True) + NORM_EPS) * g


def _inproj_kernel(x_ref, g_ref, w_ref, u_ref, nat_ref, d4_ref, d16_ref, z_scr):
    h = _rms(x_ref[...], g_ref[...]).astype(BF16)
    part = ATTN_WIDTH
    slabs = part // LANES

    def project(i):
        return jnp.dot(h, w_ref[:, pl.ds(i * part, part)], preferred_element_type=F32)

    def regroup(z, i):
        col0 = (i - 1) * part
        nat_ref[:, pl.ds(col0, part)] = z.astype(BF16)
        for c in range(slabs):
            z_scr[(i - 1) * slabs + c] = z[:, c * LANES:(c + 1) * LANES]
        for dil, ref in ((4, d4_ref), (16, d16_ref)):
            rows = TOKEN_TILE // dil
            for r in range(dil):
                for c in range(slabs):
                    ref[:, pl.ds(r * QKV_WIDTH + col0 + c * LANES, LANES)] = (
                        z_scr[(i - 1) * slabs + c, pl.ds(r, rows, stride=dil), :].astype(BF16))

    assert POOL_WIDTH == part and QKV_WIDTH == 3 * part
    z_prev = project(0)
    for i in range(1, 4):
        z_next = project(i)
        if i == 1:
            u_ref[...] = z_prev
        else:
            regroup(z_prev, i - 1)
        z_prev = z_next
    regroup(z_prev, 3)


def _inproj(x2d, g, w_bf16):
    t = x2d.shape[0]
    in_w = w_bf16.shape[1]
    tt = TOKEN_TILE
    return pl.pallas_call(
        _inproj_kernel,
        grid=(t // tt,),
        in_specs=[
            pl.BlockSpec((tt, D_MODEL), lambda i: (i, 0)),
            pl.BlockSpec((1, D_MODEL), lambda i: (0, 0)),
            pl.BlockSpec((D_MODEL, in_w), lambda i: (0, 0)),
        ],
        out_specs=[
            pl.BlockSpec((tt, POOL_WIDTH), lambda i: (i, 0)),
            pl.BlockSpec((tt, QKV_WIDTH), lambda i: (i, 0)),
            pl.BlockSpec((tt // 4, 4 * QKV_WIDTH), lambda i: (i, 0)),
            pl.BlockSpec((tt // 16, 16 * QKV_WIDTH), lambda i: (i, 0)),
        ],
        out_shape=[
            jax.ShapeDtypeStruct((t, POOL_WIDTH), F32),
            jax.ShapeDtypeStruct((t, QKV_WIDTH), BF16),
            jax.ShapeDtypeStruct((t // 4, 4 * QKV_WIDTH), BF16),
            jax.ShapeDtypeStruct((t // 16, 16 * QKV_WIDTH), BF16),
        ],
        scratch_shapes=[pltpu.VMEM((QKV_WIDTH // LANES, tt, LANES), F32)],
        compiler_params=pltpu.CompilerParams(
            dimension_semantics=("parallel",), vmem_limit_bytes=VMEM_LIMIT),
        name="inproj",
    )(x2d, g, w_bf16)


def _attn_kernel(main_ref, prev_ref, tab_ref, o_ref, lse_ref, k_scr, vt_scr, *, rq, nq):
    blk = ATTN_BLOCK
    first_group = pl.program_id(2) == 0
    nt = (((1,), (1,)), ((), ()))

    for ri in range(rq):
        base = ri * QKV_WIDTH
        k_scr[0:blk, :] = prev_ref[:, pl.ds(base + ATTN_WIDTH, ATTN_WIDTH)]
        k_scr[blk:, :] = main_ref[:, pl.ds(base + ATTN_WIDTH, ATTN_WIDTH)]
        vt_scr[:, 0:blk] = prev_ref[:, pl.ds(base + 2 * ATTN_WIDTH, ATTN_WIDTH)].astype(F32).T.astype(BF16)
        for j in range(nq):
            vt_scr[:, pl.ds((j + 1) * blk, blk)] = (
                main_ref[pl.ds(j * blk, blk), pl.ds(base + 2 * ATTN_WIDTH, ATTN_WIDTH)].astype(F32).T.astype(BF16))
        for j in range(nq):
            rows = pl.ds(j * blk, blk)
            keys = pl.ds(j * blk, 2 * blk)
            scores = []
            for h in range(N_HEADS):
                q = main_ref[rows, pl.ds(base + h * HEAD_DIM, HEAD_DIM)] * 0.125
                k2 = k_scr[keys, pl.ds(h * HEAD_DIM, HEAD_DIM)]
                scores.append(lax.dot_general(k2, q, nt, preferred_element_type=F32))
            probs, inv_l, lse_parts = [], [], []
            for h in range(N_HEADS):
                s = scores[h] + tab_ref[h]
                if j == 0:
                    pen = jnp.where(first_group, NEG_INF, 0.0).astype(F32)
                    s = jnp.concatenate([s[:blk] + pen, s[blk:]], axis=0)
                m = jnp.max(s, axis=0, keepdims=True)
                p = jnp.exp(s - m)
                l = jnp.sum(p, axis=0, keepdims=True)
                probs.append(p.astype(BF16))
                inv_l.append(1.0 / l)
                lse_parts.append(m + jnp.log(l))
            o_parts = []
            for h in range(N_HEADS):
                vt2 = vt_scr[pl.ds(h * HEAD_DIM, HEAD_DIM), keys]
                o_parts.append(jnp.dot(vt2, probs[h], preferred_element_type=F32) * inv_l[h])
            o_t = jnp.concatenate(o_parts, axis=0)
            o_ref[rows, pl.ds(ri * ATTN_WIDTH, ATTN_WIDTH)] = o_t.T.astype(BF16)
            lse_t = jnp.concatenate(lse_parts + [jnp.zeros((LANES - N_HEADS, blk), F32)], axis=0)
            lse_ref[rows, pl.ds(ri * LANES, LANES)] = lse_t.T


def _attn_branch(qkv_d, tab, batch, seq, dil, rq, nq):
    sub = seq // dil
    nb = sub // ATTN_BLOCK
    groups = nb // nq
    return pl.pallas_call(
        functools.partial(_attn_kernel, rq=rq, nq=nq),
        grid=(batch, dil // rq, groups),
        in_specs=[
            pl.BlockSpec((nq * ATTN_BLOCK, rq * QKV_WIDTH), lambda b, r, g: (b * groups + g, r)),
            pl.BlockSpec((ATTN_BLOCK, rq * QKV_WIDTH), lambda b, r, g: (b * nb + jnp.maximum(g * nq - 1, 0), r)),
            pl.BlockSpec((N_HEADS, 2 * ATTN_BLOCK, ATTN_BLOCK), lambda b, r, g: (0, 0, 0)),
        ],
        out_specs=[
            pl.BlockSpec((nq * ATTN_BLOCK, rq * ATTN_WIDTH), lambda b, r, g: (b * groups + g, r)),
            pl.BlockSpec((nq * ATTN_BLOCK, rq * LANES), lambda b, r, g: (b * groups + g, r)),
        ],
        out_shape=[
            jax.ShapeDtypeStruct((batch * sub, dil * ATTN_WIDTH), BF16),
            jax.ShapeDtypeStruct((batch * sub, dil * LANES), F32),
        ],
        scratch_shapes=[
            pltpu.VMEM(((nq + 1) * ATTN_BLOCK, ATTN_WIDTH), BF16),
            pltpu.VMEM((ATTN_WIDTH, (nq + 1) * ATTN_BLOCK), BF16),
        ],
        compiler_params=pltpu.CompilerParams(
            dimension_semantics=("parallel", "parallel", "parallel"), vmem_limit_bytes=VMEM_LIMIT),
        name="attn",
    )(qkv_d, qkv_d, tab)


def _t5_bucket(dist):
    n = jnp.maximum(dist, 1).astype(F32)
    large = REL_MAX_EXACT + (jnp.log(n / REL_MAX_EXACT) / math.log(REL_MAX_DISTANCE / REL_MAX_EXACT)
                             * (N_REL_BUCKETS - REL_MAX_EXACT)).astype(I32)
    large = jnp.minimum(large, N_REL_BUCKETS - 1)
    return jnp.where(dist < REL_MAX_EXACT, dist, large)


def _shifted_rows(w, n):
    lead = w.shape[:-1]
    width = w.shape[-1]
    flat = jnp.tile(w, (1,) * len(lead) + (n + 1,))[..., :n * (width + 1)]
    return flat.reshape(lead + (n, width + 1))[..., :n]


def _bias_tables(rel_bias, window, dil):
    blk = ATTN_BLOCK
    assert window // dil == blk
    f = rel_bias[_t5_bucket(jnp.arange(blk + 1) * dil)].T.astype(F32)
    neg = jnp.full((N_HEADS, blk - 1), NEG_INF, F32)
    neg1 = jnp.full((N_HEADS, 1), NEG_INF, F32)
    w_prev = jnp.concatenate([neg, f[:, :0:-1], neg1], axis=1)
    prev = _shifted_rows(w_prev, blk)[:, :, ::-1]
    w_cur = jnp.concatenate([neg, f[:, :blk], neg1], axis=1)
    cur = _shifted_rows(w_cur, blk)[:, ::-1, :]
    return jnp.concatenate([prev, cur], axis=1)


def _mixout_kernel(x_ref, u_ref, uh_ref, o1_ref, o4_ref, o16_ref, l1_ref, l4_ref, l16_ref,
                   wpool_ref, pscale_ref, wout_ref, gffn_ref, wrt_ref, brt_ref, spread_ref,
                   x1_ref, h2_ref, pos_ref, gate_ref, pc_ref,
                   ext_scr, mix_scr, o_scr, l_scr, *, tiles_per_seq):
    i = pl.program_id(0)
    tt = TOKEN_TILE
    seq_tile = i % tiles_per_seq

    halo = uh_ref[...]
    ext_scr[0:POOL_HALO, :] = jnp.where(seq_tile == 0, jnp.zeros_like(halo), halo)
    ext_scr[POOL_HALO:, :] = u_ref[...]
    tpos = seq_tile * tt + lax.broadcasted_iota(I32, (tt, 1), 0)
    for gi, w in enumerate(POOL_WINDOWS):
        cols = pl.ds(gi * POOL_GROUP_DIM, POOL_GROUP_DIM)
        tok = ext_scr[pl.ds(POOL_HALO, tt), cols]
        acc = tok
        for j in range(1, w):
            acc = acc + ext_scr[pl.ds(POOL_HALO - j, tt), cols]
        cnt = jnp.minimum(tpos + 1, w).astype(F32)
        pooled = (acc / cnt - tok).astype(BF16)
        mixed = jnp.dot(pooled, wpool_ref[gi], preferred_element_type=F32) * pscale_ref[:, cols]
        mix_scr[:, cols] = mixed.astype(BF16)

    o_slabs = ATTN_WIDTH // LANES
    for bi, (dil, o_ref, l_ref) in enumerate(((4, o4_ref, l4_ref), (16, o16_ref, l16_ref))):
        rows = tt // dil
        for r in range(dil):
            l_scr[bi, pl.ds(r, rows, stride=dil), :] = l_ref[:, pl.ds(r * LANES, LANES)]
            for c in range(o_slabs):
                o_scr[bi * o_slabs + c, pl.ds(r, rows, stride=dil), :] = (
                    o_ref[:, pl.ds(r * ATTN_WIDTH + c * LANES, LANES)].astype(F32))
    l1, l4, l16 = l1_ref[...], l_scr[0], l_scr[1]
    lm = jnp.maximum(jnp.maximum(l1, l4), l16)
    e1, e4, e16 = jnp.exp(l1 - lm), jnp.exp(l4 - lm), jnp.exp(l16 - lm)
    inv = 1.0 / (e1 + e4 + e16)

    def per_feature(w):
        hi = w.astype(BF16)
        lo = (w - hi.astype(F32)).astype(BF16)
        return (jnp.dot(hi, spread_ref[...], preferred_element_type=F32)
                + jnp.dot(lo, spread_ref[...], preferred_element_type=F32))

    w1, w4, w16 = per_feature(e1 * inv), per_feature(e4 * inv), per_feature(e16 * inv)
    for c in range(o_slabs):
        sl = pl.ds(c * LANES, LANES)
        cs = slice(c * LANES, (c + 1) * LANES)
        o = w1[:, cs] * o1_ref[:, sl].astype(F32) + w4[:, cs] * o_scr[c] + w16[:, cs] * o_scr[o_slabs + c]
        mix_scr[:, pl.ds(POOL_WIDTH + c * LANES, LANES)] = o.astype(BF16)

    x1 = x_ref[...] + jnp.dot(mix_scr[...], wout_ref[...], preferred_element_type=F32)
    x1_ref[...] = x1

    h2 = _rms(x1, gffn_ref[...])
    h2_ref[...] = h2.astype(BF16)
    logits = lax.dot_general(wrt_ref[...], h2, (((1,), (1,)), ((), ())),
                             precision=lax.Precision.HIGHEST, preferred_element_type=F32)
    logits = logits + brt_ref[:, 0:1]
    eio = lax.broadcasted_iota(I32, (N_EXPERTS, tt), 0)
    sel, val = [], []
    for _ in range(TOP_K):
        m = jnp.max(logits, axis=0, keepdims=True)
        idx = jnp.min(jnp.where(logits == m, eio, N_EXPERTS), axis=0, keepdims=True)
        sel.append(idx)
        val.append(m)
        logits = jnp.where(eio == idx, -jnp.inf, logits)
    ex = [jnp.exp(v - val[0]) for v in val]
    den = ex[0] + ex[1] + ex[2] + ex[3]
    gates = [e / den for e in ex]

    hot = [(eio == s) for s in sel]
    onehot = (hot[0] | hot[1] | hot[2] | hot[3]).astype(F32)
    ti = lax.broadcasted_iota(I32, (tt, tt), 0)
    tj = lax.broadcasted_iota(I32, (tt, tt), 1)
    before = (ti < tj).astype(BF16)
    rank = jnp.dot(onehot.astype(BF16), before, preferred_element_type=F32)
    cnt = jnp.sum(onehot, axis=1, keepdims=True)
    pc = jnp.floor((cnt + (ROW_ALIGN - 1)) * (1.0 / ROW_ALIGN))
    pcb = jnp.broadcast_to(pc, (N_EXPERTS, 128))
    pc_ref[0] = pcb
    pcm = jnp.concatenate([pcb, jnp.zeros((128 - N_EXPERTS, 128), F32)], axis=0).astype(BF16)
    li = lax.broadcasted_iota(I32, (N_EXPERTS, 128), 0)
    lj = lax.broadcasted_iota(I32, (N_EXPERTS, 128), 1)
    lower = (lj < li).astype(BF16)
    off = jnp.dot(lower, pcm, preferred_element_type=F32)[:, 0:1] * float(ROW_ALIGN)
    where_to = off + rank
    zero_i = jnp.zeros((1, tt), I32)
    zero_f = jnp.zeros((1, tt), F32)
    pos_rows = [jnp.sum(jnp.where(hk, where_to, 0.0), axis=0, keepdims=True).astype(I32) for hk in hot]
    pos_ref[0] = jnp.concatenate(pos_rows + [zero_i] * (8 - TOP_K), axis=0)
    gate_ref[0] = jnp.concatenate(gates + [zero_f] * (8 - TOP_K), axis=0)


def _mixout(x2d, u, o_list, lse_list, wpool, pscale, wout, gffn, wrt, brt, seq):
    t = x2d.shape[0]
    tt = TOKEN_TILE
    n_tt = t // tt
    halo_blocks = tt // POOL_HALO
    row = lambda i: (i, 0)
    const2 = lambda i: (0, 0)
    dils = [dil for _, dil in DILATED_BRANCHES]
    spread = (jnp.arange(LANES)[:, None] == jnp.arange(ATTN_WIDTH)[None, :] // HEAD_DIM).astype(BF16)
    return pl.pallas_call(
        functools.partial(_mixout_kernel, tiles_per_seq=seq // tt),
        grid=(n_tt,),
        in_specs=[
            pl.BlockSpec((tt, D_MODEL), row),
            pl.BlockSpec((tt, POOL_WIDTH), row),
            pl.BlockSpec((POOL_HALO, POOL_WIDTH), lambda i: (jnp.maximum(i * halo_blocks - 1, 0), 0)),
            *[pl.BlockSpec((tt // dil, dil * ATTN_WIDTH), row) for dil in dils],
            *[pl.BlockSpec((tt // dil, dil * LANES), row) for dil in dils],
            pl.BlockSpec((len(POOL_WINDOWS), POOL_GROUP_DIM, POOL_GROUP_DIM), lambda i: (0, 0, 0)),
            pl.BlockSpec((1, POOL_WIDTH), const2),
            pl.BlockSpec((D_MODEL, D_MODEL), const2),
            pl.BlockSpec((1, D_MODEL), const2),
            pl.BlockSpec((N_EXPERTS, D_MODEL), const2),
            pl.BlockSpec((N_EXPERTS, 128), const2),
            pl.BlockSpec((LANES, ATTN_WIDTH), const2),
        ],
        out_specs=[
            pl.BlockSpec((tt, D_MODEL), row),
            pl.BlockSpec((tt, D_MODEL), row),
            pl.BlockSpec((1, 8, tt), lambda i: (i, 0, 0)),
            pl.BlockSpec((1, 8, tt), lambda i: (i, 0, 0)),
            pl.BlockSpec((1, N_EXPERTS, 128), lambda i: (i, 0, 0)),
        ],
        out_shape=[
            jax.ShapeDtypeStruct((t, D_MODEL), F32),
            jax.ShapeDtypeStruct((t, D_MODEL), BF16),
            jax.ShapeDtypeStruct((n_tt, 8, tt), I32),
            jax.ShapeDtypeStruct((n_tt, 8, tt), F32),
            jax.ShapeDtypeStruct((n_tt, N_EXPERTS, 128), F32),
        ],
        scratch_shapes=[
            pltpu.VMEM((tt + POOL_HALO, POOL_WIDTH), F32),
            pltpu.VMEM((tt, D_MODEL), BF16),
            pltpu.VMEM((2 * ATTN_WIDTH // LANES, tt, LANES), F32),
            pltpu.VMEM((2, tt, LANES), F32),
        ],
        compiler_params=pltpu.CompilerParams(
            dimension_semantics=("parallel",), vmem_limit_bytes=VMEM_LIMIT),
        name="mixout",
    )(x2d, u, u, *o_list, *lse_list, wpool, pscale, wout, gffn, wrt, brt, spread)


def _chunk_copies(loff_s, pcnt_s, gstart_s, i, local_ref, global_ref, sem, to_global):
    out = []
    for e in range(N_EXPERTS):
        n = pl.multiple_of(pcnt_s[i, e], ROW_ALIGN)
        lo = pl.multiple_of(loff_s[i, e], ROW_ALIGN)
        go = pl.multiple_of(gstart_s[i, e], ROW_ALIGN)
        loc = local_ref.at[pl.ds(lo, n)]
        glo = global_ref.at[pl.ds(go, n)]
        src, dst = (loc, glo) if to_global else (glo, loc)
        out.append((n > 0, pltpu.make_async_copy(src, dst, sem)))
    return out


def _dispatch_kernel(loff_s, pcnt_s, gstart_s, tail_s, h2_ref, pos_ref, xs_ref, loc_scr, zero_scr, sem, tail_sem,
                     spare_sem):
    i = pl.program_id(0)
    tt = TOKEN_TILE

    @pl.when(i == 0)
    def _():
        zero_scr[...] = jnp.zeros_like(zero_scr)
        tails = []
        for e in range(N_EXPERTS):
            n = pl.multiple_of(tail_s[1, e], ROW_ALIGN)
            start = pl.multiple_of(tail_s[0, e], ROW_ALIGN)
            tails.append((n > 0, pltpu.make_async_copy(zero_scr.at[pl.ds(0, n)], xs_ref.at[pl.ds(start, n)], tail_sem)))
        for cond, cp in tails:
            pl.when(cond)(cp.start)
        for cond, cp in tails:
            pl.when(cond)(cp.wait)

    def spare_tile(j):
        return pltpu.make_async_copy(
            zero_scr, xs_ref.at[pl.ds(pl.multiple_of(j * EXPERT_TILE, EXPERT_TILE), EXPERT_TILE)], spare_sem)

    n_tiles = xs_ref.shape[0] // EXPERT_TILE

    @pl.when(i == 0)
    def _():
        lax.fori_loop(tail_s[2, 0], n_tiles, lambda j, c: (spare_tile(j).start(), c)[1], 0)

    @pl.when(i == pl.num_programs(0) - 1)
    def _():
        lax.fori_loop(tail_s[2, 0], n_tiles, lambda j, c: (spare_tile(j).wait(), c)[1], 0)

    pos = pos_ref[0]
    h2 = h2_ref[...]
    for jb in range(LOCAL_ROWS // SORT_BLOCK):
        jio = lax.broadcasted_iota(I32, (SORT_BLOCK, tt), 0) + jb * SORT_BLOCK
        hit = (jio == pos[0:1]) | (jio == pos[1:2]) | (jio == pos[2:3]) | (jio == pos[3:4])
        onehot = jnp.where(hit, 1.0, 0.0).astype(BF16)
        loc_scr[pl.ds(jb * SORT_BLOCK, SORT_BLOCK), :] = jnp.dot(
            onehot, h2, preferred_element_type=F32).astype(BF16)
    copies = _chunk_copies(loff_s, pcnt_s, gstart_s, i, loc_scr, xs_ref, sem, True)
    for cond, cp in copies:
        pl.when(cond)(cp.start)
    for cond, cp in copies:
        pl.when(cond)(cp.wait)


def _dispatch(loff, pcnt, gstart, tails, h2, pos, n_rows):
    t = h2.shape[0]
    tt = TOKEN_TILE
    grid_spec = pltpu.PrefetchScalarGridSpec(
        num_scalar_prefetch=4,
        grid=(t // tt,),
        in_specs=[
            pl.BlockSpec((tt, D_MODEL), lambda i, *_: (i, 0)),
            pl.BlockSpec((1, 8, tt), lambda i, *_: (i, 0, 0)),
        ],
        out_specs=pl.BlockSpec(memory_space=pl.ANY),
        scratch_shapes=[
            pltpu.VMEM((LOCAL_ROWS, D_MODEL), BF16),
            pltpu.VMEM((EXPERT_TILE, D_MODEL), BF16),
            pltpu.SemaphoreType.DMA(()),
            pltpu.SemaphoreType.DMA(()),
            pltpu.SemaphoreType.DMA(()),
        ],
    )
    return pl.pallas_call(
        _dispatch_kernel,
        grid_spec=grid_spec,
        out_shape=jax.ShapeDtypeStruct((n_rows, D_MODEL), BF16),
        compiler_params=pltpu.CompilerParams(
            dimension_semantics=("arbitrary",), vmem_limit_bytes=VMEM_LIMIT),
        name="dispatch",
    )(loff, pcnt, gstart, tails, h2, pos)


def _expert_kernel(te_s, nu_s, xs_ref, wgu_ref, bgu_ref, wd_ref, bd_ref, ys_ref, wgu_bf, wd_bf, act_scr):
    i = pl.program_id(0)
    live = i < nu_s[0]
    new_expert = (i == 0) | (te_s[i] != te_s[jnp.maximum(i - 1, 0)])

    @pl.when(live & new_expert)
    def _():
        wgu_bf[...] = wgu_ref[0].astype(BF16)
        wd_bf[...] = wd_ref[0].astype(BF16)

    @pl.when(live)
    def _():
        x = xs_ref[...]
        n_chunks = D_EXPERT // EXPERT_CHUNK

        def gate_up(c):
            gc = pl.ds(c * EXPERT_CHUNK, EXPERT_CHUNK)
            uc = pl.ds(D_EXPERT + c * EXPERT_CHUNK, EXPERT_CHUNK)
            return (jnp.dot(x, wgu_bf[:, gc], preferred_element_type=F32) + bgu_ref[0, :, gc],
                    jnp.dot(x, wgu_bf[:, uc], preferred_element_type=F32) + bgu_ref[0, :, uc])

        g, u = gate_up(0)
        for c in range(n_chunks):
            nxt = gate_up(c + 1) if c + 1 < n_chunks else None
            g = jnp.minimum(g, SWIGLU_LIMIT)
            u = jnp.clip(u, -SWIGLU_LIMIT, SWIGLU_LIMIT)
            act_scr[:, pl.ds(c * EXPERT_CHUNK, EXPERT_CHUNK)] = (
                (u + 1.0) * (g * jax.nn.sigmoid(SWIGLU_ALPHA * g))).astype(BF16)
            if nxt is not None:
                g, u = nxt
        y = jnp.dot(act_scr[...], wd_bf[...], preferred_element_type=F32) + bd_ref[0]
        ys_ref[...] = y.astype(BF16)

    @pl.when(jnp.logical_not(live))
    def _():
        ys_ref[...] = jnp.zeros_like(ys_ref)


def _experts(tile_expert, n_used, xs, wgu, bgu, wd, bd):
    n_rows = xs.shape[0]
    tm = EXPERT_TILE
    live = lambda i, te, nu: jnp.minimum(i, nu[0] - 1)
    grid_spec = pltpu.PrefetchScalarGridSpec(
        num_scalar_prefetch=2,
        grid=(n_rows // tm,),
        in_specs=[
            pl.BlockSpec((tm, D_MODEL), lambda i, te, nu: (live(i, te, nu), 0)),
            pl.BlockSpec((1, D_MODEL, 2 * D_EXPERT), lambda i, te, nu: (te[live(i, te, nu)], 0, 0)),
            pl.BlockSpec((1, 1, 2 * D_EXPERT), lambda i, te, nu: (te[live(i, te, nu)], 0, 0)),
            pl.BlockSpec((1, D_EXPERT, D_MODEL), lambda i, te, nu: (te[live(i, te, nu)], 0, 0)),
            pl.BlockSpec((1, 1, D_MODEL), lambda i, te, nu: (te[live(i, te, nu)], 0, 0)),
        ],
        out_specs=pl.BlockSpec((tm, D_MODEL), lambda i, te, nu: (i, 0)),
        scratch_shapes=[pltpu.VMEM((D_MODEL, 2 * D_EXPERT), BF16), pltpu.VMEM((D_EXPERT, D_MODEL), BF16),
                        pltpu.VMEM((tm, D_EXPERT), BF16)],
    )
    return pl.pallas_call(
        _expert_kernel,
        grid_spec=grid_spec,
        out_shape=jax.ShapeDtypeStruct((n_rows, D_MODEL), BF16),
        compiler_params=pltpu.CompilerParams(
            dimension_semantics=("arbitrary",), vmem_limit_bytes=VMEM_LIMIT),
        name="experts",
    )(tile_expert, n_used, xs, wgu, bgu, wd, bd)


def _combine_kernel(loff_s, pcnt_s, gstart_s, x1_ref, post_ref, gatet_ref, p_ref, gple_ref, wg_ref, wp_ref,
                    gfin_ref, ys_ref, out_ref, loc_scr, sem):
    i = pl.program_id(0)
    tt = TOKEN_TILE

    @pl.when(i == 0)
    def _():
        loc_scr[...] = jnp.zeros_like(loc_scr)

    copies = _chunk_copies(loff_s, pcnt_s, gstart_s, i, loc_scr, ys_ref, sem, False)
    for cond, cp in copies:
        pl.when(cond)(cp.start)
    for cond, cp in copies:
        pl.when(cond)(cp.wait)

    post = post_ref[...]
    gatet = gatet_ref[...]
    moe = jnp.zeros((tt, D_MODEL), F32)
    for jb in range(LOCAL_ROWS // GATHER_BLOCK):
        jio = lax.broadcasted_iota(I32, (tt, GATHER_BLOCK), 1) + jb * GATHER_BLOCK
        w = jnp.zeros((tt, GATHER_BLOCK), F32)
        for k in range(TOP_K):
            w = w + jnp.where(jio == post[:, k:k + 1], gatet[:, k:k + 1], 0.0)
        y = loc_scr[pl.ds(jb * GATHER_BLOCK, GATHER_BLOCK), :]
        moe = moe + jnp.dot(w.astype(BF16), y, preferred_element_type=F32)

    x2 = x1_ref[...] + moe
    gate = jax.nn.sigmoid(jnp.dot(_rms(x2, gple_ref[...]).astype(BF16), wg_ref[...], preferred_element_type=F32))
    emb = jnp.dot(p_ref[...].astype(BF16), wp_ref[...], preferred_element_type=F32)
    out_ref[...] = _rms(x2 + emb * gate, gfin_ref[...])


def _combine(loff, pcnt, gstart, x1, post, gatet, p2d, gple, wg, wp, gfin, ys):
    t = x1.shape[0]
    tt = TOKEN_TILE
    row = lambda i, *_: (i, 0)
    const2 = lambda i, *_: (0, 0)
    grid_spec = pltpu.PrefetchScalarGridSpec(
        num_scalar_prefetch=3,
        grid=(t // tt,),
        in_specs=[
            pl.BlockSpec((tt, D_MODEL), row),
            pl.BlockSpec((tt, 8), row),
            pl.BlockSpec((tt, 8), row),
            pl.BlockSpec((tt, PLE_DIM), row),
            pl.BlockSpec((1, D_MODEL), const2),
            pl.BlockSpec((D_MODEL, D_MODEL), const2),
            pl.BlockSpec((PLE_DIM, D_MODEL), const2),
            pl.BlockSpec((1, D_MODEL), const2),
            pl.BlockSpec(memory_space=pl.ANY),
        ],
        out_specs=pl.BlockSpec((tt, D_MODEL), row),
        scratch_shapes=[pltpu.VMEM((LOCAL_ROWS, D_MODEL), BF16), pltpu.SemaphoreType.DMA(())],
    )
    return pl.pallas_call(
        _combine_kernel,
        grid_spec=grid_spec,
        out_shape=jax.ShapeDtypeStruct((t, D_MODEL), F32),
        compiler_params=pltpu.CompilerParams(
            dimension_semantics=("arbitrary",), vmem_limit_bytes=VMEM_LIMIT),
        name="combine",
    )(loff, pcnt, gstart, x1, post, gatet, p2d, gple, wg, wp, gfin, ys)


def _routing_tables(pc16):
    pcnt = pc16 * ROW_ALIGN
    loff = jnp.cumsum(pcnt, axis=1) - pcnt
    seg = jnp.sum(pcnt, axis=0)
    segpad = (seg + EXPERT_TILE - 1) // EXPERT_TILE * EXPERT_TILE
    seg_end = jnp.cumsum(segpad)
    ebase = seg_end - segpad
    gstart = ebase[None, :] + jnp.cumsum(pcnt, axis=0) - pcnt
    tails = jnp.stack([ebase + seg, segpad - seg, jnp.broadcast_to(seg_end[-1] // EXPERT_TILE, seg.shape)])
    return pcnt, loff, gstart, tails, seg_end


def kernel(x, p, g_mix, w_in, w_pool, pool_scale, rel_bias, w_out, g_ffn, w_router, b_router, w_gate_up,
           b_gate_up, w_down, b_down, g_ple, w_ple_gate, w_ple_proj, g_final):
    b, s, d = x.shape
    t = b * s
    x2d = x.reshape(t, d)
    for layer in range(w_in.shape[0]):
        u, *qkv_by_dil = _inproj(x2d, g_mix[layer][None], w_in[layer].astype(BF16))
        o_list, lse_list = [], []
        for (window, dil), qkv_d, (rq, nq) in zip(DILATED_BRANCHES, qkv_by_dil, ATTN_STEP):
            o_d, lse_d = _attn_branch(qkv_d, _bias_tables(rel_bias, window, dil), b, s, dil, rq, nq)
            o_list.append(o_d)
            lse_list.append(lse_d)
        brt = jnp.broadcast_to(b_router[layer][:, None], (N_EXPERTS, 128))
        x1, h2, pos, gates, pc = _mixout(
            x2d, u, o_list, lse_list, w_pool[layer].astype(BF16), pool_scale[layer][None],
            w_out[layer].astype(BF16), g_ffn[layer][None], w_router[layer].T, brt, s)

        pc16 = pc[:, :, 0].astype(I32)
        pcnt, loff, gstart, tails, seg_end = _routing_tables(pc16)
        n_rows = t * TOP_K + N_EXPERTS * (t // TOKEN_TILE) * (ROW_ALIGN - 1) + N_EXPERTS * (EXPERT_TILE - 1)
        n_rows = (n_rows + EXPERT_TILE - 1) // EXPERT_TILE * EXPERT_TILE
        n_tiles = n_rows // EXPERT_TILE
        tile_start = jnp.arange(n_tiles, dtype=I32) * EXPERT_TILE
        tile_expert = jnp.minimum(jnp.sum(seg_end[None, :] <= tile_start[:, None], axis=1), N_EXPERTS - 1).astype(I32)
        n_used = (seg_end[-1] // EXPERT_TILE).astype(I32)[None]
        xs = _dispatch(loff, pcnt, gstart, tails, h2, pos, n_rows)
        ys = _experts(tile_expert, n_used, xs, w_gate_up[layer], b_gate_up[layer][:, None, :],
                      w_down[layer], b_down[layer][:, None, :])

        post = pos.transpose(0, 2, 1).reshape(t, 8)
        gatet = gates.transpose(0, 2, 1).reshape(t, 8)
        assert layer == w_in.shape[0] - 1, "single-layer pipeline: the final norm is fused into combine"
        x2d = _combine(loff, pcnt, gstart, x1, post, gatet, p[layer].reshape(t, PLE_DIM), g_ple[layer][None],
                       w_ple_gate[layer].astype(BF16), w_ple_proj[layer].astype(BF16), g_final[None], ys)
    return x2d.reshape(b, s, d)
```

```python
import functools
import math

import jax
import jax.numpy as jnp
from jax import lax
from jax.experimental import pallas as pl
from jax.experimental.pallas import tpu as pltpu

F32 = jnp.float32
BF16 = jnp.bfloat16
I32 = jnp.int32

D_MODEL = 1024
POOL_WIDTH = 512
POOL_WINDOWS = (2, 4, 8, 16)
POOL_GROUP_DIM = 128
ATTN_WIDTH = 512
QKV_WIDTH = 3 * ATTN_WIDTH
HEAD_DIM = 64
N_HEADS = 8
DILATED_BRANCHES = ((128, 1), (512, 4), (2048, 16))
ATTN_BLOCK = 128
N_REL_BUCKETS = 32
REL_MAX_EXACT = 16
REL_MAX_DISTANCE = 2048
N_EXPERTS = 32
TOP_K = 4
D_EXPERT = 1024
SWIGLU_LIMIT = 7.0
SWIGLU_ALPHA = 1.702
PLE_DIM = 256
NORM_EPS = 1e-6
NEG_INF = -1e30

LANES = 128
POOL_HALO = 16
TOKEN_TILE = 512
ROW_ALIGN = 16
EXPERT_TILE = 512
EXPERT_CHUNK = 256
LOCAL_ROWS = TOKEN_TILE * TOP_K + N_EXPERTS * ROW_ALIGN
SORT_BLOCK = 256
GATHER_BLOCK = 512
ATTN_STEP = ((1, 4), (1, 4), (2, 2))
VMEM_LIMIT = 56 * 1024 * 1024


def _rms(x, g):
    return x * lax.rsqrt(jnp.mean(x * x, axis=-1, keepdims=True) + NORM_EPS) * g


def _inproj_kernel(x_ref, g_ref, w_ref, u_ref, nat_ref, d4_ref, d16_ref, z_scr):
    h = _rms(x_ref[...], g_ref[...]).astype(BF16)
    part = ATTN_WIDTH
    slabs = part // LANES

    def project(i):
        return jnp.dot(h, w_ref[:, pl.ds(i * part, part)], preferred_element_type=F32)

    def regroup(z, i):
        col0 = (i - 1) * part
        nat_ref[:, pl.ds(col0, part)] = z.astype(BF16)
        for c in range(slabs):
            z_scr[(i - 1) * slabs + c] = z[:, c * LANES:(c + 1) * LANES]
        for dil, ref in ((4, d4_ref), (16, d16_ref)):
            rows = TOKEN_TILE // dil
            for r in range(dil):
                for c in range(slabs):
                    ref[:, pl.ds(r * QKV_WIDTH + col0 + c * LANES, LANES)] = (
                        z_scr[(i - 1) * slabs + c, pl.ds(r, rows, stride=dil), :].astype(BF16))

    assert POOL_WIDTH == part and QKV_WIDTH == 3 * part
    z_prev = project(0)
    for i in range(1, 4):
        z_next = project(i)
        if i == 1:
            u_ref[...] = z_prev
        else:
            regroup(z_prev, i - 1)
        z_prev = z_next
    regroup(z_prev, 3)


def _inproj(x2d, g, w_bf16):
    t = x2d.shape[0]
    in_w = w_bf16.shape[1]
    tt = TOKEN_TILE
    return pl.pallas_call(
        _inproj_kernel,
        grid=(t // tt,),
        in_specs=[
            pl.BlockSpec((tt, D_MODEL), lambda i: (i, 0)),
            pl.BlockSpec((1, D_MODEL), lambda i: (0, 0)),
            pl.BlockSpec((D_MODEL, in_w), lambda i: (0, 0)),
        ],
        out_specs=[
            pl.BlockSpec((tt, POOL_WIDTH), lambda i: (i, 0)),
            pl.BlockSpec((tt, QKV_WIDTH), lambda i: (i, 0)),
            pl.BlockSpec((tt // 4, 4 * QKV_WIDTH), lambda i: (i, 0)),
            pl.BlockSpec((tt // 16, 16 * QKV_WIDTH), lambda i: (i, 0)),
        ],
        out_shape=[
            jax.ShapeDtypeStruct((t, POOL_WIDTH), F32),
            jax.ShapeDtypeStruct((t, QKV_WIDTH), BF16),
            jax.ShapeDtypeStruct((t // 4, 4 * QKV_WIDTH), BF16),
            jax.ShapeDtypeStruct((t // 16, 16 * QKV_WIDTH), BF16),
        ],
        scratch_shapes=[pltpu.VMEM((QKV_WIDTH // LANES, tt, LANES), F32)],
        compiler_params=pltpu.CompilerParams(
            dimension_semantics=("parallel",), vmem_limit_bytes=VMEM_LIMIT),
        name="inproj",
    )(x2d, g, w_bf16)


def _attn_kernel(main_ref, prev_ref, tab_ref, o_ref, lse_ref, k_scr, vt_scr, *, rq, nq):
    blk = ATTN_BLOCK
    first_group = pl.program_id(2) == 0
    nt = (((1,), (1,)), ((), ()))

    for ri in range(rq):
        base = ri * QKV_WIDTH
        k_scr[0:blk, :] = prev_ref[:, pl.ds(base + ATTN_WIDTH, ATTN_WIDTH)]
        k_scr[blk:, :] = main_ref[:, pl.ds(base + ATTN_WIDTH, ATTN_WIDTH)]
        vt_scr[:, 0:blk] = prev_ref[:, pl.ds(base + 2 * ATTN_WIDTH, ATTN_WIDTH)].astype(F32).T.astype(BF16)
        for j in range(nq):
            vt_scr[:, pl.ds((j + 1) * blk, blk)] = (
                main_ref[pl.ds(j * blk, blk), pl.ds(base + 2 * ATTN_WIDTH, ATTN_WIDTH)].astype(F32).T.astype(BF16))
        for j in range(nq):
            rows = pl.ds(j * blk, blk)
            keys = pl.ds(j * blk, 2 * blk)
            scores = []
            for h in range(N_HEADS):
                q = main_ref[rows, pl.ds(base + h * HEAD_DIM, HEAD_DIM)] * 0.125
                k2 = k_scr[keys, pl.ds(h * HEAD_DIM, HEAD_DIM)]
                scores.append(lax.dot_general(k2, q, nt, preferred_element_type=F32))
            probs, inv_l, lse_parts = [], [], []
            for h in range(N_HEADS):
                s = scores[h] + tab_ref[h]
                if j == 0:
                    pen = jnp.where(first_group, NEG_INF, 0.0).astype(F32)
                    s = jnp.concatenate([s[:blk] + pen, s[blk:]], axis=0)
                m = jnp.max(s, axis=0, keepdims=True)
                p = jnp.exp(s - m)
                l = jnp.sum(p, axis=0, keepdims=True)
                probs.append(p.astype(BF16))
                inv_l.append(1.0 / l)
                lse_parts.append(m + jnp.log(l))
            o_parts = []
            for h in range(N_HEADS):
                vt2 = vt_scr[pl.ds(h * HEAD_DIM, HEAD_DIM), keys]
                o_parts.append(jnp.dot(vt2, probs[h], preferred_element_type=F32) * inv_l[h])
            o_t = jnp.concatenate(o_parts, axis=0)
            o_ref[rows, pl.ds(ri * ATTN_WIDTH, ATTN_WIDTH)] = o_t.T.astype(BF16)
            lse_t = jnp.concatenate(lse_parts + [jnp.zeros((LANES - N_HEADS, blk), F32)], axis=0)
            lse_ref[rows, pl.ds(ri * LANES, LANES)] = lse_t.T


def _attn_branch(qkv_d, tab, batch, seq, dil, rq, nq):
    sub = seq // dil
    nb = sub // ATTN_BLOCK
    groups = nb // nq
    return pl.pallas_call(
        functools.partial(_attn_kernel, rq=rq, nq=nq),
        grid=(batch, dil // rq, groups),
        in_specs=[
            pl.BlockSpec((nq * ATTN_BLOCK, rq * QKV_WIDTH), lambda b, r, g: (b * groups + g, r)),
            pl.BlockSpec((ATTN_BLOCK, rq * QKV_WIDTH), lambda b, r, g: (b * nb + jnp.maximum(g * nq - 1, 0), r)),
            pl.BlockSpec((N_HEADS, 2 * ATTN_BLOCK, ATTN_BLOCK), lambda b, r, g: (0, 0, 0)),
        ],
        out_specs=[
            pl.BlockSpec((nq * ATTN_BLOCK, rq * ATTN_WIDTH), lambda b, r, g: (b * groups + g, r)),
            pl.BlockSpec((nq * ATTN_BLOCK, rq * LANES), lambda b, r, g: (b * groups + g, r)),
        ],
        out_shape=[
            jax.ShapeDtypeStruct((batch * sub, dil * ATTN_WIDTH), BF16),
            jax.ShapeDtypeStruct((batch * sub, dil * LANES), F32),
        ],
        scratch_shapes=[
            pltpu.VMEM(((nq + 1) * ATTN_BLOCK, ATTN_WIDTH), BF16),
            pltpu.VMEM((ATTN_WIDTH, (nq + 1) * ATTN_BLOCK), BF16),
        ],
        compiler_params=pltpu.CompilerParams(
            dimension_semantics=("parallel", "parallel", "parallel"), vmem_limit_bytes=VMEM_LIMIT),
        name="attn",
    )(qkv_d, qkv_d, tab)


def _t5_bucket(dist):
    n = jnp.maximum(dist, 1).astype(F32)
    large = REL_MAX_EXACT + (jnp.log(n / REL_MAX_EXACT) / math.log(REL_MAX_DISTANCE / REL_MAX_EXACT)
                             * (N_REL_BUCKETS - REL_MAX_EXACT)).astype(I32)
    large = jnp.minimum(large, N_REL_BUCKETS - 1)
    return jnp.where(dist < REL_MAX_EXACT, dist, large)


def _shifted_rows(w, n):
    lead = w.shape[:-1]
    width = w.shape[-1]
    flat = jnp.tile(w, (1,) * len(lead) + (n + 1,))[..., :n * (width + 1)]
    return flat.reshape(lead + (n, width + 1))[..., :n]


def _bias_tables(rel_bias, window, dil):
    blk = ATTN_BLOCK
    assert window // dil == blk
    f = rel_bias[_t5_bucket(jnp.arange(blk + 1) * dil)].T.astype(F32)
    neg = jnp.full((N_HEADS, blk - 1), NEG_INF, F32)
    neg1 = jnp.full((N_HEADS, 1), NEG_INF, F32)
    w_prev = jnp.concatenate([neg, f[:, :0:-1], neg1], axis=1)
    prev = _shifted_rows(w_prev, blk)[:, :, ::-1]
    w_cur = jnp.concatenate([neg, f[:, :blk], neg1], axis=1)
    cur = _shifted_rows(w_cur, blk)[:, ::-1, :]
    return jnp.concatenate([prev, cur], axis=1)


def _mixout_kernel(x_ref, u_ref, uh_ref, o1_ref, o4_ref, o16_ref, l1_ref, l4_ref, l16_ref,
                   wpool_ref, pscale_ref, wout_ref, gffn_ref, wrt_ref, brt_ref, spread_ref,
                   x1_ref, h2_ref, pos_ref, gate_ref, pc_ref,
                   ext_scr, mix_scr, o_scr, l_scr, *, tiles_per_seq):
    i = pl.program_id(0)
    tt = TOKEN_TILE
    seq_tile = i % tiles_per_seq

    halo = uh_ref[...]
    ext_scr[0:POOL_HALO, :] = jnp.where(seq_tile == 0, jnp.zeros_like(halo), halo)
    ext_scr[POOL_HALO:, :] = u_ref[...]
    tpos = seq_tile * tt + lax.broadcasted_iota(I32, (tt, 1), 0)
    for gi, w in enumerate(POOL_WINDOWS):
        cols = pl.ds(gi * POOL_GROUP_DIM, POOL_GROUP_DIM)
        tok = ext_scr[pl.ds(POOL_HALO, tt), cols]
        acc = tok
        for j in range(1, w):
            acc = acc + ext_scr[pl.ds(POOL_HALO - j, tt), cols]
        cnt = jnp.minimum(tpos + 1, w).astype(F32)
        pooled = (acc / cnt - tok).astype(BF16)
        mixed = jnp.dot(pooled, wpool_ref[gi], preferred_element_type=F32) * pscale_ref[:, cols]
        mix_scr[:, cols] = mixed.astype(BF16)

    o_slabs = ATTN_WIDTH // LANES
    for bi, (dil, o_ref, l_ref) in enumerate(((4, o4_ref, l4_ref), (16, o16_ref, l16_ref))):
        rows = tt // dil
        for r in range(dil):
            l_scr[bi, pl.ds(r, rows, stride=dil), :] = l_ref[:, pl.ds(r * LANES, LANES)]
            for c in range(o_slabs):
                o_scr[bi * o_slabs + c, pl.ds(r, rows, stride=dil), :] = (
                    o_ref[:, pl.ds(r * ATTN_WIDTH + c * LANES, LANES)].astype(F32))
    l1, l4, l16 = l1_ref[...], l_scr[0], l_scr[1]
    lm = jnp.maximum(jnp.maximum(l1, l4), l16)
    e1, e4, e16 = jnp.exp(l1 - lm), jnp.exp(l4 - lm), jnp.exp(l16 - lm)
    inv = 1.0 / (e1 + e4 + e16)

    def per_feature(w):
        hi = w.astype(BF16)
        lo = (w - hi.astype(F32)).astype(BF16)
        return (jnp.dot(hi, spread_ref[...], preferred_element_type=F32)
                + jnp.dot(lo, spread_ref[...], preferred_element_type=F32))

    w1, w4, w16 = per_feature(e1 * inv), per_feature(e4 * inv), per_feature(e16 * inv)
    for c in range(o_slabs):
        sl = pl.ds(c * LANES, LANES)
        cs = slice(c * LANES, (c + 1) * LANES)
        o = w1[:, cs] * o1_ref[:, sl].astype(F32) + w4[:, cs] * o_scr[c] + w16[:, cs] * o_scr[o_slabs + c]
        mix_scr[:, pl.ds(POOL_WIDTH + c * LANES, LANES)] = o.astype(BF16)

    x1 = x_ref[...] + jnp.dot(mix_scr[...], wout_ref[...], preferred_element_type=F32)
    x1_ref[...] = x1

    h2 = _rms(x1, gffn_ref[...])
    h2_ref[...] = h2.astype(BF16)
    logits = lax.dot_general(wrt_ref[...], h2, (((1,), (1,)), ((), ())),
                             precision=lax.Precision.HIGHEST, preferred_element_type=F32)
    logits = logits + brt_ref[:, 0:1]
    eio = lax.broadcasted_iota(I32, (N_EXPERTS, tt), 0)
    sel, val = [], []
    for _ in range(TOP_K):
        m = jnp.max(logits, axis=0, keepdims=True)
        idx = jnp.min(jnp.where(logits == m, eio, N_EXPERTS), axis=0, keepdims=True)
        sel.append(idx)
        val.append(m)
        logits = jnp.where(eio == idx, -jnp.inf, logits)
    ex = [jnp.exp(v - val[0]) for v in val]
    den = ex[0] + ex[1] + ex[2] + ex[3]
    gates = [e / den for e in ex]

    hot = [(eio == s) for s in sel]
    onehot = (hot[0] | hot[1] | hot[2] | hot[3]).astype(F32)
    ti = lax.broadcasted_iota(I32, (tt, tt), 0)
    tj = lax.broadcasted_iota(I32, (tt, tt), 1)
    before = (ti < tj).astype(BF16)
    rank = jnp.dot(onehot.astype(BF16), before, preferred_element_type=F32)
    cnt = jnp.sum(onehot, axis=1, keepdims=True)
    pc = jnp.floor((cnt + (ROW_ALIGN - 1)) * (1.0 / ROW_ALIGN))
    pcb = jnp.broadcast_to(pc, (N_EXPERTS, 128))
    pc_ref[0] = pcb
    pcm = jnp.concatenate([pcb, jnp.zeros((128 - N_EXPERTS, 128), F32)], axis=0).astype(BF16)
    li = lax.broadcasted_iota(I32, (N_EXPERTS, 128), 0)
    lj = lax.broadcasted_iota(I32, (N_EXPERTS, 128), 1)
    lower = (lj < li).astype(BF16)
    off = jnp.dot(lower, pcm, preferred_element_type=F32)[:, 0:1] * float(ROW_ALIGN)
    where_to = off + rank
    zero_i = jnp.zeros((1, tt), I32)
    zero_f = jnp.zeros((1, tt), F32)
    pos_rows = [jnp.sum(jnp.where(hk, where_to, 0.0), axis=0, keepdims=True).astype(I32) for hk in hot]
    pos_ref[0] = jnp.concatenate(pos_rows + [zero_i] * (8 - TOP_K), axis=0)
    gate_ref[0] = jnp.concatenate(gates + [zero_f] * (8 - TOP_K), axis=0)


def _mixout(x2d, u, o_list, lse_list, wpool, pscale, wout, gffn, wrt, brt, seq):
    t = x2d.shape[0]
    tt = TOKEN_TILE
    n_tt = t // tt
    halo_blocks = tt // POOL_HALO
    row = lambda i: (i, 0)
    const2 = lambda i: (0, 0)
    dils = [dil for _, dil in DILATED_BRANCHES]
    spread = (jnp.arange(LANES)[:, None] == jnp.arange(ATTN_WIDTH)[None, :] // HEAD_DIM).astype(BF16)
    return pl.pallas_call(
        functools.partial(_mixout_kernel, tiles_per_seq=seq // tt),
        grid=(n_tt,),
        in_specs=[
            pl.BlockSpec((tt, D_MODEL), row),
            pl.BlockSpec((tt, POOL_WIDTH), row),
            pl.BlockSpec((POOL_HALO, POOL_WIDTH), lambda i: (jnp.maximum(i * halo_blocks - 1, 0), 0)),
            *[pl.BlockSpec((tt // dil, dil * ATTN_WIDTH), row) for dil in dils],
            *[pl.BlockSpec((tt // dil, dil * LANES), row) for dil in dils],
            pl.BlockSpec((len(POOL_WINDOWS), POOL_GROUP_DIM, POOL_GROUP_DIM), lambda i: (0, 0, 0)),
            pl.BlockSpec((1, POOL_WIDTH), const2),
            pl.BlockSpec((D_MODEL, D_MODEL), const2),
            pl.BlockSpec((1, D_MODEL), const2),
            pl.BlockSpec((N_EXPERTS, D_MODEL), const2),
            pl.BlockSpec((N_EXPERTS, 128), const2),
            pl.BlockSpec((LANES, ATTN_WIDTH), const2),
        ],
        out_specs=[
            pl.BlockSpec((tt, D_MODEL), row),
            pl.BlockSpec((tt, D_MODEL), row),
            pl.BlockSpec((1, 8, tt), lambda i: (i, 0, 0)),
            pl.BlockSpec((1, 8, tt), lambda i: (i, 0, 0)),
            pl.BlockSpec((1, N_EXPERTS, 128), lambda i: (i, 0, 0)),
        ],
        out_shape=[
            jax.ShapeDtypeStruct((t, D_MODEL), F32),
            jax.ShapeDtypeStruct((t, D_MODEL), BF16),
            jax.ShapeDtypeStruct((n_tt, 8, tt), I32),
            jax.ShapeDtypeStruct((n_tt, 8, tt), F32),
            jax.ShapeDtypeStruct((n_tt, N_EXPERTS, 128), F32),
        ],
        scratch_shapes=[
            pltpu.VMEM((tt + POOL_HALO, POOL_WIDTH), F32),
            pltpu.VMEM((tt, D_MODEL), BF16),
            pltpu.VMEM((2 * ATTN_WIDTH // LANES, tt, LANES), F32),
            pltpu.VMEM((2, tt, LANES), F32),
        ],
        compiler_params=pltpu.CompilerParams(
            dimension_semantics=("parallel",), vmem_limit_bytes=VMEM_LIMIT),
        name="mixout",
    )(x2d, u, u, *o_list, *lse_list, wpool, pscale, wout, gffn, wrt, brt, spread)


def _chunk_copies(loff_s, pcnt_s, gstart_s, tile, local_ref, global_ref, sems, to_global):
    slot = tile % 2
    out = []
    for e in range(N_EXPERTS):
        n = pl.multiple_of(pcnt_s[tile, e], ROW_ALIGN)
        lo = pl.multiple_of(loff_s[tile, e], ROW_ALIGN)
        go = pl.multiple_of(gstart_s[tile, e], ROW_ALIGN)
        loc = local_ref.at[slot, pl.ds(lo, n)]
        glo = global_ref.at[pl.ds(go, n)]
        src, dst = (loc, glo) if to_global else (glo, loc)
        out.append((n > 0, pltpu.make_async_copy(src, dst, sems.at[slot])))
    return out


def _start_all(copies):
    for cond, cp in copies:
        pl.when(cond)(cp.start)


def _wait_all(copies):
    for cond, cp in copies:
        pl.when(cond)(cp.wait)


def _dispatch_kernel(loff_s, pcnt_s, gstart_s, tail_s, h2_ref, pos_ref, xs_ref, loc_scr, zero_scr, sem, tail_sem,
                     spare_sem):
    i = pl.program_id(0)
    tt = TOKEN_TILE

    @pl.when(i == 0)
    def _():
        zero_scr[...] = jnp.zeros_like(zero_scr)
        tails = []
        for e in range(N_EXPERTS):
            n = pl.multiple_of(tail_s[1, e], ROW_ALIGN)
            start = pl.multiple_of(tail_s[0, e], ROW_ALIGN)
            tails.append((n > 0, pltpu.make_async_copy(zero_scr.at[pl.ds(0, n)], xs_ref.at[pl.ds(start, n)], tail_sem)))
        for cond, cp in tails:
            pl.when(cond)(cp.start)
        for cond, cp in tails:
            pl.when(cond)(cp.wait)

    def spare_tile(j):
        return pltpu.make_async_copy(
            zero_scr, xs_ref.at[pl.ds(pl.multiple_of(j * EXPERT_TILE, EXPERT_TILE), EXPERT_TILE)], spare_sem)

    n_tiles = xs_ref.shape[0] // EXPERT_TILE

    @pl.when(i == 0)
    def _():
        lax.fori_loop(tail_s[2, 0], n_tiles, lambda j, c: (spare_tile(j).start(), c)[1], 0)

    @pl.when(i == pl.num_programs(0) - 1)
    def _():
        lax.fori_loop(tail_s[2, 0], n_tiles, lambda j, c: (spare_tile(j).wait(), c)[1], 0)

    pos = pos_ref[0]
    h2 = h2_ref[...]
    slot = i % 2
    for jb in range(LOCAL_ROWS // SORT_BLOCK):
        jio = lax.broadcasted_iota(I32, (SORT_BLOCK, tt), 0) + jb * SORT_BLOCK
        hit = (jio == pos[0:1]) | (jio == pos[1:2]) | (jio == pos[2:3]) | (jio == pos[3:4])
        onehot = jnp.where(hit, 1.0, 0.0).astype(BF16)
        loc_scr[slot, pl.ds(jb * SORT_BLOCK, SORT_BLOCK), :] = jnp.dot(
            onehot, h2, preferred_element_type=F32).astype(BF16)

    @pl.when(i > 0)
    def _():
        _wait_all(_chunk_copies(loff_s, pcnt_s, gstart_s, i - 1, loc_scr, xs_ref, sem, True))

    copies = _chunk_copies(loff_s, pcnt_s, gstart_s, i, loc_scr, xs_ref, sem, True)
    _start_all(copies)

    @pl.when(i == pl.num_programs(0) - 1)
    def _():
        _wait_all(copies)


def _dispatch(loff, pcnt, gstart, tails, h2, pos, n_rows):
    t = h2.shape[0]
    tt = TOKEN_TILE
    grid_spec = pltpu.PrefetchScalarGridSpec(
        num_scalar_prefetch=4,
        grid=(t // tt,),
        in_specs=[
            pl.BlockSpec((tt, D_MODEL), lambda i, *_: (i, 0)),
            pl.BlockSpec((1, 8, tt), lambda i, *_: (i, 0, 0)),
        ],
        out_specs=pl.BlockSpec(memory_space=pl.ANY),
        scratch_shapes=[
            pltpu.VMEM((2, LOCAL_ROWS, D_MODEL), BF16),
            pltpu.VMEM((EXPERT_TILE, D_MODEL), BF16),
            pltpu.SemaphoreType.DMA((2,)),
            pltpu.SemaphoreType.DMA(()),
            pltpu.SemaphoreType.DMA(()),
        ],
    )
    return pl.pallas_call(
        _dispatch_kernel,
        grid_spec=grid_spec,
        out_shape=jax.ShapeDtypeStruct((n_rows, D_MODEL), BF16),
        compiler_params=pltpu.CompilerParams(
            dimension_semantics=("arbitrary",), vmem_limit_bytes=VMEM_LIMIT),
        name="dispatch",
    )(loff, pcnt, gstart, tails, h2, pos)


def _expert_kernel(te_s, nu_s, slot_s, next_s, xs_ref, wgu_hbm, bgu_ref, wd_hbm, bd_ref, ys_ref,
                   wgu_stage, wd_stage, wgu_bf, wd_bf, act_scr, wsem):
    i = pl.program_id(0)
    live = i < nu_s[0]
    expert = te_s[i]
    new_expert = (i == 0) | (expert != te_s[jnp.maximum(i - 1, 0)])

    def weight_copies(e, slot):
        return (pltpu.make_async_copy(wgu_hbm.at[e], wgu_stage.at[slot], wsem.at[0, slot]),
                pltpu.make_async_copy(wd_hbm.at[e], wd_stage.at[slot], wsem.at[1, slot]))

    @pl.when(live & new_expert)
    def _():
        slot = slot_s[expert]

        @pl.when(i == 0)
        def _():
            for cp in weight_copies(expert, slot):
                cp.start()

        for cp in weight_copies(expert, slot):
            cp.wait()
        wgu_bf[...] = wgu_stage[slot].astype(BF16)
        wd_bf[...] = wd_stage[slot].astype(BF16)
        upcoming = next_s[expert]

        @pl.when(upcoming < N_EXPERTS)
        def _():
            for cp in weight_copies(upcoming, 1 - slot):
                cp.start()

    @pl.when(live)
    def _():
        x = xs_ref[...]
        n_chunks = D_EXPERT // EXPERT_CHUNK

        def gate_up(c):
            gc = pl.ds(c * EXPERT_CHUNK, EXPERT_CHUNK)
            uc = pl.ds(D_EXPERT + c * EXPERT_CHUNK, EXPERT_CHUNK)
            return (jnp.dot(x, wgu_bf[:, gc], preferred_element_type=F32) + bgu_ref[0, :, gc],
                    jnp.dot(x, wgu_bf[:, uc], preferred_element_type=F32) + bgu_ref[0, :, uc])

        g, u = gate_up(0)
        for c in range(n_chunks):
            nxt = gate_up(c + 1) if c + 1 < n_chunks else None
            g = jnp.minimum(g, SWIGLU_LIMIT)
            u = jnp.clip(u, -SWIGLU_LIMIT, SWIGLU_LIMIT)
            act_scr[:, pl.ds(c * EXPERT_CHUNK, EXPERT_CHUNK)] = (
                (u + 1.0) * (g * jax.nn.sigmoid(SWIGLU_ALPHA * g))).astype(BF16)
            if nxt is not None:
                g, u = nxt
        y = jnp.dot(act_scr[...], wd_bf[...], preferred_element_type=F32) + bd_ref[0]
        ys_ref[...] = y.astype(BF16)

    @pl.when(jnp.logical_not(live))
    def _():
        ys_ref[...] = jnp.zeros_like(ys_ref)


def _experts(tile_expert, n_used, stage_slot, next_expert, xs, wgu, bgu, wd, bd):
    n_rows = xs.shape[0]
    tm = EXPERT_TILE
    live = lambda i, te, nu: jnp.minimum(i, nu[0] - 1)
    grid_spec = pltpu.PrefetchScalarGridSpec(
        num_scalar_prefetch=4,
        grid=(n_rows // tm,),
        in_specs=[
            pl.BlockSpec((tm, D_MODEL), lambda i, te, nu, *_: (live(i, te, nu), 0)),
            pl.BlockSpec(memory_space=pl.ANY),
            pl.BlockSpec((1, 1, 2 * D_EXPERT), lambda i, te, nu, *_: (te[live(i, te, nu)], 0, 0)),
            pl.BlockSpec(memory_space=pl.ANY),
            pl.BlockSpec((1, 1, D_MODEL), lambda i, te, nu, *_: (te[live(i, te, nu)], 0, 0)),
        ],
        out_specs=pl.BlockSpec((tm, D_MODEL), lambda i, te, nu, *_: (i, 0)),
        scratch_shapes=[
            pltpu.VMEM((2, D_MODEL, 2 * D_EXPERT), F32),
            pltpu.VMEM((2, D_EXPERT, D_MODEL), F32),
            pltpu.VMEM((D_MODEL, 2 * D_EXPERT), BF16),
            pltpu.VMEM((D_EXPERT, D_MODEL), BF16),
            pltpu.VMEM((tm, D_EXPERT), BF16),
            pltpu.SemaphoreType.DMA((2, 2)),
        ],
    )
    return pl.pallas_call(
        _expert_kernel,
        grid_spec=grid_spec,
        out_shape=jax.ShapeDtypeStruct((n_rows, D_MODEL), BF16),
        compiler_params=pltpu.CompilerParams(
            dimension_semantics=("arbitrary",), vmem_limit_bytes=VMEM_LIMIT),
        name="experts",
    )(tile_expert, n_used, stage_slot, next_expert, xs, wgu, bgu, wd, bd)


def _combine_kernel(loff_s, pcnt_s, gstart_s, x1_ref, post_ref, gatet_ref, p_ref, gple_ref, wg_ref, wp_ref,
                    gfin_ref, ys_ref, out_ref, loc_scr, sem):
    i = pl.program_id(0)
    tt = TOKEN_TILE

    @pl.when(i == 0)
    def _():
        loc_scr[...] = jnp.zeros_like(loc_scr)
        _start_all(_chunk_copies(loff_s, pcnt_s, gstart_s, i, loc_scr, ys_ref, sem, False))

    _wait_all(_chunk_copies(loff_s, pcnt_s, gstart_s, i, loc_scr, ys_ref, sem, False))

    @pl.when(i + 1 < pl.num_programs(0))
    def _():
        _start_all(_chunk_copies(loff_s, pcnt_s, gstart_s, i + 1, loc_scr, ys_ref, sem, False))

    slot = i % 2

    post = post_ref[...]
    gatet = gatet_ref[...]
    moe = jnp.zeros((tt, D_MODEL), F32)
    for jb in range(LOCAL_ROWS // GATHER_BLOCK):
        jio = lax.broadcasted_iota(I32, (tt, GATHER_BLOCK), 1) + jb * GATHER_BLOCK
        w = jnp.zeros((tt, GATHER_BLOCK), F32)
        for k in range(TOP_K):
            w = w + jnp.where(jio == post[:, k:k + 1], gatet[:, k:k + 1], 0.0)
        y = loc_scr[slot, pl.ds(jb * GATHER_BLOCK, GATHER_BLOCK), :]
        moe = moe + jnp.dot(w.astype(BF16), y, preferred_element_type=F32)

    x2 = x1_ref[...] + moe
    gate = jax.nn.sigmoid(jnp.dot(_rms(x2, gple_ref[...]).astype(BF16), wg_ref[...], preferred_element_type=F32))
    emb = jnp.dot(p_ref[...].astype(BF16), wp_ref[...], preferred_element_type=F32)
    out_ref[...] = _rms(x2 + emb * gate, gfin_ref[...])


def _combine(loff, pcnt, gstart, x1, post, gatet, p2d, gple, wg, wp, gfin, ys):
    t = x1.shape[0]
    tt = TOKEN_TILE
    row = lambda i, *_: (i, 0)
    const2 = lambda i, *_: (0, 0)
    grid_spec = pltpu.PrefetchScalarGridSpec(
        num_scalar_prefetch=3,
        grid=(t // tt,),
        in_specs=[
            pl.BlockSpec((tt, D_MODEL), row),
            pl.BlockSpec((tt, 8), row),
            pl.BlockSpec((tt, 8), row),
            pl.BlockSpec((tt, PLE_DIM), row),
            pl.BlockSpec((1, D_MODEL), const2),
            pl.BlockSpec((D_MODEL, D_MODEL), const2),
            pl.BlockSpec((PLE_DIM, D_MODEL), const2),
            pl.BlockSpec((1, D_MODEL), const2),
            pl.BlockSpec(memory_space=pl.ANY),
        ],
        out_specs=pl.BlockSpec((tt, D_MODEL), row),
        scratch_shapes=[pltpu.VMEM((2, LOCAL_ROWS, D_MODEL), BF16), pltpu.SemaphoreType.DMA((2,))],
    )
    return pl.pallas_call(
        _combine_kernel,
        grid_spec=grid_spec,
        out_shape=jax.ShapeDtypeStruct((t, D_MODEL), F32),
        compiler_params=pltpu.CompilerParams(
            dimension_semantics=("arbitrary",), vmem_limit_bytes=VMEM_LIMIT),
        name="combine",
    )(loff, pcnt, gstart, x1, post, gatet, p2d, gple, wg, wp, gfin, ys)


def _routing_tables(pc16):
    pcnt = pc16 * ROW_ALIGN
    loff = jnp.cumsum(pcnt, axis=1) - pcnt
    seg = jnp.sum(pcnt, axis=0)
    segpad = (seg + EXPERT_TILE - 1) // EXPERT_TILE * EXPERT_TILE
    seg_end = jnp.cumsum(segpad)
    ebase = seg_end - segpad
    gstart = ebase[None, :] + jnp.cumsum(pcnt, axis=0) - pcnt
    tails = jnp.stack([ebase + seg, segpad - seg, jnp.broadcast_to(seg_end[-1] // EXPERT_TILE, seg.shape)])
    return pcnt, loff, gstart, tails, seg_end


def kernel(x, p, g_mix, w_in, w_pool, pool_scale, rel_bias, w_out, g_ffn, w_router, b_router, w_gate_up,
           b_gate_up, w_down, b_down, g_ple, w_ple_gate, w_ple_proj, g_final):
    b, s, d = x.shape
    t = b * s
    x2d = x.reshape(t, d)
    for layer in range(w_in.shape[0]):
        u, *qkv_by_dil = _inproj(x2d, g_mix[layer][None], w_in[layer].astype(BF16))
        o_list, lse_list = [], []
        for (window, dil), qkv_d, (rq, nq) in zip(DILATED_BRANCHES, qkv_by_dil, ATTN_STEP):
            o_d, lse_d = _attn_branch(qkv_d, _bias_tables(rel_bias, window, dil), b, s, dil, rq, nq)
            o_list.append(o_d)
            lse_list.append(lse_d)
        brt = jnp.broadcast_to(b_router[layer][:, None], (N_EXPERTS, 128))
        x1, h2, pos, gates, pc = _mixout(
            x2d, u, o_list, lse_list, w_pool[layer].astype(BF16), pool_scale[layer][None],
            w_out[layer].astype(BF16), g_ffn[layer][None], w_router[layer].T, brt, s)

        pc16 = pc[:, :, 0].astype(I32)
        pcnt, loff, gstart, tails, seg_end = _routing_tables(pc16)
        n_rows = t * TOP_K + N_EXPERTS * (t // TOKEN_TILE) * (ROW_ALIGN - 1) + N_EXPERTS * (EXPERT_TILE - 1)
        n_rows = (n_rows + EXPERT_TILE - 1) // EXPERT_TILE * EXPERT_TILE
        n_tiles = n_rows // EXPERT_TILE
        tile_start = jnp.arange(n_tiles, dtype=I32) * EXPERT_TILE
        tile_expert = jnp.minimum(jnp.sum(seg_end[None, :] <= tile_start[:, None], axis=1), N_EXPERTS - 1).astype(I32)
        n_used = (seg_end[-1] // EXPERT_TILE).astype(I32)[None]
        xs = _dispatch(loff, pcnt, gstart, tails, h2, pos, n_rows)
        has_rows = jnp.diff(seg_end, prepend=0) > 0
        experts = jnp.arange(N_EXPERTS, dtype=I32)
        stage_slot = ((jnp.cumsum(has_rows) - 1) % 2).astype(I32)
        next_expert = jnp.min(jnp.where(has_rows[None, :] & (experts[None, :] > experts[:, None]),
                                        experts[None, :], N_EXPERTS), axis=1).astype(I32)
        ys = _experts(tile_expert, n_used, stage_slot, next_expert, xs, w_gate_up[layer],
                      b_gate_up[layer][:, None, :], w_down[layer], b_down[layer][:, None, :])

        post = pos.transpose(0, 2, 1).reshape(t, 8)
        gatet = gates.transpose(0, 2, 1).reshape(t, 8)
        assert layer == w_in.shape[0] - 1, "single-layer pipeline: the final norm is fused into combine"
        x2d = _combine(loff, pcnt, gstart, x1, post, gatet, p[layer].reshape(t, PLE_DIM), g_ple[layer][None],
                       w_ple_gate[layer].astype(BF16), w_ple_proj[layer].astype(BF16), g_final[None], ys)
    return x2d.reshape(b, s, d)
```

```python
import functools
import math

import jax
import jax.numpy as jnp
from jax import lax
from jax.experimental import pallas as pl
from jax.experimental.pallas import tpu as pltpu

F32 = jnp.float32
BF16 = jnp.bfloat16
I32 = jnp.int32

D_MODEL = 1024
POOL_WIDTH = 512
POOL_WINDOWS = (2, 4, 8, 16)
POOL_GROUP_DIM = 128
ATTN_WIDTH = 512
QKV_WIDTH = 3 * ATTN_WIDTH
HEAD_DIM = 64
N_HEADS = 8
DILATED_BRANCHES = ((128, 1), (512, 4), (2048, 16))
ATTN_BLOCK = 128
N_REL_BUCKETS = 32
REL_MAX_EXACT = 16
REL_MAX_DISTANCE = 2048
N_EXPERTS = 32
TOP_K = 4
D_EXPERT = 1024
SWIGLU_LIMIT = 7.0
SWIGLU_ALPHA = 1.702
PLE_DIM = 256
NORM_EPS = 1e-6
NEG_INF = -1e30

LANES = 128
POOL_HALO = 16
TOKEN_TILE = 512
ROW_ALIGN = 16
EXPERT_TILE = 512
EXPERT_CHUNK = 256
LOCAL_ROWS = TOKEN_TILE * TOP_K + N_EXPERTS * ROW_ALIGN
SORT_BLOCK = 256
GATHER_BLOCK = 512
ATTN_STEP = ((1, 8), (1, 8), (4, 2))
VMEM_LIMIT = 56 * 1024 * 1024


def _rms(x, g):
    return x * lax.rsqrt(jnp.mean(x * x, axis=-1, keepdims=True) + NORM_EPS) * g


def _inproj_kernel(x_ref, g_ref, w_ref, u_ref, nat_ref, d4_ref, d16_ref, z_scr):
    h = _rms(x_ref[...], g_ref[...]).astype(BF16)
    part = ATTN_WIDTH
    slabs = part // LANES

    def project(i):
        return jnp.dot(h, w_ref[:, pl.ds(i * part, part)], preferred_element_type=F32)

    def regroup(z, i):
        col0 = (i - 1) * part
        nat_ref[:, pl.ds(col0, part)] = z.astype(BF16)
        for c in range(slabs):
            z_scr[(i - 1) * slabs + c] = z[:, c * LANES:(c + 1) * LANES]
        for dil, ref in ((4, d4_ref), (16, d16_ref)):
            rows = TOKEN_TILE // dil
            for r in range(dil):
                for c in range(slabs):
                    ref[:, pl.ds(r * QKV_WIDTH + col0 + c * LANES, LANES)] = (
                        z_scr[(i - 1) * slabs + c, pl.ds(r, rows, stride=dil), :].astype(BF16))

    assert POOL_WIDTH == part and QKV_WIDTH == 3 * part
    z_prev = project(0)
    for i in range(1, 4):
        z_next = project(i)
        if i == 1:
            u_ref[...] = z_prev
        else:
            regroup(z_prev, i - 1)
        z_prev = z_next
    regroup(z_prev, 3)


def _inproj(x2d, g, w_bf16):
    t = x2d.shape[0]
    in_w = w_bf16.shape[1]
    tt = TOKEN_TILE
    return pl.pallas_call(
        _inproj_kernel,
        grid=(t // tt,),
        in_specs=[
            pl.BlockSpec((tt, D_MODEL), lambda i: (i, 0)),
            pl.BlockSpec((1, D_MODEL), lambda i: (0, 0)),
            pl.BlockSpec((D_MODEL, in_w), lambda i: (0, 0)),
        ],
        out_specs=[
            pl.BlockSpec((tt, POOL_WIDTH), lambda i: (i, 0)),
            pl.BlockSpec((tt, QKV_WIDTH), lambda i: (i, 0)),
            pl.BlockSpec((tt // 4, 4 * QKV_WIDTH), lambda i: (i, 0)),
            pl.BlockSpec((tt // 16, 16 * QKV_WIDTH), lambda i: (i, 0)),
        ],
        out_shape=[
            jax.ShapeDtypeStruct((t, POOL_WIDTH), F32),
            jax.ShapeDtypeStruct((t, QKV_WIDTH), BF16),
            jax.ShapeDtypeStruct((t // 4, 4 * QKV_WIDTH), BF16),
            jax.ShapeDtypeStruct((t // 16, 16 * QKV_WIDTH), BF16),
        ],
        scratch_shapes=[pltpu.VMEM((QKV_WIDTH // LANES, tt, LANES), F32)],
        compiler_params=pltpu.CompilerParams(
            dimension_semantics=("parallel",), vmem_limit_bytes=VMEM_LIMIT),
        name="inproj",
    )(x2d, g, w_bf16)


def _attn_kernel(main_ref, prev_ref, tab_ref, o_ref, lse_ref, k_scr, vt_scr, *, rq, nq):
    blk = ATTN_BLOCK
    first_group = pl.program_id(2) == 0
    nt = (((1,), (1,)), ((), ()))

    for ri in range(rq):
        base = ri * QKV_WIDTH
        k_scr[0:blk, :] = prev_ref[:, pl.ds(base + ATTN_WIDTH, ATTN_WIDTH)]
        k_scr[blk:, :] = main_ref[:, pl.ds(base + ATTN_WIDTH, ATTN_WIDTH)]
        vt_scr[:, 0:blk] = prev_ref[:, pl.ds(base + 2 * ATTN_WIDTH, ATTN_WIDTH)].astype(F32).T.astype(BF16)
        for j in range(nq):
            vt_scr[:, pl.ds((j + 1) * blk, blk)] = (
                main_ref[pl.ds(j * blk, blk), pl.ds(base + 2 * ATTN_WIDTH, ATTN_WIDTH)].astype(F32).T.astype(BF16))
        for j in range(nq):
            rows = pl.ds(j * blk, blk)
            keys = pl.ds(j * blk, 2 * blk)
            scores = []
            for h in range(N_HEADS):
                q = main_ref[rows, pl.ds(base + h * HEAD_DIM, HEAD_DIM)] * 0.125
                k2 = k_scr[keys, pl.ds(h * HEAD_DIM, HEAD_DIM)]
                scores.append(lax.dot_general(k2, q, nt, preferred_element_type=F32))
            probs, inv_l, lse_parts = [], [], []
            for h in range(N_HEADS):
                s = scores[h] + tab_ref[h]
                if j == 0:
                    pen = jnp.where(first_group, NEG_INF, 0.0).astype(F32)
                    s = jnp.concatenate([s[:blk] + pen, s[blk:]], axis=0)
                m = jnp.max(s, axis=0, keepdims=True)
                p = jnp.exp(s - m)
                l = jnp.sum(p, axis=0, keepdims=True)
                probs.append(p.astype(BF16))
                inv_l.append(1.0 / l)
                lse_parts.append(m + jnp.log(l))
            o_parts = []
            for h in range(N_HEADS):
                vt2 = vt_scr[pl.ds(h * HEAD_DIM, HEAD_DIM), keys]
                o_parts.append(jnp.dot(vt2, probs[h], preferred_element_type=F32) * inv_l[h])
            o_t = jnp.concatenate(o_parts, axis=0)
            o_ref[rows, pl.ds(ri * ATTN_WIDTH, ATTN_WIDTH)] = o_t.T.astype(BF16)
            lse_t = jnp.concatenate(lse_parts + [jnp.zeros((LANES - N_HEADS, blk), F32)], axis=0)
            lse_ref[rows, pl.ds(ri * LANES, LANES)] = lse_t.T


def _attn_branch(qkv_d, tab, batch, seq, dil, rq, nq):
    sub = seq // dil
    nb = sub // ATTN_BLOCK
    groups = nb // nq
    return pl.pallas_call(
        functools.partial(_attn_kernel, rq=rq, nq=nq),
        grid=(batch, dil // rq, groups),
        in_specs=[
            pl.BlockSpec((nq * ATTN_BLOCK, rq * QKV_WIDTH), lambda b, r, g: (b * groups + g, r)),
            pl.BlockSpec((ATTN_BLOCK, rq * QKV_WIDTH), lambda b, r, g: (b * nb + jnp.maximum(g * nq - 1, 0), r)),
            pl.BlockSpec((N_HEADS, 2 * ATTN_BLOCK, ATTN_BLOCK), lambda b, r, g: (0, 0, 0)),
        ],
        out_specs=[
            pl.BlockSpec((nq * ATTN_BLOCK, rq * ATTN_WIDTH), lambda b, r, g: (b * groups + g, r)),
            pl.BlockSpec((nq * ATTN_BLOCK, rq * LANES), lambda b, r, g: (b * groups + g, r)),
        ],
        out_shape=[
            jax.ShapeDtypeStruct((batch * sub, dil * ATTN_WIDTH), BF16),
            jax.ShapeDtypeStruct((batch * sub, dil * LANES), F32),
        ],
        scratch_shapes=[
            pltpu.VMEM(((nq + 1) * ATTN_BLOCK, ATTN_WIDTH), BF16),
            pltpu.VMEM((ATTN_WIDTH, (nq + 1) * ATTN_BLOCK), BF16),
        ],
        compiler_params=pltpu.CompilerParams(
            dimension_semantics=("parallel", "parallel", "parallel"), vmem_limit_bytes=VMEM_LIMIT),
        name="attn",
    )(qkv_d, qkv_d, tab)


def _t5_bucket(dist):
    n = jnp.maximum(dist, 1).astype(F32)
    large = REL_MAX_EXACT + (jnp.log(n / REL_MAX_EXACT) / math.log(REL_MAX_DISTANCE / REL_MAX_EXACT)
                             * (N_REL_BUCKETS - REL_MAX_EXACT)).astype(I32)
    large = jnp.minimum(large, N_REL_BUCKETS - 1)
    return jnp.where(dist < REL_MAX_EXACT, dist, large)


def _shifted_rows(w, n):
    lead = w.shape[:-1]
    width = w.shape[-1]
    flat = jnp.tile(w, (1,) * len(lead) + (n + 1,))[..., :n * (width + 1)]
    return flat.reshape(lead + (n, width + 1))[..., :n]


def _bias_tables(rel_bias, window, dil):
    blk = ATTN_BLOCK
    assert window // dil == blk
    f = rel_bias[_t5_bucket(jnp.arange(blk + 1) * dil)].T.astype(F32)
    neg = jnp.full((N_HEADS, blk - 1), NEG_INF, F32)
    neg1 = jnp.full((N_HEADS, 1), NEG_INF, F32)
    w_prev = jnp.concatenate([neg, f[:, :0:-1], neg1], axis=1)
    prev = _shifted_rows(w_prev, blk)[:, :, ::-1]
    w_cur = jnp.concatenate([neg, f[:, :blk], neg1], axis=1)
    cur = _shifted_rows(w_cur, blk)[:, ::-1, :]
    return jnp.concatenate([prev, cur], axis=1)


def _mixout_kernel(x_ref, u_ref, uh_ref, o1_ref, o4_ref, o16_ref, l1_ref, l4_ref, l16_ref,
                   wpool_ref, pscale_ref, wout_ref, gffn_ref, wrt_ref, brt_ref, spread_ref,
                   x1_ref, h2_ref, pos_ref, gate_ref, pc_ref,
                   ext_scr, mix_scr, o_scr, l_scr, *, tiles_per_seq):
    i = pl.program_id(0)
    tt = TOKEN_TILE
    seq_tile = i % tiles_per_seq

    halo = uh_ref[...]
    ext_scr[0:POOL_HALO, :] = jnp.where(seq_tile == 0, jnp.zeros_like(halo), halo)
    ext_scr[POOL_HALO:, :] = u_ref[...]
    tpos = seq_tile * tt + lax.broadcasted_iota(I32, (tt, 1), 0)
    for gi, w in enumerate(POOL_WINDOWS):
        cols = pl.ds(gi * POOL_GROUP_DIM, POOL_GROUP_DIM)
        tok = ext_scr[pl.ds(POOL_HALO, tt), cols]
        acc = tok
        for j in range(1, w):
            acc = acc + ext_scr[pl.ds(POOL_HALO - j, tt), cols]
        cnt = jnp.minimum(tpos + 1, w).astype(F32)
        pooled = (acc / cnt - tok).astype(BF16)
        mixed = jnp.dot(pooled, wpool_ref[gi], preferred_element_type=F32) * pscale_ref[:, cols]
        mix_scr[:, cols] = mixed.astype(BF16)

    o_slabs = ATTN_WIDTH // LANES
    for bi, (dil, o_ref, l_ref) in enumerate(((4, o4_ref, l4_ref), (16, o16_ref, l16_ref))):
        rows = tt // dil
        for r in range(dil):
            l_scr[bi, pl.ds(r, rows, stride=dil), :] = l_ref[:, pl.ds(r * LANES, LANES)]
            for c in range(o_slabs):
                o_scr[bi * o_slabs + c, pl.ds(r, rows, stride=dil), :] = (
                    o_ref[:, pl.ds(r * ATTN_WIDTH + c * LANES, LANES)].astype(F32))
    l1, l4, l16 = l1_ref[...], l_scr[0], l_scr[1]
    lm = jnp.maximum(jnp.maximum(l1, l4), l16)
    e1, e4, e16 = jnp.exp(l1 - lm), jnp.exp(l4 - lm), jnp.exp(l16 - lm)
    inv = 1.0 / (e1 + e4 + e16)

    def per_feature(w):
        hi = w.astype(BF16)
        lo = (w - hi.astype(F32)).astype(BF16)
        return (jnp.dot(hi, spread_ref[...], preferred_element_type=F32)
                + jnp.dot(lo, spread_ref[...], preferred_element_type=F32))

    w1, w4, w16 = per_feature(e1 * inv), per_feature(e4 * inv), per_feature(e16 * inv)
    for c in range(o_slabs):
        sl = pl.ds(c * LANES, LANES)
        cs = slice(c * LANES, (c + 1) * LANES)
        o = w1[:, cs] * o1_ref[:, sl].astype(F32) + w4[:, cs] * o_scr[c] + w16[:, cs] * o_scr[o_slabs + c]
        mix_scr[:, pl.ds(POOL_WIDTH + c * LANES, LANES)] = o.astype(BF16)

    x1 = x_ref[...] + jnp.dot(mix_scr[...], wout_ref[...], preferred_element_type=F32)
    x1_ref[...] = x1

    h2 = _rms(x1, gffn_ref[...])
    h2_ref[...] = h2.astype(BF16)
    nt = (((1,), (1,)), ((), ()))
    wr = wrt_ref[...]
    wr_hi = wr.astype(BF16)
    wr_lo = (wr - wr_hi.astype(F32)).astype(BF16)
    h2_hi = h2.astype(BF16)
    h2_lo = (h2 - h2_hi.astype(F32)).astype(BF16)
    logits = (lax.dot_general(wr_hi, h2_hi, nt, preferred_element_type=F32)
              + lax.dot_general(wr_hi, h2_lo, nt, preferred_element_type=F32)
              + lax.dot_general(wr_lo, h2_hi, nt, preferred_element_type=F32))
    logits = logits + brt_ref[:, 0:1]
    eio = lax.broadcasted_iota(I32, (N_EXPERTS, tt), 0)
    sel, val = [], []
    for _ in range(TOP_K):
        m = jnp.max(logits, axis=0, keepdims=True)
        idx = jnp.min(jnp.where(logits == m, eio, N_EXPERTS), axis=0, keepdims=True)
        sel.append(idx)
        val.append(m)
        logits = jnp.where(eio == idx, -jnp.inf, logits)
    ex = [jnp.exp(v - val[0]) for v in val]
    den = ex[0] + ex[1] + ex[2] + ex[3]
    gates = [e / den for e in ex]

    hot = [(eio == s) for s in sel]
    onehot = (hot[0] | hot[1] | hot[2] | hot[3]).astype(F32)
    ti = lax.broadcasted_iota(I32, (tt, tt), 0)
    tj = lax.broadcasted_iota(I32, (tt, tt), 1)
    before = (ti < tj).astype(BF16)
    rank = jnp.dot(onehot.astype(BF16), before, preferred_element_type=F32)
    cnt = jnp.sum(onehot, axis=1, keepdims=True)
    pc = jnp.floor((cnt + (ROW_ALIGN - 1)) * (1.0 / ROW_ALIGN))
    pcb = jnp.broadcast_to(pc, (N_EXPERTS, 128))
    pc_ref[0] = pcb
    pcm = jnp.concatenate([pcb, jnp.zeros((128 - N_EXPERTS, 128), F32)], axis=0).astype(BF16)
    li = lax.broadcasted_iota(I32, (N_EXPERTS, 128), 0)
    lj = lax.broadcasted_iota(I32, (N_EXPERTS, 128), 1)
    lower = (lj < li).astype(BF16)
    off = jnp.dot(lower, pcm, preferred_element_type=F32)[:, 0:1] * float(ROW_ALIGN)
    where_to = off + rank
    zero_i = jnp.zeros((1, tt), I32)
    zero_f = jnp.zeros((1, tt), F32)
    pos_rows = [jnp.sum(jnp.where(hk, where_to, 0.0), axis=0, keepdims=True).astype(I32) for hk in hot]
    pos_ref[0] = jnp.concatenate(pos_rows + [zero_i] * (8 - TOP_K), axis=0)
    gate_ref[0] = jnp.concatenate(gates + [zero_f] * (8 - TOP_K), axis=0)


def _mixout(x2d, u, o_list, lse_list, wpool, pscale, wout, gffn, wrt, brt, seq):
    t = x2d.shape[0]
    tt = TOKEN_TILE
    n_tt = t // tt
    halo_blocks = tt // POOL_HALO
    row = lambda i: (i, 0)
    const2 = lambda i: (0, 0)
    dils = [dil for _, dil in DILATED_BRANCHES]
    spread = (jnp.arange(LANES)[:, None] == jnp.arange(ATTN_WIDTH)[None, :] // HEAD_DIM).astype(BF16)
    return pl.pallas_call(
        functools.partial(_mixout_kernel, tiles_per_seq=seq // tt),
        grid=(n_tt,),
        in_specs=[
            pl.BlockSpec((tt, D_MODEL), row),
            pl.BlockSpec((tt, POOL_WIDTH), row),
            pl.BlockSpec((POOL_HALO, POOL_WIDTH), lambda i: (jnp.maximum(i * halo_blocks - 1, 0), 0)),
            *[pl.BlockSpec((tt // dil, dil * ATTN_WIDTH), row) for dil in dils],
            *[pl.BlockSpec((tt // dil, dil * LANES), row) for dil in dils],
            pl.BlockSpec((len(POOL_WINDOWS), POOL_GROUP_DIM, POOL_GROUP_DIM), lambda i: (0, 0, 0)),
            pl.BlockSpec((1, POOL_WIDTH), const2),
            pl.BlockSpec((D_MODEL, D_MODEL), const2),
            pl.BlockSpec((1, D_MODEL), const2),
            pl.BlockSpec((N_EXPERTS, D_MODEL), const2),
            pl.BlockSpec((N_EXPERTS, 128), const2),
            pl.BlockSpec((LANES, ATTN_WIDTH), const2),
        ],
        out_specs=[
            pl.BlockSpec((tt, D_MODEL), row),
            pl.BlockSpec((tt, D_MODEL), row),
            pl.BlockSpec((1, 8, tt), lambda i: (i, 0, 0)),
            pl.BlockSpec((1, 8, tt), lambda i: (i, 0, 0)),
            pl.BlockSpec((1, N_EXPERTS, 128), lambda i: (i, 0, 0)),
        ],
        out_shape=[
            jax.ShapeDtypeStruct((t, D_MODEL), F32),
            jax.ShapeDtypeStruct((t, D_MODEL), BF16),
            jax.ShapeDtypeStruct((n_tt, 8, tt), I32),
            jax.ShapeDtypeStruct((n_tt, 8, tt), F32),
            jax.ShapeDtypeStruct((n_tt, N_EXPERTS, 128), F32),
        ],
        scratch_shapes=[
            pltpu.VMEM((tt + POOL_HALO, POOL_WIDTH), F32),
            pltpu.VMEM((tt, D_MODEL), BF16),
            pltpu.VMEM((2 * ATTN_WIDTH // LANES, tt, LANES), F32),
            pltpu.VMEM((2, tt, LANES), F32),
        ],
        compiler_params=pltpu.CompilerParams(
            dimension_semantics=("parallel",), vmem_limit_bytes=VMEM_LIMIT),
        name="mixout",
    )(x2d, u, u, *o_list, *lse_list, wpool, pscale, wout, gffn, wrt, brt, spread)


def _chunk_copies(loff_s, pcnt_s, gstart_s, tile, local_ref, global_ref, sems, to_global):
    slot = tile % 2
    out = []
    for e in range(N_EXPERTS):
        n = pl.multiple_of(pcnt_s[tile, e], ROW_ALIGN)
        lo = pl.multiple_of(loff_s[tile, e], ROW_ALIGN)
        go = pl.multiple_of(gstart_s[tile, e], ROW_ALIGN)
        loc = local_ref.at[slot, pl.ds(lo, n)]
        glo = global_ref.at[pl.ds(go, n)]
        src, dst = (loc, glo) if to_global else (glo, loc)
        out.append((n > 0, pltpu.make_async_copy(src, dst, sems.at[slot])))
    return out


def _start_all(copies):
    for cond, cp in copies:
        pl.when(cond)(cp.start)


def _wait_all(copies):
    for cond, cp in copies:
        pl.when(cond)(cp.wait)


def _dispatch_kernel(loff_s, pcnt_s, gstart_s, tail_s, h2_ref, pos_ref, xs_ref, loc_scr, zero_scr, sem, tail_sem,
                     spare_sem):
    i = pl.program_id(0)
    tt = TOKEN_TILE

    @pl.when(i == 0)
    def _():
        zero_scr[...] = jnp.zeros_like(zero_scr)
        tails = []
        for e in range(N_EXPERTS):
            n = pl.multiple_of(tail_s[1, e], ROW_ALIGN)
            start = pl.multiple_of(tail_s[0, e], ROW_ALIGN)
            tails.append((n > 0, pltpu.make_async_copy(zero_scr.at[pl.ds(0, n)], xs_ref.at[pl.ds(start, n)], tail_sem)))
        for cond, cp in tails:
            pl.when(cond)(cp.start)
        for cond, cp in tails:
            pl.when(cond)(cp.wait)

    def spare_tile(j):
        return pltpu.make_async_copy(
            zero_scr, xs_ref.at[pl.ds(pl.multiple_of(j * EXPERT_TILE, EXPERT_TILE), EXPERT_TILE)], spare_sem)

    n_tiles = xs_ref.shape[0] // EXPERT_TILE

    @pl.when(i == 0)
    def _():
        lax.fori_loop(tail_s[2, 0], n_tiles, lambda j, c: (spare_tile(j).start(), c)[1], 0)

    @pl.when(i == pl.num_programs(0) - 1)
    def _():
        lax.fori_loop(tail_s[2, 0], n_tiles, lambda j, c: (spare_tile(j).wait(), c)[1], 0)

    pos = pos_ref[0]
    h2 = h2_ref[...]
    slot = i % 2
    used_rows = loff_s[i, N_EXPERTS - 1] + pcnt_s[i, N_EXPERTS - 1]
    for jb in range(LOCAL_ROWS // SORT_BLOCK):
        def sort_block(jb=jb):
            jio = lax.broadcasted_iota(I32, (SORT_BLOCK, tt), 0) + jb * SORT_BLOCK
            hit = (jio == pos[0:1]) | (jio == pos[1:2]) | (jio == pos[2:3]) | (jio == pos[3:4])
            onehot = jnp.where(hit, 1.0, 0.0).astype(BF16)
            loc_scr[slot, pl.ds(jb * SORT_BLOCK, SORT_BLOCK), :] = jnp.dot(
                onehot, h2, preferred_element_type=F32).astype(BF16)

        if (jb + 1) * SORT_BLOCK <= TOKEN_TILE * TOP_K:
            sort_block()
        else:
            pl.when(jb * SORT_BLOCK < used_rows)(sort_block)

    @pl.when(i > 0)
    def _():
        _wait_all(_chunk_copies(loff_s, pcnt_s, gstart_s, i - 1, loc_scr, xs_ref, sem, True))

    copies = _chunk_copies(loff_s, pcnt_s, gstart_s, i, loc_scr, xs_ref, sem, True)
    _start_all(copies)

    @pl.when(i == pl.num_programs(0) - 1)
    def _():
        _wait_all(copies)


def _dispatch(loff, pcnt, gstart, tails, h2, pos, n_rows):
    t = h2.shape[0]
    tt = TOKEN_TILE
    grid_spec = pltpu.PrefetchScalarGridSpec(
        num_scalar_prefetch=4,
        grid=(t // tt,),
        in_specs=[
            pl.BlockSpec((tt, D_MODEL), lambda i, *_: (i, 0)),
            pl.BlockSpec((1, 8, tt), lambda i, *_: (i, 0, 0)),
        ],
        out_specs=pl.BlockSpec(memory_space=pl.ANY),
        scratch_shapes=[
            pltpu.VMEM((2, LOCAL_ROWS, D_MODEL), BF16),
            pltpu.VMEM((EXPERT_TILE, D_MODEL), BF16),
            pltpu.SemaphoreType.DMA((2,)),
            pltpu.SemaphoreType.DMA(()),
            pltpu.SemaphoreType.DMA(()),
        ],
    )
    return pl.pallas_call(
        _dispatch_kernel,
        grid_spec=grid_spec,
        out_shape=jax.ShapeDtypeStruct((n_rows, D_MODEL), BF16),
        compiler_params=pltpu.CompilerParams(
            dimension_semantics=("arbitrary",), vmem_limit_bytes=VMEM_LIMIT),
        name="dispatch",
    )(loff, pcnt, gstart, tails, h2, pos)


def _expert_kernel(te_s, nu_s, slot_s, next_s, xs_ref, wgu_hbm, bgu_ref, wd_hbm, bd_ref, ys_ref,
                   wgu_stage, wd_stage, wgu_bf, wd_bf, act_scr, wsem):
    i = pl.program_id(0)
    live = i < nu_s[0]
    expert = te_s[i]
    new_expert = (i == 0) | (expert != te_s[jnp.maximum(i - 1, 0)])

    def weight_copies(e, slot):
        return (pltpu.make_async_copy(wgu_hbm.at[e], wgu_stage.at[slot], wsem.at[0, slot]),
                pltpu.make_async_copy(wd_hbm.at[e], wd_stage.at[slot], wsem.at[1, slot]))

    @pl.when(live & new_expert)
    def _():
        slot = slot_s[expert]

        @pl.when(i == 0)
        def _():
            for cp in weight_copies(expert, slot):
                cp.start()

        for cp in weight_copies(expert, slot):
            cp.wait()
        wgu_bf[...] = wgu_stage[slot].astype(BF16)
        wd_bf[...] = wd_stage[slot].astype(BF16)
        upcoming = next_s[expert]

        @pl.when(upcoming < N_EXPERTS)
        def _():
            for cp in weight_copies(upcoming, 1 - slot):
                cp.start()

    @pl.when(live)
    def _():
        x = xs_ref[...]
        n_chunks = D_EXPERT // EXPERT_CHUNK

        def gate_up(c):
            gc = pl.ds(c * EXPERT_CHUNK, EXPERT_CHUNK)
            uc = pl.ds(D_EXPERT + c * EXPERT_CHUNK, EXPERT_CHUNK)
            return (jnp.dot(x, wgu_bf[:, gc], preferred_element_type=F32) + bgu_ref[0, :, gc],
                    jnp.dot(x, wgu_bf[:, uc], preferred_element_type=F32) + bgu_ref[0, :, uc])

        g, u = gate_up(0)
        for c in range(n_chunks):
            nxt = gate_up(c + 1) if c + 1 < n_chunks else None
            g = jnp.minimum(g, SWIGLU_LIMIT)
            u = jnp.clip(u, -SWIGLU_LIMIT, SWIGLU_LIMIT)
            act_scr[:, pl.ds(c * EXPERT_CHUNK, EXPERT_CHUNK)] = (
                (u + 1.0) * (g * jax.nn.sigmoid(SWIGLU_ALPHA * g))).astype(BF16)
            if nxt is not None:
                g, u = nxt
        y = jnp.dot(act_scr[...], wd_bf[...], preferred_element_type=F32) + bd_ref[0]
        ys_ref[...] = y.astype(BF16)

    @pl.when(jnp.logical_not(live))
    def _():
        ys_ref[...] = jnp.zeros_like(ys_ref)


def _experts(tile_expert, n_used, stage_slot, next_expert, xs, wgu, bgu, wd, bd):
    n_rows = xs.shape[0]
    tm = EXPERT_TILE
    live = lambda i, te, nu: jnp.minimum(i, nu[0] - 1)
    grid_spec = pltpu.PrefetchScalarGridSpec(
        num_scalar_prefetch=4,
        grid=(n_rows // tm,),
        in_specs=[
            pl.BlockSpec((tm, D_MODEL), lambda i, te, nu, *_: (live(i, te, nu), 0)),
            pl.BlockSpec(memory_space=pl.ANY),
            pl.BlockSpec((1, 1, 2 * D_EXPERT), lambda i, te, nu, *_: (te[live(i, te, nu)], 0, 0)),
            pl.BlockSpec(memory_space=pl.ANY),
            pl.BlockSpec((1, 1, D_MODEL), lambda i, te, nu, *_: (te[live(i, te, nu)], 0, 0)),
        ],
        out_specs=pl.BlockSpec((tm, D_MODEL), lambda i, te, nu, *_: (i, 0)),
        scratch_shapes=[
            pltpu.VMEM((2, D_MODEL, 2 * D_EXPERT), F32),
            pltpu.VMEM((2, D_EXPERT, D_MODEL), F32),
            pltpu.VMEM((D_MODEL, 2 * D_EXPERT), BF16),
            pltpu.VMEM((D_EXPERT, D_MODEL), BF16),
            pltpu.VMEM((tm, D_EXPERT), BF16),
            pltpu.SemaphoreType.DMA((2, 2)),
        ],
    )
    return pl.pallas_call(
        _expert_kernel,
        grid_spec=grid_spec,
        out_shape=jax.ShapeDtypeStruct((n_rows, D_MODEL), BF16),
        compiler_params=pltpu.CompilerParams(
            dimension_semantics=("arbitrary",), vmem_limit_bytes=VMEM_LIMIT),
        name="experts",
    )(tile_expert, n_used, stage_slot, next_expert, xs, wgu, bgu, wd, bd)


def _combine_kernel(loff_s, pcnt_s, gstart_s, x1_ref, post_ref, gatet_ref, p_ref, gple_ref, wg_ref, wp_ref,
                    gfin_ref, ys_ref, out_ref, loc_scr, sem):
    i = pl.program_id(0)
    tt = TOKEN_TILE

    @pl.when(i == 0)
    def _():
        loc_scr[...] = jnp.zeros_like(loc_scr)
        _start_all(_chunk_copies(loff_s, pcnt_s, gstart_s, i, loc_scr, ys_ref, sem, False))

    _wait_all(_chunk_copies(loff_s, pcnt_s, gstart_s, i, loc_scr, ys_ref, sem, False))

    @pl.when(i + 1 < pl.num_programs(0))
    def _():
        _start_all(_chunk_copies(loff_s, pcnt_s, gstart_s, i + 1, loc_scr, ys_ref, sem, False))

    slot = i % 2

    post = post_ref[...]
    gatet = gatet_ref[...]
    moe = jnp.zeros((tt, D_MODEL), F32)
    for jb in range(LOCAL_ROWS // GATHER_BLOCK):
        jio = lax.broadcasted_iota(I32, (tt, GATHER_BLOCK), 1) + jb * GATHER_BLOCK
        w = jnp.zeros((tt, GATHER_BLOCK), F32)
        for k in range(TOP_K):
            w = jnp.where(jio == post[:, k:k + 1], gatet[:, k:k + 1], w)
        y = loc_scr[slot, pl.ds(jb * GATHER_BLOCK, GATHER_BLOCK), :]
        moe = moe + jnp.dot(w.astype(BF16), y, preferred_element_type=F32)

    x2 = x1_ref[...] + moe
    gate = jax.nn.sigmoid(jnp.dot(_rms(x2, gple_ref[...]).astype(BF16), wg_ref[...], preferred_element_type=F32))
    emb = jnp.dot(p_ref[...].astype(BF16), wp_ref[...], preferred_element_type=F32)
    out_ref[...] = _rms(x2 + emb * gate, gfin_ref[...])


def _combine(loff, pcnt, gstart, x1, post, gatet, p2d, gple, wg, wp, gfin, ys):
    t = x1.shape[0]
    tt = TOKEN_TILE
    row = lambda i, *_: (i, 0)
    const2 = lambda i, *_: (0, 0)
    grid_spec = pltpu.PrefetchScalarGridSpec(
        num_scalar_prefetch=3,
        grid=(t // tt,),
        in_specs=[
            pl.BlockSpec((tt, D_MODEL), row),
            pl.BlockSpec((tt, 8), row),
            pl.BlockSpec((tt, 8), row),
            pl.BlockSpec((tt, PLE_DIM), row),
            pl.BlockSpec((1, D_MODEL), const2),
            pl.BlockSpec((D_MODEL, D_MODEL), const2),
            pl.BlockSpec((PLE_DIM, D_MODEL), const2),
            pl.BlockSpec((1, D_MODEL), const2),
            pl.BlockSpec(memory_space=pl.ANY),
        ],
        out_specs=pl.BlockSpec((tt, D_MODEL), row),
        scratch_shapes=[pltpu.VMEM((2, LOCAL_ROWS, D_MODEL), BF16), pltpu.SemaphoreType.DMA((2,))],
    )
    return pl.pallas_call(
        _combine_kernel,
        grid_spec=grid_spec,
        out_shape=jax.ShapeDtypeStruct((t, D_MODEL), F32),
        compiler_params=pltpu.CompilerParams(
            dimension_semantics=("arbitrary",), vmem_limit_bytes=VMEM_LIMIT),
        name="combine",
    )(loff, pcnt, gstart, x1, post, gatet, p2d, gple, wg, wp, gfin, ys)


def _routing_tables(pc16):
    pcnt = pc16 * ROW_ALIGN
    loff = jnp.cumsum(pcnt, axis=1) - pcnt
    seg = jnp.sum(pcnt, axis=0)
    segpad = (seg + EXPERT_TILE - 1) // EXPERT_TILE * EXPERT_TILE
    seg_end = jnp.cumsum(segpad)
    ebase = seg_end - segpad
    gstart = ebase[None, :] + jnp.cumsum(pcnt, axis=0) - pcnt
    tails = jnp.stack([ebase + seg, segpad - seg, jnp.broadcast_to(seg_end[-1] // EXPERT_TILE, seg.shape)])
    return pcnt, loff, gstart, tails, seg_end


def kernel(x, p, g_mix, w_in, w_pool, pool_scale, rel_bias, w_out, g_ffn, w_router, b_router, w_gate_up,
           b_gate_up, w_down, b_down, g_ple, w_ple_gate, w_ple_proj, g_final):
    b, s, d = x.shape
    t = b * s
    x2d = x.reshape(t, d)
    for layer in range(w_in.shape[0]):
        u, *qkv_by_dil = _inproj(x2d, g_mix[layer][None], w_in[layer].astype(BF16))
        o_list, lse_list = [], []
        for (window, dil), qkv_d, (rq, nq) in zip(DILATED_BRANCHES, qkv_by_dil, ATTN_STEP):
            o_d, lse_d = _attn_branch(qkv_d, _bias_tables(rel_bias, window, dil), b, s, dil, rq, nq)
            o_list.append(o_d)
            lse_list.append(lse_d)
        brt = jnp.broadcast_to(b_router[layer][:, None], (N_EXPERTS, 128))
        x1, h2, pos, gates, pc = _mixout(
            x2d, u, o_list, lse_list, w_pool[layer].astype(BF16), pool_scale[layer][None],
            w_out[layer].astype(BF16), g_ffn[layer][None], w_router[layer].T, brt, s)

        pc16 = pc[:, :, 0].astype(I32)
        pcnt, loff, gstart, tails, seg_end = _routing_tables(pc16)
        n_rows = t * TOP_K + N_EXPERTS * (t // TOKEN_TILE) * (ROW_ALIGN - 1) + N_EXPERTS * (EXPERT_TILE - 1)
        n_rows = (n_rows + EXPERT_TILE - 1) // EXPERT_TILE * EXPERT_TILE
        n_tiles = n_rows // EXPERT_TILE
        tile_start = jnp.arange(n_tiles, dtype=I32) * EXPERT_TILE
        tile_expert = jnp.minimum(jnp.sum(seg_end[None, :] <= tile_start[:, None], axis=1), N_EXPERTS - 1).astype(I32)
        n_used = (seg_end[-1] // EXPERT_TILE).astype(I32)[None]
        xs = _dispatch(loff, pcnt, gstart, tails, h2, pos, n_rows)
        has_rows = jnp.diff(seg_end, prepend=0) > 0
        experts = jnp.arange(N_EXPERTS, dtype=I32)
        stage_slot = ((jnp.cumsum(has_rows) - 1) % 2).astype(I32)
        next_expert = jnp.min(jnp.where(has_rows[None, :] & (experts[None, :] > experts[:, None]),
                                        experts[None, :], N_EXPERTS), axis=1).astype(I32)
        ys = _experts(tile_expert, n_used, stage_slot, next_expert, xs, w_gate_up[layer],
                      b_gate_up[layer][:, None, :], w_down[layer], b_down[layer][:, None, :])

        post = pos.transpose(0, 2, 1).reshape(t, 8)
        gatet = gates.transpose(0, 2, 1).reshape(t, 8)
        assert layer == w_in.shape[0] - 1, "single-layer pipeline: the final norm is fused into combine"
        x2d = _combine(loff, pcnt, gstart, x1, post, gatet, p[layer].reshape(t, PLE_DIM), g_ple[layer][None],
                       w_ple_gate[layer].astype(BF16), w_ple_proj[layer].astype(BF16), g_final[None], ys)
    return x2d.reshape(b, s, d)
```

```python
import functools
import math

import jax
import jax.numpy as jnp
from jax import lax
from jax.experimental import pallas as pl
from jax.experimental.pallas import tpu as pltpu

F32 = jnp.float32
BF16 = jnp.bfloat16
I32 = jnp.int32

D_MODEL = 1024
POOL_WIDTH = 512
POOL_WINDOWS = (2, 4, 8, 16)
POOL_GROUP_DIM = 128
ATTN_WIDTH = 512
QKV_WIDTH = 3 * ATTN_WIDTH
HEAD_DIM = 64
N_HEADS = 8
DILATED_BRANCHES = ((128, 1), (512, 4), (2048, 16))
ATTN_BLOCK = 128
N_REL_BUCKETS = 32
REL_MAX_EXACT = 16
REL_MAX_DISTANCE = 2048
N_EXPERTS = 32
TOP_K = 4
D_EXPERT = 1024
SWIGLU_LIMIT = 7.0
SWIGLU_ALPHA = 1.702
PLE_DIM = 256
NORM_EPS = 1e-6
NEG_INF = -1e30
LOG2_E = math.log2(math.e)
QUERY_SCALE = LOG2_E / math.sqrt(HEAD_DIM)

LANES = 128
POOL_HALO = 16
TOKEN_TILE = 512
ROW_ALIGN = 16
EXPERT_TILE = 512
EXPERT_CHUNK = 256
LOCAL_ROWS = TOKEN_TILE * TOP_K + N_EXPERTS * ROW_ALIGN
SORT_BLOCK = 256
GATHER_BLOCK = 512
ATTN_STEP = ((1, 8), (1, 8), (4, 2))
VMEM_LIMIT = 56 * 1024 * 1024


def _rms(x, g):
    return x * lax.rsqrt(jnp.mean(x * x, axis=-1, keepdims=True) + NORM_EPS) * g


def _inproj_kernel(x_ref, g_ref, w_ref, u_ref, nat_ref, d4_ref, d16_ref, z_scr):
    h = _rms(x_ref[...], g_ref[...]).astype(BF16)
    part = ATTN_WIDTH
    slabs = part // LANES

    def project(i):
        return jnp.dot(h, w_ref[:, pl.ds(i * part, part)], preferred_element_type=F32)

    def regroup(z, i):
        col0 = (i - 1) * part
        nat_ref[:, pl.ds(col0, part)] = z.astype(BF16)
        for c in range(slabs):
            z_scr[(i - 1) * slabs + c] = z[:, c * LANES:(c + 1) * LANES]
        for dil, ref in ((4, d4_ref), (16, d16_ref)):
            rows = TOKEN_TILE // dil
            for r in range(dil):
                for c in range(slabs):
                    ref[:, pl.ds(r * QKV_WIDTH + col0 + c * LANES, LANES)] = (
                        z_scr[(i - 1) * slabs + c, pl.ds(r, rows, stride=dil), :].astype(BF16))

    assert POOL_WIDTH == part and QKV_WIDTH == 3 * part
    z_prev = project(0)
    for i in range(1, 4):
        z_next = project(i)
        if i == 1:
            u_ref[...] = z_prev
        else:
            regroup(z_prev, i - 1)
        z_prev = z_next
    regroup(z_prev, 3)


def _inproj(x2d, g, w_bf16):
    t = x2d.shape[0]
    in_w = w_bf16.shape[1]
    tt = TOKEN_TILE
    return pl.pallas_call(
        _inproj_kernel,
        grid=(t // tt,),
        in_specs=[
            pl.BlockSpec((tt, D_MODEL), lambda i: (i, 0)),
            pl.BlockSpec((1, D_MODEL), lambda i: (0, 0)),
            pl.BlockSpec((D_MODEL, in_w), lambda i: (0, 0)),
        ],
        out_specs=[
            pl.BlockSpec((tt, POOL_WIDTH), lambda i: (i, 0)),
            pl.BlockSpec((tt, QKV_WIDTH), lambda i: (i, 0)),
            pl.BlockSpec((tt // 4, 4 * QKV_WIDTH), lambda i: (i, 0)),
            pl.BlockSpec((tt // 16, 16 * QKV_WIDTH), lambda i: (i, 0)),
        ],
        out_shape=[
            jax.ShapeDtypeStruct((t, POOL_WIDTH), F32),
            jax.ShapeDtypeStruct((t, QKV_WIDTH), BF16),
            jax.ShapeDtypeStruct((t // 4, 4 * QKV_WIDTH), BF16),
            jax.ShapeDtypeStruct((t // 16, 16 * QKV_WIDTH), BF16),
        ],
        scratch_shapes=[pltpu.VMEM((QKV_WIDTH // LANES, tt, LANES), F32)],
        compiler_params=pltpu.CompilerParams(
            dimension_semantics=("parallel",), vmem_limit_bytes=VMEM_LIMIT),
        name="inproj",
    )(x2d, g, w_bf16)


def _attn_kernel(main_ref, prev_ref, tab_ref, o_ref, lse_ref, k_scr, vt_scr, *, rq, nq):
    blk = ATTN_BLOCK
    first_group = pl.program_id(2) == 0
    nt = (((1,), (1,)), ((), ()))

    for ri in range(rq):
        base = ri * QKV_WIDTH
        k_scr[0:blk, :] = prev_ref[:, pl.ds(base + ATTN_WIDTH, ATTN_WIDTH)]
        k_scr[blk:, :] = main_ref[:, pl.ds(base + ATTN_WIDTH, ATTN_WIDTH)]
        vt_scr[:, 0:blk] = prev_ref[:, pl.ds(base + 2 * ATTN_WIDTH, ATTN_WIDTH)].astype(F32).T.astype(BF16)
        for j in range(nq):
            vt_scr[:, pl.ds((j + 1) * blk, blk)] = (
                main_ref[pl.ds(j * blk, blk), pl.ds(base + 2 * ATTN_WIDTH, ATTN_WIDTH)].astype(F32).T.astype(BF16))
        for j in range(nq):
            rows = pl.ds(j * blk, blk)
            keys = pl.ds(j * blk, 2 * blk)
            scores = []
            for h in range(N_HEADS):
                q = main_ref[rows, pl.ds(base + h * HEAD_DIM, HEAD_DIM)]
                k2 = k_scr[keys, pl.ds(h * HEAD_DIM, HEAD_DIM)]
                scores.append(lax.dot_general(k2, q, nt, preferred_element_type=F32))
            probs, inv_l, lse_parts = [], [], []
            for h in range(N_HEADS):
                s = scores[h] + tab_ref[h]
                if j == 0:
                    pen = jnp.where(first_group, NEG_INF, 0.0).astype(F32)
                    s = jnp.concatenate([s[:blk] + pen, s[blk:]], axis=0)
                m = jnp.max(s, axis=0, keepdims=True)
                p = jnp.exp2(s - m)
                l = jnp.sum(p, axis=0, keepdims=True)
                probs.append(p.astype(BF16))
                inv_l.append(1.0 / l)
                lse_parts.append(m * math.log(2.0) + jnp.log(l))
            o_parts = []
            for h in range(N_HEADS):
                vt2 = vt_scr[pl.ds(h * HEAD_DIM, HEAD_DIM), keys]
                o_parts.append(jnp.dot(vt2, probs[h], preferred_element_type=F32) * inv_l[h])
            o_t = jnp.concatenate(o_parts, axis=0)
            o_ref[rows, pl.ds(ri * ATTN_WIDTH, ATTN_WIDTH)] = o_t.T.astype(BF16)
            lse_t = jnp.concatenate(lse_parts + [jnp.zeros((LANES - N_HEADS, blk), F32)], axis=0)
            lse_ref[rows, pl.ds(ri * LANES, LANES)] = lse_t.T


def _attn_branch(qkv_d, tab, batch, seq, dil, rq, nq):
    sub = seq // dil
    nb = sub // ATTN_BLOCK
    groups = nb // nq
    return pl.pallas_call(
        functools.partial(_attn_kernel, rq=rq, nq=nq),
        grid=(batch, dil // rq, groups),
        in_specs=[
            pl.BlockSpec((nq * ATTN_BLOCK, rq * QKV_WIDTH), lambda b, r, g: (b * groups + g, r)),
            pl.BlockSpec((ATTN_BLOCK, rq * QKV_WIDTH), lambda b, r, g: (b * nb + jnp.maximum(g * nq - 1, 0), r)),
            pl.BlockSpec((N_HEADS, 2 * ATTN_BLOCK, ATTN_BLOCK), lambda b, r, g: (0, 0, 0)),
        ],
        out_specs=[
            pl.BlockSpec((nq * ATTN_BLOCK, rq * ATTN_WIDTH), lambda b, r, g: (b * groups + g, r)),
            pl.BlockSpec((nq * ATTN_BLOCK, rq * LANES), lambda b, r, g: (b * groups + g, r)),
        ],
        out_shape=[
            jax.ShapeDtypeStruct((batch * sub, dil * ATTN_WIDTH), BF16),
            jax.ShapeDtypeStruct((batch * sub, dil * LANES), F32),
        ],
        scratch_shapes=[
            pltpu.VMEM(((nq + 1) * ATTN_BLOCK, ATTN_WIDTH), BF16),
            pltpu.VMEM((ATTN_WIDTH, (nq + 1) * ATTN_BLOCK), BF16),
        ],
        compiler_params=pltpu.CompilerParams(
            dimension_semantics=("parallel", "parallel", "parallel"), vmem_limit_bytes=VMEM_LIMIT),
        name="attn",
    )(qkv_d, qkv_d, tab)


def _t5_bucket(dist):
    n = jnp.maximum(dist, 1).astype(F32)
    large = REL_MAX_EXACT + (jnp.log(n / REL_MAX_EXACT) / math.log(REL_MAX_DISTANCE / REL_MAX_EXACT)
                             * (N_REL_BUCKETS - REL_MAX_EXACT)).astype(I32)
    large = jnp.minimum(large, N_REL_BUCKETS - 1)
    return jnp.where(dist < REL_MAX_EXACT, dist, large)


def _shifted_rows(w, n):
    lead = w.shape[:-1]
    width = w.shape[-1]
    flat = jnp.tile(w, (1,) * len(lead) + (n + 1,))[..., :n * (width + 1)]
    return flat.reshape(lead + (n, width + 1))[..., :n]


def _bias_tables(rel_bias, window, dil):
    blk = ATTN_BLOCK
    assert window // dil == blk
    f = rel_bias[_t5_bucket(jnp.arange(blk + 1) * dil)].T.astype(F32)
    neg = jnp.full((N_HEADS, blk - 1), NEG_INF, F32)
    neg1 = jnp.full((N_HEADS, 1), NEG_INF, F32)
    w_prev = jnp.concatenate([neg, f[:, :0:-1], neg1], axis=1)
    prev = _shifted_rows(w_prev, blk)[:, :, ::-1]
    w_cur = jnp.concatenate([neg, f[:, :blk], neg1], axis=1)
    cur = _shifted_rows(w_cur, blk)[:, ::-1, :]
    return jnp.concatenate([prev, cur], axis=1) * LOG2_E


def _mixout_kernel(x_ref, u_ref, uh_ref, o1_ref, o4_ref, o16_ref, l1_ref, l4_ref, l16_ref,
                   wpool_ref, pscale_ref, wout_ref, gffn_ref, wrt_ref, brt_ref, spread_ref,
                   x1_ref, h2_ref, pos_ref, gate_ref, pc_ref,
                   ext_scr, mix_scr, o_scr, l_scr, *, tiles_per_seq):
    i = pl.program_id(0)
    tt = TOKEN_TILE
    seq_tile = i % tiles_per_seq

    halo = uh_ref[...]
    ext_scr[0:POOL_HALO, :] = jnp.where(seq_tile == 0, jnp.zeros_like(halo), halo)
    ext_scr[POOL_HALO:, :] = u_ref[...]
    tpos = seq_tile * tt + lax.broadcasted_iota(I32, (tt, 1), 0)
    for gi, w in enumerate(POOL_WINDOWS):
        cols = pl.ds(gi * POOL_GROUP_DIM, POOL_GROUP_DIM)
        tok = ext_scr[pl.ds(POOL_HALO, tt), cols]
        acc = tok
        for j in range(1, w):
            acc = acc + ext_scr[pl.ds(POOL_HALO - j, tt), cols]
        cnt = jnp.minimum(tpos + 1, w).astype(F32)
        pooled = (acc / cnt - tok).astype(BF16)
        mixed = jnp.dot(pooled, wpool_ref[gi], preferred_element_type=F32) * pscale_ref[:, cols]
        mix_scr[:, cols] = mixed.astype(BF16)

    o_slabs = ATTN_WIDTH // LANES
    for bi, (dil, o_ref, l_ref) in enumerate(((4, o4_ref, l4_ref), (16, o16_ref, l16_ref))):
        rows = tt // dil
        for r in range(dil):
            l_scr[bi, pl.ds(r, rows, stride=dil), :] = l_ref[:, pl.ds(r * LANES, LANES)]
            for c in range(o_slabs):
                o_scr[bi * o_slabs + c, pl.ds(r, rows, stride=dil), :] = (
                    o_ref[:, pl.ds(r * ATTN_WIDTH + c * LANES, LANES)].astype(F32))
    l1, l4, l16 = l1_ref[...], l_scr[0], l_scr[1]
    lm = jnp.maximum(jnp.maximum(l1, l4), l16)
    e1, e4, e16 = jnp.exp(l1 - lm), jnp.exp(l4 - lm), jnp.exp(l16 - lm)
    inv = 1.0 / (e1 + e4 + e16)

    def per_feature(w):
        hi = w.astype(BF16)
        lo = (w - hi.astype(F32)).astype(BF16)
        return (jnp.dot(hi, spread_ref[...], preferred_element_type=F32)
                + jnp.dot(lo, spread_ref[...], preferred_element_type=F32))

    w1, w4, w16 = per_feature(e1 * inv), per_feature(e4 * inv), per_feature(e16 * inv)
    for c in range(o_slabs):
        sl = pl.ds(c * LANES, LANES)
        cs = slice(c * LANES, (c + 1) * LANES)
        o = w1[:, cs] * o1_ref[:, sl].astype(F32) + w4[:, cs] * o_scr[c] + w16[:, cs] * o_scr[o_slabs + c]
        mix_scr[:, pl.ds(POOL_WIDTH + c * LANES, LANES)] = o.astype(BF16)

    x1 = x_ref[...] + jnp.dot(mix_scr[...], wout_ref[...], preferred_element_type=F32)
    x1_ref[...] = x1

    h2 = _rms(x1, gffn_ref[...])
    h2_ref[...] = h2.astype(BF16)
    nt = (((1,), (1,)), ((), ()))
    wr = wrt_ref[...]
    wr_hi = wr.astype(BF16)
    wr_lo = (wr - wr_hi.astype(F32)).astype(BF16)
    h2_hi = h2.astype(BF16)
    h2_lo = (h2 - h2_hi.astype(F32)).astype(BF16)
    logits = (lax.dot_general(wr_hi, h2_hi, nt, preferred_element_type=F32)
              + lax.dot_general(wr_hi, h2_lo, nt, preferred_element_type=F32)
              + lax.dot_general(wr_lo, h2_hi, nt, preferred_element_type=F32))
    logits = logits + brt_ref[:, 0:1]
    eio = lax.broadcasted_iota(I32, (N_EXPERTS, tt), 0)
    sel, val = [], []
    for _ in range(TOP_K):
        m = jnp.max(logits, axis=0, keepdims=True)
        idx = jnp.min(jnp.where(logits == m, eio, N_EXPERTS), axis=0, keepdims=True)
        sel.append(idx)
        val.append(m)
        logits = jnp.where(eio == idx, -jnp.inf, logits)
    ex = [jnp.exp(v - val[0]) for v in val]
    den = ex[0] + ex[1] + ex[2] + ex[3]
    gates = [e / den for e in ex]

    hot = [(eio == s) for s in sel]
    onehot = (hot[0] | hot[1] | hot[2] | hot[3]).astype(F32)
    ti = lax.broadcasted_iota(I32, (tt, tt), 0)
    tj = lax.broadcasted_iota(I32, (tt, tt), 1)
    before = (ti < tj).astype(BF16)
    rank = jnp.dot(onehot.astype(BF16), before, preferred_element_type=F32)
    cnt = jnp.sum(onehot, axis=1, keepdims=True)
    pc = jnp.floor((cnt + (ROW_ALIGN - 1)) * (1.0 / ROW_ALIGN))
    pcb = jnp.broadcast_to(pc, (N_EXPERTS, 128))
    pc_ref[0] = pcb
    pcm = jnp.concatenate([pcb, jnp.zeros((128 - N_EXPERTS, 128), F32)], axis=0).astype(BF16)
    li = lax.broadcasted_iota(I32, (N_EXPERTS, 128), 0)
    lj = lax.broadcasted_iota(I32, (N_EXPERTS, 128), 1)
    lower = (lj < li).astype(BF16)
    off = jnp.dot(lower, pcm, preferred_element_type=F32)[:, 0:1] * float(ROW_ALIGN)
    where_to = off + rank
    zero_i = jnp.zeros((1, tt), I32)
    zero_f = jnp.zeros((1, tt), F32)
    pos_rows = [jnp.sum(jnp.where(hk, where_to, 0.0), axis=0, keepdims=True).astype(I32) for hk in hot]
    pos_ref[0] = jnp.concatenate(pos_rows + [zero_i] * (8 - TOP_K), axis=0)
    gate_ref[0] = jnp.concatenate(gates + [zero_f] * (8 - TOP_K), axis=0)


def _mixout(x2d, u, o_list, lse_list, wpool, pscale, wout, gffn, wrt, brt, seq):
    t = x2d.shape[0]
    tt = TOKEN_TILE
    n_tt = t // tt
    halo_blocks = tt // POOL_HALO
    row = lambda i: (i, 0)
    const2 = lambda i: (0, 0)
    dils = [dil for _, dil in DILATED_BRANCHES]
    spread = (jnp.arange(LANES)[:, None] == jnp.arange(ATTN_WIDTH)[None, :] // HEAD_DIM).astype(BF16)
    return pl.pallas_call(
        functools.partial(_mixout_kernel, tiles_per_seq=seq // tt),
        grid=(n_tt,),
        in_specs=[
            pl.BlockSpec((tt, D_MODEL), row),
            pl.BlockSpec((tt, POOL_WIDTH), row),
            pl.BlockSpec((POOL_HALO, POOL_WIDTH), lambda i: (jnp.maximum(i * halo_blocks - 1, 0), 0)),
            *[pl.BlockSpec((tt // dil, dil * ATTN_WIDTH), row) for dil in dils],
            *[pl.BlockSpec((tt // dil, dil * LANES), row) for dil in dils],
            pl.BlockSpec((len(POOL_WINDOWS), POOL_GROUP_DIM, POOL_GROUP_DIM), lambda i: (0, 0, 0)),
            pl.BlockSpec((1, POOL_WIDTH), const2),
            pl.BlockSpec((D_MODEL, D_MODEL), const2),
            pl.BlockSpec((1, D_MODEL), const2),
            pl.BlockSpec((N_EXPERTS, D_MODEL), const2),
            pl.BlockSpec((N_EXPERTS, 128), const2),
            pl.BlockSpec((LANES, ATTN_WIDTH), const2),
        ],
        out_specs=[
            pl.BlockSpec((tt, D_MODEL), row),
            pl.BlockSpec((tt, D_MODEL), row),
            pl.BlockSpec((1, 8, tt), lambda i: (i, 0, 0)),
            pl.BlockSpec((1, 8, tt), lambda i: (i, 0, 0)),
            pl.BlockSpec((1, N_EXPERTS, 128), lambda i: (i, 0, 0)),
        ],
        out_shape=[
            jax.ShapeDtypeStruct((t, D_MODEL), F32),
            jax.ShapeDtypeStruct((t, D_MODEL), BF16),
            jax.ShapeDtypeStruct((n_tt, 8, tt), I32),
            jax.ShapeDtypeStruct((n_tt, 8, tt), F32),
            jax.ShapeDtypeStruct((n_tt, N_EXPERTS, 128), F32),
        ],
        scratch_shapes=[
            pltpu.VMEM((tt + POOL_HALO, POOL_WIDTH), F32),
            pltpu.VMEM((tt, D_MODEL), BF16),
            pltpu.VMEM((2 * ATTN_WIDTH // LANES, tt, LANES), F32),
            pltpu.VMEM((2, tt, LANES), F32),
        ],
        compiler_params=pltpu.CompilerParams(
            dimension_semantics=("parallel",), vmem_limit_bytes=VMEM_LIMIT),
        name="mixout",
    )(x2d, u, u, *o_list, *lse_list, wpool, pscale, wout, gffn, wrt, brt, spread)


def _chunk_copies(loff_s, pcnt_s, gstart_s, tile, local_ref, global_ref, sems, to_global):
    slot = tile % 2
    out = []
    for e in range(N_EXPERTS):
        n = pl.multiple_of(pcnt_s[tile, e], ROW_ALIGN)
        lo = pl.multiple_of(loff_s[tile, e], ROW_ALIGN)
        go = pl.multiple_of(gstart_s[tile, e], ROW_ALIGN)
        loc = local_ref.at[slot, pl.ds(lo, n)]
        glo = global_ref.at[pl.ds(go, n)]
        src, dst = (loc, glo) if to_global else (glo, loc)
        out.append((n > 0, pltpu.make_async_copy(src, dst, sems.at[slot])))
    return out


def _start_all(copies):
    for cond, cp in copies:
        pl.when(cond)(cp.start)


def _wait_chunks(loff_s, pcnt_s, tile, local_ref, global_ref, sems, to_global):
    slot = tile % 2
    rows = pl.multiple_of(loff_s[tile, N_EXPERTS - 1] + pcnt_s[tile, N_EXPERTS - 1], ROW_ALIGN)
    loc = local_ref.at[slot, pl.ds(0, rows)]
    glo = global_ref.at[pl.ds(0, rows)]
    src, dst = (loc, glo) if to_global else (glo, loc)
    pltpu.make_async_copy(src, dst, sems.at[slot]).wait()


def _dispatch_kernel(loff_s, pcnt_s, gstart_s, tail_s, h2_ref, pos_ref, xs_ref, loc_scr, zero_scr, sem, tail_sem,
                     spare_sem):
    i = pl.program_id(0)
    tt = TOKEN_TILE

    @pl.when(i == 0)
    def _():
        zero_scr[...] = jnp.zeros_like(zero_scr)
        tails = []
        for e in range(N_EXPERTS):
            n = pl.multiple_of(tail_s[1, e], ROW_ALIGN)
            start = pl.multiple_of(tail_s[0, e], ROW_ALIGN)
            tails.append((n > 0, pltpu.make_async_copy(zero_scr.at[pl.ds(0, n)], xs_ref.at[pl.ds(start, n)], tail_sem)))
        for cond, cp in tails:
            pl.when(cond)(cp.start)
        for cond, cp in tails:
            pl.when(cond)(cp.wait)

    def spare_tile(j):
        return pltpu.make_async_copy(
            zero_scr, xs_ref.at[pl.ds(pl.multiple_of(j * EXPERT_TILE, EXPERT_TILE), EXPERT_TILE)], spare_sem)

    n_tiles = xs_ref.shape[0] // EXPERT_TILE

    @pl.when(i == 0)
    def _():
        lax.fori_loop(tail_s[2, 0], n_tiles, lambda j, c: (spare_tile(j).start(), c)[1], 0)

    @pl.when(i == pl.num_programs(0) - 1)
    def _():
        lax.fori_loop(tail_s[2, 0], n_tiles, lambda j, c: (spare_tile(j).wait(), c)[1], 0)

    pos = pos_ref[0]
    h2 = h2_ref[...]
    slot = i % 2
    used_rows = loff_s[i, N_EXPERTS - 1] + pcnt_s[i, N_EXPERTS - 1]
    for jb in range(LOCAL_ROWS // SORT_BLOCK):
        def sort_block(jb=jb):
            jio = lax.broadcasted_iota(I32, (SORT_BLOCK, tt), 0) + jb * SORT_BLOCK
            hit = (jio == pos[0:1]) | (jio == pos[1:2]) | (jio == pos[2:3]) | (jio == pos[3:4])
            onehot = jnp.where(hit, 1.0, 0.0).astype(BF16)
            loc_scr[slot, pl.ds(jb * SORT_BLOCK, SORT_BLOCK), :] = jnp.dot(
                onehot, h2, preferred_element_type=F32).astype(BF16)

        if (jb + 1) * SORT_BLOCK <= TOKEN_TILE * TOP_K:
            sort_block()
        else:
            pl.when(jb * SORT_BLOCK < used_rows)(sort_block)

    @pl.when(i > 0)
    def _():
        _wait_chunks(loff_s, pcnt_s, i - 1, loc_scr, xs_ref, sem, True)

    _start_all(_chunk_copies(loff_s, pcnt_s, gstart_s, i, loc_scr, xs_ref, sem, True))

    @pl.when(i == pl.num_programs(0) - 1)
    def _():
        _wait_chunks(loff_s, pcnt_s, i, loc_scr, xs_ref, sem, True)


def _dispatch(loff, pcnt, gstart, tails, h2, pos, n_rows):
    t = h2.shape[0]
    tt = TOKEN_TILE
    grid_spec = pltpu.PrefetchScalarGridSpec(
        num_scalar_prefetch=4,
        grid=(t // tt,),
        in_specs=[
            pl.BlockSpec((tt, D_MODEL), lambda i, *_: (i, 0)),
            pl.BlockSpec((1, 8, tt), lambda i, *_: (i, 0, 0)),
        ],
        out_specs=pl.BlockSpec(memory_space=pl.ANY),
        scratch_shapes=[
            pltpu.VMEM((2, LOCAL_ROWS, D_MODEL), BF16),
            pltpu.VMEM((EXPERT_TILE, D_MODEL), BF16),
            pltpu.SemaphoreType.DMA((2,)),
            pltpu.SemaphoreType.DMA(()),
            pltpu.SemaphoreType.DMA(()),
        ],
    )
    return pl.pallas_call(
        _dispatch_kernel,
        grid_spec=grid_spec,
        out_shape=jax.ShapeDtypeStruct((n_rows, D_MODEL), BF16),
        compiler_params=pltpu.CompilerParams(
            dimension_semantics=("arbitrary",), vmem_limit_bytes=VMEM_LIMIT),
        name="dispatch",
    )(loff, pcnt, gstart, tails, h2, pos)


def _expert_kernel(te_s, nu_s, slot_s, next_s, xs_ref, wgu_hbm, bgu_ref, wd_hbm, bd_ref, ys_ref,
                   wgu_stage, wd_stage, wgu_bf, wd_bf, act_scr, wsem):
    i = pl.program_id(0)
    live = i < nu_s[0]
    expert = te_s[i]
    new_expert = (i == 0) | (expert != te_s[jnp.maximum(i - 1, 0)])

    def weight_copies(e, slot):
        return (pltpu.make_async_copy(wgu_hbm.at[e], wgu_stage.at[slot], wsem.at[0, slot]),
                pltpu.make_async_copy(wd_hbm.at[e], wd_stage.at[slot], wsem.at[1, slot]))

    @pl.when(live & new_expert)
    def _():
        slot = slot_s[expert]

        @pl.when(i == 0)
        def _():
            for cp in weight_copies(expert, slot):
                cp.start()

        for cp in weight_copies(expert, slot):
            cp.wait()
        wgu_bf[...] = wgu_stage[slot].astype(BF16)
        wd_bf[...] = wd_stage[slot].astype(BF16)
        upcoming = next_s[expert]

        @pl.when(upcoming < N_EXPERTS)
        def _():
            for cp in weight_copies(upcoming, 1 - slot):
                cp.start()

    @pl.when(live)
    def _():
        x = xs_ref[...]
        n_chunks = D_EXPERT // EXPERT_CHUNK

        def gate_up(c):
            gc = pl.ds(c * EXPERT_CHUNK, EXPERT_CHUNK)
            uc = pl.ds(D_EXPERT + c * EXPERT_CHUNK, EXPERT_CHUNK)
            return (jnp.dot(x, wgu_bf[:, gc], preferred_element_type=F32) + bgu_ref[0, :, gc],
                    jnp.dot(x, wgu_bf[:, uc], preferred_element_type=F32) + bgu_ref[0, :, uc])

        g, u = gate_up(0)
        for c in range(n_chunks):
            nxt = gate_up(c + 1) if c + 1 < n_chunks else None
            g = jnp.minimum(g, SWIGLU_LIMIT)
            u = jnp.clip(u, -SWIGLU_LIMIT, SWIGLU_LIMIT)
            act_scr[:, pl.ds(c * EXPERT_CHUNK, EXPERT_CHUNK)] = (
                (u + 1.0) * (g * jax.nn.sigmoid(SWIGLU_ALPHA * g))).astype(BF16)
            if nxt is not None:
                g, u = nxt
        y = jnp.dot(act_scr[...], wd_bf[...], preferred_element_type=F32) + bd_ref[0]
        ys_ref[...] = y.astype(BF16)

    @pl.when(jnp.logical_not(live))
    def _():
        ys_ref[...] = jnp.zeros_like(ys_ref)


def _experts(tile_expert, n_used, stage_slot, next_expert, xs, wgu, bgu, wd, bd):
    n_rows = xs.shape[0]
    tm = EXPERT_TILE
    live = lambda i, te, nu: jnp.minimum(i, nu[0] - 1)
    grid_spec = pltpu.PrefetchScalarGridSpec(
        num_scalar_prefetch=4,
        grid=(n_rows // tm,),
        in_specs=[
            pl.BlockSpec((tm, D_MODEL), lambda i, te, nu, *_: (live(i, te, nu), 0)),
            pl.BlockSpec(memory_space=pl.ANY),
            pl.BlockSpec((1, 1, 2 * D_EXPERT), lambda i, te, nu, *_: (te[live(i, te, nu)], 0, 0)),
            pl.BlockSpec(memory_space=pl.ANY),
            pl.BlockSpec((1, 1, D_MODEL), lambda i, te, nu, *_: (te[live(i, te, nu)], 0, 0)),
        ],
        out_specs=pl.BlockSpec((tm, D_MODEL), lambda i, te, nu, *_: (i, 0)),
        scratch_shapes=[
            pltpu.VMEM((2, D_MODEL, 2 * D_EXPERT), F32),
            pltpu.VMEM((2, D_EXPERT, D_MODEL), F32),
            pltpu.VMEM((D_MODEL, 2 * D_EXPERT), BF16),
            pltpu.VMEM((D_EXPERT, D_MODEL), BF16),
            pltpu.VMEM((tm, D_EXPERT), BF16),
            pltpu.SemaphoreType.DMA((2, 2)),
        ],
    )
    return pl.pallas_call(
        _expert_kernel,
        grid_spec=grid_spec,
        out_shape=jax.ShapeDtypeStruct((n_rows, D_MODEL), BF16),
        compiler_params=pltpu.CompilerParams(
            dimension_semantics=("arbitrary",), vmem_limit_bytes=VMEM_LIMIT),
        name="experts",
    )(tile_expert, n_used, stage_slot, next_expert, xs, wgu, bgu, wd, bd)


def _combine_kernel(loff_s, pcnt_s, gstart_s, x1_ref, post_ref, gatet_ref, p_ref, gple_ref, wg_ref, wp_ref,
                    gfin_ref, ys_ref, out_ref, loc_scr, sem):
    i = pl.program_id(0)
    tt = TOKEN_TILE

    @pl.when(i == 0)
    def _():
        loc_scr[...] = jnp.zeros_like(loc_scr)
        _start_all(_chunk_copies(loff_s, pcnt_s, gstart_s, i, loc_scr, ys_ref, sem, False))

    _wait_chunks(loff_s, pcnt_s, i, loc_scr, ys_ref, sem, False)

    @pl.when(i + 1 < pl.num_programs(0))
    def _():
        _start_all(_chunk_copies(loff_s, pcnt_s, gstart_s, i + 1, loc_scr, ys_ref, sem, False))

    slot = i % 2

    post = post_ref[...]
    gatet = gatet_ref[...]
    moe = jnp.zeros((tt, D_MODEL), F32)
    for jb in range(LOCAL_ROWS // GATHER_BLOCK):
        jio = lax.broadcasted_iota(I32, (tt, GATHER_BLOCK), 1) + jb * GATHER_BLOCK
        w = jnp.zeros((tt, GATHER_BLOCK), F32)
        for k in range(TOP_K):
            w = jnp.where(jio == post[:, k:k + 1], gatet[:, k:k + 1], w)
        y = loc_scr[slot, pl.ds(jb * GATHER_BLOCK, GATHER_BLOCK), :]
        moe = moe + jnp.dot(w.astype(BF16), y, preferred_element_type=F32)

    x2 = x1_ref[...] + moe
    gate = jax.nn.sigmoid(jnp.dot(_rms(x2, gple_ref[...]).astype(BF16), wg_ref[...], preferred_element_type=F32))
    emb = jnp.dot(p_ref[...].astype(BF16), wp_ref[...], preferred_element_type=F32)
    out_ref[...] = _rms(x2 + emb * gate, gfin_ref[...])


def _combine(loff, pcnt, gstart, x1, post, gatet, p2d, gple, wg, wp, gfin, ys):
    t = x1.shape[0]
    tt = TOKEN_TILE
    row = lambda i, *_: (i, 0)
    const2 = lambda i, *_: (0, 0)
    grid_spec = pltpu.PrefetchScalarGridSpec(
        num_scalar_prefetch=3,
        grid=(t // tt,),
        in_specs=[
            pl.BlockSpec((tt, D_MODEL), row),
            pl.BlockSpec((tt, 8), row),
            pl.BlockSpec((tt, 8), row),
            pl.BlockSpec((tt, PLE_DIM), row),
            pl.BlockSpec((1, D_MODEL), const2),
            pl.BlockSpec((D_MODEL, D_MODEL), const2),
            pl.BlockSpec((PLE_DIM, D_MODEL), const2),
            pl.BlockSpec((1, D_MODEL), const2),
            pl.BlockSpec(memory_space=pl.ANY),
        ],
        out_specs=pl.BlockSpec((tt, D_MODEL), row),
        scratch_shapes=[pltpu.VMEM((2, LOCAL_ROWS, D_MODEL), BF16), pltpu.SemaphoreType.DMA((2,))],
    )
    return pl.pallas_call(
        _combine_kernel,
        grid_spec=grid_spec,
        out_shape=jax.ShapeDtypeStruct((t, D_MODEL), F32),
        compiler_params=pltpu.CompilerParams(
            dimension_semantics=("arbitrary",), vmem_limit_bytes=VMEM_LIMIT),
        name="combine",
    )(loff, pcnt, gstart, x1, post, gatet, p2d, gple, wg, wp, gfin, ys)


def _routing_tables(pc16):
    pcnt = pc16 * ROW_ALIGN
    loff = jnp.cumsum(pcnt, axis=1) - pcnt
    seg = jnp.sum(pcnt, axis=0)
    segpad = (seg + EXPERT_TILE - 1) // EXPERT_TILE * EXPERT_TILE
    seg_end = jnp.cumsum(segpad)
    ebase = seg_end - segpad
    gstart = ebase[None, :] + jnp.cumsum(pcnt, axis=0) - pcnt
    tails = jnp.stack([ebase + seg, segpad - seg, jnp.broadcast_to(seg_end[-1] // EXPERT_TILE, seg.shape)])
    return pcnt, loff, gstart, tails, seg_end


def kernel(x, p, g_mix, w_in, w_pool, pool_scale, rel_bias, w_out, g_ffn, w_router, b_router, w_gate_up,
           b_gate_up, w_down, b_down, g_ple, w_ple_gate, w_ple_proj, g_final):
    b, s, d = x.shape
    t = b * s
    x2d = x.reshape(t, d)
    for layer in range(w_in.shape[0]):
        col = jnp.arange(w_in.shape[2])
        is_q = (col >= POOL_WIDTH) & (col < POOL_WIDTH + ATTN_WIDTH)
        w_in_scaled = (w_in[layer] * jnp.where(is_q, QUERY_SCALE, 1.0)).astype(BF16)
        u, *qkv_by_dil = _inproj(x2d, g_mix[layer][None], w_in_scaled)
        o_list, lse_list = [], []
        for (window, dil), qkv_d, (rq, nq) in zip(DILATED_BRANCHES, qkv_by_dil, ATTN_STEP):
            o_d, lse_d = _attn_branch(qkv_d, _bias_tables(rel_bias, window, dil), b, s, dil, rq, nq)
            o_list.append(o_d)
            lse_list.append(lse_d)
        brt = jnp.broadcast_to(b_router[layer][:, None], (N_EXPERTS, 128))
        x1, h2, pos, gates, pc = _mixout(
            x2d, u, o_list, lse_list, w_pool[layer].astype(BF16), pool_scale[layer][None],
            w_out[layer].astype(BF16), g_ffn[layer][None], w_router[layer].T, brt, s)

        pc16 = pc[:, :, 0].astype(I32)
        pcnt, loff, gstart, tails, seg_end = _routing_tables(pc16)
        n_rows = t * TOP_K + N_EXPERTS * (t // TOKEN_TILE) * (ROW_ALIGN - 1) + N_EXPERTS * (EXPERT_TILE - 1)
        n_rows = (n_rows + EXPERT_TILE - 1) // EXPERT_TILE * EXPERT_TILE
        n_tiles = n_rows // EXPERT_TILE
        tile_start = jnp.arange(n_tiles, dtype=I32) * EXPERT_TILE
        tile_expert = jnp.minimum(jnp.sum(seg_end[None, :] <= tile_start[:, None], axis=1), N_EXPERTS - 1).astype(I32)
        n_used = (seg_end[-1] // EXPERT_TILE).astype(I32)[None]
        xs = _dispatch(loff, pcnt, gstart, tails, h2, pos, n_rows)
        has_rows = jnp.diff(seg_end, prepend=0) > 0
        experts = jnp.arange(N_EXPERTS, dtype=I32)
        stage_slot = ((jnp.cumsum(has_rows) - 1) % 2).astype(I32)
        next_expert = jnp.min(jnp.where(has_rows[None, :] & (experts[None, :] > experts[:, None]),
                                        experts[None, :], N_EXPERTS), axis=1).astype(I32)
        ys = _experts(tile_expert, n_used, stage_slot, next_expert, xs, w_gate_up[layer],
                      b_gate_up[layer][:, None, :], w_down[layer], b_down[layer][:, None, :])

        post = pos.transpose(0, 2, 1).reshape(t, 8)
        gatet = gates.transpose(0, 2, 1).reshape(t, 8)
        assert layer == w_in.shape[0] - 1, "single-layer pipeline: the final norm is fused into combine"
        x2d = _combine(loff, pcnt, gstart, x1, post, gatet, p[layer].reshape(t, PLE_DIM), g_ple[layer][None],
                       w_ple_gate[layer].astype(BF16), w_ple_proj[layer].astype(BF16), g_final[None], ys)
    return x2d.reshape(b, s, d)
```

```python
import functools
import math

import jax
import jax.numpy as jnp
from jax import lax
from jax.experimental import pallas as pl
from jax.experimental.pallas import tpu as pltpu

F32 = jnp.float32
BF16 = jnp.bfloat16
I32 = jnp.int32

D_MODEL = 1024
POOL_WIDTH = 512
POOL_WINDOWS = (2, 4, 8, 16)
POOL_GROUP_DIM = 128
ATTN_WIDTH = 512
QKV_WIDTH = 3 * ATTN_WIDTH
HEAD_DIM = 64
N_HEADS = 8
DILATED_BRANCHES = ((128, 1), (512, 4), (2048, 16))
ATTN_BLOCK = 128
N_REL_BUCKETS = 32
REL_MAX_EXACT = 16
REL_MAX_DISTANCE = 2048
N_EXPERTS = 32
TOP_K = 4
D_EXPERT = 1024
SWIGLU_LIMIT = 7.0
SWIGLU_ALPHA = 1.702
PLE_DIM = 256
NORM_EPS = 1e-6
NEG_INF = -1e30
LOG2_E = math.log2(math.e)
QUERY_SCALE = LOG2_E / math.sqrt(HEAD_DIM)

LANES = 128
POOL_HALO = 16
TOKEN_TILE = 512
ROW_ALIGN = 16
EXPERT_TILE = 512
EXPERT_CHUNK = 256
LOCAL_ROWS = TOKEN_TILE * TOP_K + N_EXPERTS * ROW_ALIGN
SORT_BLOCK = 256
GATHER_BLOCK = 512
ATTN_STEP = ((1, 8), (1, 8), (4, 2))
VMEM_LIMIT = 56 * 1024 * 1024


def _rms(x, g):
    return x * lax.rsqrt(jnp.mean(x * x, axis=-1, keepdims=True) + NORM_EPS) * g


def _inproj_kernel(x_ref, g_ref, w_ref, u_ref, nat_ref, d4_ref, d16_ref, z_scr, g_scr):
    h = _rms(x_ref[...], g_ref[...]).astype(BF16)
    part = ATTN_WIDTH
    slabs = part // LANES

    def project(i):
        return jnp.dot(h, w_ref[:, pl.ds(i * part, part)], preferred_element_type=F32)

    def regroup(z, i):
        col0 = (i - 1) * part
        nat_ref[:, pl.ds(col0, part)] = z.astype(BF16)
        for c in range(slabs):
            slab = (i - 1) * slabs + c
            cols = col0 + c * LANES
            z_scr[slab] = z[:, c * LANES:(c + 1) * LANES]
            for rl in range(4):
                grp = z_scr[slab, pl.ds(rl, TOKEN_TILE // 4, stride=4), :]
                d4_ref[:, pl.ds(rl * QKV_WIDTH + cols, LANES)] = grp.astype(BF16)
                g_scr[rl * n_slabs + slab] = grp
            for rl in range(4):
                for rh in range(4):
                    d16_ref[:, pl.ds((4 * rh + rl) * QKV_WIDTH + cols, LANES)] = (
                        g_scr[rl * n_slabs + slab, pl.ds(rh, TOKEN_TILE // 16, stride=4), :].astype(BF16))

    assert POOL_WIDTH == part and QKV_WIDTH == 3 * part and [d for _, d in DILATED_BRANCHES] == [1, 4, 16]
    n_slabs = QKV_WIDTH // LANES
    z_prev = project(0)
    for i in range(1, 4):
        z_next = project(i)
        if i == 1:
            u_ref[...] = z_prev
        else:
            regroup(z_prev, i - 1)
        z_prev = z_next
    regroup(z_prev, 3)


def _inproj(x2d, g, w_bf16):
    t = x2d.shape[0]
    in_w = w_bf16.shape[1]
    tt = TOKEN_TILE
    return pl.pallas_call(
        _inproj_kernel,
        grid=(t // tt,),
        in_specs=[
            pl.BlockSpec((tt, D_MODEL), lambda i: (i, 0)),
            pl.BlockSpec((1, D_MODEL), lambda i: (0, 0)),
            pl.BlockSpec((D_MODEL, in_w), lambda i: (0, 0)),
        ],
        out_specs=[
            pl.BlockSpec((tt, POOL_WIDTH), lambda i: (i, 0)),
            pl.BlockSpec((tt, QKV_WIDTH), lambda i: (i, 0)),
            pl.BlockSpec((tt // 4, 4 * QKV_WIDTH), lambda i: (i, 0)),
            pl.BlockSpec((tt // 16, 16 * QKV_WIDTH), lambda i: (i, 0)),
        ],
        out_shape=[
            jax.ShapeDtypeStruct((t, POOL_WIDTH), F32),
            jax.ShapeDtypeStruct((t, QKV_WIDTH), BF16),
            jax.ShapeDtypeStruct((t // 4, 4 * QKV_WIDTH), BF16),
            jax.ShapeDtypeStruct((t // 16, 16 * QKV_WIDTH), BF16),
        ],
        scratch_shapes=[pltpu.VMEM((QKV_WIDTH // LANES, tt, LANES), F32),
                        pltpu.VMEM((4 * QKV_WIDTH // LANES, tt // 4, LANES), F32)],
        compiler_params=pltpu.CompilerParams(
            dimension_semantics=("parallel",), vmem_limit_bytes=VMEM_LIMIT),
        name="inproj",
    )(x2d, g, w_bf16)


def _attn_kernel(main_ref, prev_ref, tab_ref, o_ref, lse_ref, k_scr, vt_scr, *, rq, nq):
    blk = ATTN_BLOCK
    first_group = pl.program_id(2) == 0
    nt = (((1,), (1,)), ((), ()))

    for ri in range(rq):
        base = ri * QKV_WIDTH
        k_scr[0:blk, :] = prev_ref[:, pl.ds(base + ATTN_WIDTH, ATTN_WIDTH)]
        k_scr[blk:, :] = main_ref[:, pl.ds(base + ATTN_WIDTH, ATTN_WIDTH)]
        vt_scr[:, 0:blk] = prev_ref[:, pl.ds(base + 2 * ATTN_WIDTH, ATTN_WIDTH)].astype(F32).T.astype(BF16)
        for j in range(nq):
            vt_scr[:, pl.ds((j + 1) * blk, blk)] = (
                main_ref[pl.ds(j * blk, blk), pl.ds(base + 2 * ATTN_WIDTH, ATTN_WIDTH)].astype(F32).T.astype(BF16))
        for j in range(nq):
            rows = pl.ds(j * blk, blk)
            keys = pl.ds(j * blk, 2 * blk)
            scores = []
            for h in range(N_HEADS):
                q = main_ref[rows, pl.ds(base + h * HEAD_DIM, HEAD_DIM)]
                k2 = k_scr[keys, pl.ds(h * HEAD_DIM, HEAD_DIM)]
                scores.append(lax.dot_general(k2, q, nt, preferred_element_type=F32))
            probs, inv_l, lse_parts = [], [], []
            for h in range(N_HEADS):
                s = scores[h] + tab_ref[h]
                if j == 0:
                    pen = jnp.where(first_group, NEG_INF, 0.0).astype(F32)
                    s = jnp.concatenate([s[:blk] + pen, s[blk:]], axis=0)
                m = jnp.max(s, axis=0, keepdims=True)
                p = jnp.exp2(s - m)
                l = jnp.sum(p, axis=0, keepdims=True)
                probs.append(p.astype(BF16))
                inv_l.append(1.0 / l)
                lse_parts.append(m * math.log(2.0) + jnp.log(l))
            o_parts = []
            for h in range(N_HEADS):
                vt2 = vt_scr[pl.ds(h * HEAD_DIM, HEAD_DIM), keys]
                o_parts.append(jnp.dot(vt2, probs[h], preferred_element_type=F32) * inv_l[h])
            o_t = jnp.concatenate(o_parts, axis=0)
            o_ref[rows, pl.ds(ri * ATTN_WIDTH, ATTN_WIDTH)] = o_t.T.astype(BF16)
            lse_t = jnp.concatenate(lse_parts + [jnp.zeros((LANES - N_HEADS, blk), F32)], axis=0)
            lse_ref[rows, pl.ds(ri * LANES, LANES)] = lse_t.T


def _attn_branch(qkv_d, tab, batch, seq, dil, rq, nq):
    sub = seq // dil
    nb = sub // ATTN_BLOCK
    groups = nb // nq
    return pl.pallas_call(
        functools.partial(_attn_kernel, rq=rq, nq=nq),
        grid=(batch, dil // rq, groups),
        in_specs=[
            pl.BlockSpec((nq * ATTN_BLOCK, rq * QKV_WIDTH), lambda b, r, g: (b * groups + g, r)),
            pl.BlockSpec((ATTN_BLOCK, rq * QKV_WIDTH), lambda b, r, g: (b * nb + jnp.maximum(g * nq - 1, 0), r)),
            pl.BlockSpec((N_HEADS, 2 * ATTN_BLOCK, ATTN_BLOCK), lambda b, r, g: (0, 0, 0)),
        ],
        out_specs=[
            pl.BlockSpec((nq * ATTN_BLOCK, rq * ATTN_WIDTH), lambda b, r, g: (b * groups + g, r)),
            pl.BlockSpec((nq * ATTN_BLOCK, rq * LANES), lambda b, r, g: (b * groups + g, r)),
        ],
        out_shape=[
            jax.ShapeDtypeStruct((batch * sub, dil * ATTN_WIDTH), BF16),
            jax.ShapeDtypeStruct((batch * sub, dil * LANES), F32),
        ],
        scratch_shapes=[
            pltpu.VMEM(((nq + 1) * ATTN_BLOCK, ATTN_WIDTH), BF16),
            pltpu.VMEM((ATTN_WIDTH, (nq + 1) * ATTN_BLOCK), BF16),
        ],
        compiler_params=pltpu.CompilerParams(
            dimension_semantics=("parallel", "parallel", "parallel"), vmem_limit_bytes=VMEM_LIMIT),
        name="attn",
    )(qkv_d, qkv_d, tab)


def _t5_bucket(dist):
    n = jnp.maximum(dist, 1).astype(F32)
    large = REL_MAX_EXACT + (jnp.log(n / REL_MAX_EXACT) / math.log(REL_MAX_DISTANCE / REL_MAX_EXACT)
                             * (N_REL_BUCKETS - REL_MAX_EXACT)).astype(I32)
    large = jnp.minimum(large, N_REL_BUCKETS - 1)
    return jnp.where(dist < REL_MAX_EXACT, dist, large)


def _shifted_rows(w, n):
    lead = w.shape[:-1]
    width = w.shape[-1]
    flat = jnp.tile(w, (1,) * len(lead) + (n + 1,))[..., :n * (width + 1)]
    return flat.reshape(lead + (n, width + 1))[..., :n]


def _bias_tables(rel_bias, window, dil):
    blk = ATTN_BLOCK
    assert window // dil == blk
    f = rel_bias[_t5_bucket(jnp.arange(blk + 1) * dil)].T.astype(F32)
    neg = jnp.full((N_HEADS, blk - 1), NEG_INF, F32)
    neg1 = jnp.full((N_HEADS, 1), NEG_INF, F32)
    w_prev = jnp.concatenate([neg, f[:, :0:-1], neg1], axis=1)
    prev = _shifted_rows(w_prev, blk)[:, :, ::-1]
    w_cur = jnp.concatenate([neg, f[:, :blk], neg1], axis=1)
    cur = _shifted_rows(w_cur, blk)[:, ::-1, :]
    return jnp.concatenate([prev, cur], axis=1) * LOG2_E


def _mixout_kernel(x_ref, u_ref, uh_ref, o1_ref, o4_ref, o16_ref, l1_ref, l4_ref, l16_ref,
                   wpool_ref, pscale_ref, wout_ref, gffn_ref, wrt_ref, brt_ref, spread_ref,
                   x1_ref, h2_ref, pos_ref, gate_ref, pc_ref,
                   ext_scr, mix_scr, o_scr, l_scr, *, tiles_per_seq):
    i = pl.program_id(0)
    tt = TOKEN_TILE
    seq_tile = i % tiles_per_seq

    halo = uh_ref[...]
    ext_scr[0:POOL_HALO, :] = jnp.where(seq_tile == 0, jnp.zeros_like(halo), halo)
    ext_scr[POOL_HALO:, :] = u_ref[...]
    tpos = seq_tile * tt + lax.broadcasted_iota(I32, (tt, 1), 0)
    for gi, w in enumerate(POOL_WINDOWS):
        cols = pl.ds(gi * POOL_GROUP_DIM, POOL_GROUP_DIM)
        tok = ext_scr[pl.ds(POOL_HALO, tt), cols]
        acc = tok
        for j in range(1, w):
            acc = acc + ext_scr[pl.ds(POOL_HALO - j, tt), cols]
        cnt = jnp.minimum(tpos + 1, w).astype(F32)
        pooled = (acc / cnt - tok).astype(BF16)
        mixed = jnp.dot(pooled, wpool_ref[gi], preferred_element_type=F32) * pscale_ref[:, cols]
        mix_scr[:, cols] = mixed.astype(BF16)

    o_slabs = ATTN_WIDTH // LANES
    for bi, (dil, o_ref, l_ref) in enumerate(((4, o4_ref, l4_ref), (16, o16_ref, l16_ref))):
        rows = tt // dil
        for r in range(dil):
            l_scr[bi, pl.ds(r, rows, stride=dil), :] = l_ref[:, pl.ds(r * LANES, LANES)]
            for c in range(o_slabs):
                o_scr[bi * o_slabs + c, pl.ds(r, rows, stride=dil), :] = (
                    o_ref[:, pl.ds(r * ATTN_WIDTH + c * LANES, LANES)].astype(F32))
    l1, l4, l16 = l1_ref[...], l_scr[0], l_scr[1]
    lm = jnp.maximum(jnp.maximum(l1, l4), l16)
    e1, e4, e16 = jnp.exp(l1 - lm), jnp.exp(l4 - lm), jnp.exp(l16 - lm)
    inv = 1.0 / (e1 + e4 + e16)

    def per_feature(w):
        hi = w.astype(BF16)
        lo = (w - hi.astype(F32)).astype(BF16)
        return (jnp.dot(hi, spread_ref[...], preferred_element_type=F32)
                + jnp.dot(lo, spread_ref[...], preferred_element_type=F32))

    w1, w4, w16 = per_feature(e1 * inv), per_feature(e4 * inv), per_feature(e16 * inv)
    for c in range(o_slabs):
        sl = pl.ds(c * LANES, LANES)
        cs = slice(c * LANES, (c + 1) * LANES)
        o = w1[:, cs] * o1_ref[:, sl].astype(F32) + w4[:, cs] * o_scr[c] + w16[:, cs] * o_scr[o_slabs + c]
        mix_scr[:, pl.ds(POOL_WIDTH + c * LANES, LANES)] = o.astype(BF16)

    x1 = x_ref[...] + jnp.dot(mix_scr[...], wout_ref[...], preferred_element_type=F32)
    x1_ref[...] = x1

    h2 = _rms(x1, gffn_ref[...])
    h2_ref[...] = h2.astype(BF16)
    nt = (((1,), (1,)), ((), ()))
    wr = wrt_ref[...]
    wr_hi = wr.astype(BF16)
    wr_lo = (wr - wr_hi.astype(F32)).astype(BF16)
    h2_hi = h2.astype(BF16)
    h2_lo = (h2 - h2_hi.astype(F32)).astype(BF16)
    logits = (lax.dot_general(wr_hi, h2_hi, nt, preferred_element_type=F32)
              + lax.dot_general(wr_hi, h2_lo, nt, preferred_element_type=F32)
              + lax.dot_general(wr_lo, h2_hi, nt, preferred_element_type=F32))
    logits = logits + brt_ref[:, 0:1]
    eio = lax.broadcasted_iota(I32, (N_EXPERTS, tt), 0)
    sel, val = [], []
    for _ in range(TOP_K):
        m = jnp.max(logits, axis=0, keepdims=True)
        idx = jnp.min(jnp.where(logits == m, eio, N_EXPERTS), axis=0, keepdims=True)
        sel.append(idx)
        val.append(m)
        logits = jnp.where(eio == idx, -jnp.inf, logits)
    ex = [jnp.exp(v - val[0]) for v in val]
    den = ex[0] + ex[1] + ex[2] + ex[3]
    gates = [e / den for e in ex]

    hot = [(eio == s) for s in sel]
    onehot = (hot[0] | hot[1] | hot[2] | hot[3]).astype(F32)
    ti = lax.broadcasted_iota(I32, (tt, tt), 0)
    tj = lax.broadcasted_iota(I32, (tt, tt), 1)
    before = (ti < tj).astype(BF16)
    rank = jnp.dot(onehot.astype(BF16), before, preferred_element_type=F32)
    cnt = jnp.sum(onehot, axis=1, keepdims=True)
    pc = jnp.maximum(jnp.floor((cnt + (ROW_ALIGN - 1)) * (1.0 / ROW_ALIGN)), 1.0)
    pcb = jnp.broadcast_to(pc, (N_EXPERTS, 128))
    pc_ref[0] = pcb
    pcm = jnp.concatenate([pcb, jnp.zeros((128 - N_EXPERTS, 128), F32)], axis=0).astype(BF16)
    li = lax.broadcasted_iota(I32, (N_EXPERTS, 128), 0)
    lj = lax.broadcasted_iota(I32, (N_EXPERTS, 128), 1)
    lower = (lj < li).astype(BF16)
    off = jnp.dot(lower, pcm, preferred_element_type=F32)[:, 0:1] * float(ROW_ALIGN)
    where_to = off + rank
    zero_i = jnp.zeros((1, tt), I32)
    zero_f = jnp.zeros((1, tt), F32)
    pos_rows = [jnp.sum(jnp.where(hk, where_to, 0.0), axis=0, keepdims=True).astype(I32) for hk in hot]
    pos_ref[0] = jnp.concatenate(pos_rows + [zero_i] * (8 - TOP_K), axis=0)
    gate_ref[0] = jnp.concatenate(gates + [zero_f] * (8 - TOP_K), axis=0)


def _mixout(x2d, u, o_list, lse_list, wpool, pscale, wout, gffn, wrt, brt, seq):
    t = x2d.shape[0]
    tt = TOKEN_TILE
    n_tt = t // tt
    halo_blocks = tt // POOL_HALO
    row = lambda i: (i, 0)
    const2 = lambda i: (0, 0)
    dils = [dil for _, dil in DILATED_BRANCHES]
    spread = (jnp.arange(LANES)[:, None] == jnp.arange(ATTN_WIDTH)[None, :] // HEAD_DIM).astype(BF16)
    return pl.pallas_call(
        functools.partial(_mixout_kernel, tiles_per_seq=seq // tt),
        grid=(n_tt,),
        in_specs=[
            pl.BlockSpec((tt, D_MODEL), row),
            pl.BlockSpec((tt, POOL_WIDTH), row),
            pl.BlockSpec((POOL_HALO, POOL_WIDTH), lambda i: (jnp.maximum(i * halo_blocks - 1, 0), 0)),
            *[pl.BlockSpec((tt // dil, dil * ATTN_WIDTH), row) for dil in dils],
            *[pl.BlockSpec((tt // dil, dil * LANES), row) for dil in dils],
            pl.BlockSpec((len(POOL_WINDOWS), POOL_GROUP_DIM, POOL_GROUP_DIM), lambda i: (0, 0, 0)),
            pl.BlockSpec((1, POOL_WIDTH), const2),
            pl.BlockSpec((D_MODEL, D_MODEL), const2),
            pl.BlockSpec((1, D_MODEL), const2),
            pl.BlockSpec((N_EXPERTS, D_MODEL), const2),
            pl.BlockSpec((N_EXPERTS, 128), const2),
            pl.BlockSpec((LANES, ATTN_WIDTH), const2),
        ],
        out_specs=[
            pl.BlockSpec((tt, D_MODEL), row),
            pl.BlockSpec((tt, D_MODEL), row),
            pl.BlockSpec((1, 8, tt), lambda i: (i, 0, 0)),
            pl.BlockSpec((1, 8, tt), lambda i: (i, 0, 0)),
            pl.BlockSpec((1, N_EXPERTS, 128), lambda i: (i, 0, 0)),
        ],
        out_shape=[
            jax.ShapeDtypeStruct((t, D_MODEL), F32),
            jax.ShapeDtypeStruct((t, D_MODEL), BF16),
            jax.ShapeDtypeStruct((n_tt, 8, tt), I32),
            jax.ShapeDtypeStruct((n_tt, 8, tt), F32),
            jax.ShapeDtypeStruct((n_tt, N_EXPERTS, 128), F32),
        ],
        scratch_shapes=[
            pltpu.VMEM((tt + POOL_HALO, POOL_WIDTH), F32),
            pltpu.VMEM((tt, D_MODEL), BF16),
            pltpu.VMEM((2 * ATTN_WIDTH // LANES, tt, LANES), F32),
            pltpu.VMEM((2, tt, LANES), F32),
        ],
        compiler_params=pltpu.CompilerParams(
            dimension_semantics=("parallel",), vmem_limit_bytes=VMEM_LIMIT),
        name="mixout",
    )(x2d, u, u, *o_list, *lse_list, wpool, pscale, wout, gffn, wrt, brt, spread)


def _chunk_copies(loff_s, pcnt_s, gstart_s, tile, local_ref, global_ref, sems, to_global):
    slot = tile % 2
    out = []
    for e in range(N_EXPERTS):
        n = pl.multiple_of(pcnt_s[tile, e], ROW_ALIGN)
        lo = pl.multiple_of(loff_s[tile, e], ROW_ALIGN)
        go = pl.multiple_of(gstart_s[tile, e], ROW_ALIGN)
        loc = local_ref.at[slot, pl.ds(lo, n)]
        glo = global_ref.at[pl.ds(go, n)]
        src, dst = (loc, glo) if to_global else (glo, loc)
        out.append(pltpu.make_async_copy(src, dst, sems.at[slot]))
    return out


def _start_all(copies):
    for cp in copies:
        cp.start()


def _wait_chunks(loff_s, pcnt_s, tile, local_ref, global_ref, sems, to_global):
    slot = tile % 2
    rows = pl.multiple_of(loff_s[tile, N_EXPERTS - 1] + pcnt_s[tile, N_EXPERTS - 1], ROW_ALIGN)
    loc = local_ref.at[slot, pl.ds(0, rows)]
    glo = global_ref.at[pl.ds(0, rows)]
    src, dst = (loc, glo) if to_global else (glo, loc)
    pltpu.make_async_copy(src, dst, sems.at[slot]).wait()


def _dispatch_kernel(loff_s, pcnt_s, gstart_s, tail_s, h2_ref, pos_ref, xs_ref, loc_scr, zero_scr, sem, tail_sem,
                     spare_sem):
    i = pl.program_id(0)
    tt = TOKEN_TILE

    @pl.when(i == 0)
    def _():
        zero_scr[...] = jnp.zeros_like(zero_scr)
        tails = []
        for e in range(N_EXPERTS):
            n = pl.multiple_of(tail_s[1, e], ROW_ALIGN)
            start = pl.multiple_of(tail_s[0, e], ROW_ALIGN)
            tails.append((n > 0, pltpu.make_async_copy(zero_scr.at[pl.ds(0, n)], xs_ref.at[pl.ds(start, n)], tail_sem)))
        for cond, cp in tails:
            pl.when(cond)(cp.start)
        for cond, cp in tails:
            pl.when(cond)(cp.wait)

    def spare_tile(j):
        return pltpu.make_async_copy(
            zero_scr, xs_ref.at[pl.ds(pl.multiple_of(j * EXPERT_TILE, EXPERT_TILE), EXPERT_TILE)], spare_sem)

    n_tiles = xs_ref.shape[0] // EXPERT_TILE

    @pl.when(i == 0)
    def _():
        lax.fori_loop(tail_s[2, 0], n_tiles, lambda j, c: (spare_tile(j).start(), c)[1], 0)

    @pl.when(i == pl.num_programs(0) - 1)
    def _():
        lax.fori_loop(tail_s[2, 0], n_tiles, lambda j, c: (spare_tile(j).wait(), c)[1], 0)

    pos = pos_ref[0]
    h2 = h2_ref[...]
    slot = i % 2
    used_rows = loff_s[i, N_EXPERTS - 1] + pcnt_s[i, N_EXPERTS - 1]
    def one_hot(jb):
        jio = lax.broadcasted_iota(I32, (SORT_BLOCK, tt), 0) + jb * SORT_BLOCK
        hit = (jio == pos[0:1]) | (jio == pos[1:2]) | (jio == pos[2:3]) | (jio == pos[3:4])
        return jnp.where(hit, 1.0, 0.0).astype(BF16)

    def sort_block(jb, onehot):
        loc_scr[slot, pl.ds(jb * SORT_BLOCK, SORT_BLOCK), :] = jnp.dot(
            onehot, h2, preferred_element_type=F32).astype(BF16)

    n_sure = TOKEN_TILE * TOP_K // SORT_BLOCK
    onehot = one_hot(0)
    for jb in range(n_sure):
        nxt = one_hot(jb + 1) if jb + 1 < n_sure else None
        sort_block(jb, onehot)
        onehot = nxt
    for jb in range(n_sure, LOCAL_ROWS // SORT_BLOCK):
        pl.when(jb * SORT_BLOCK < used_rows)(lambda jb=jb: sort_block(jb, one_hot(jb)))

    @pl.when(i > 0)
    def _():
        _wait_chunks(loff_s, pcnt_s, i - 1, loc_scr, xs_ref, sem, True)

    _start_all(_chunk_copies(loff_s, pcnt_s, gstart_s, i, loc_scr, xs_ref, sem, True))

    @pl.when(i == pl.num_programs(0) - 1)
    def _():
        _wait_chunks(loff_s, pcnt_s, i, loc_scr, xs_ref, sem, True)


def _dispatch(loff, pcnt, gstart, tails, h2, pos, n_rows):
    t = h2.shape[0]
    tt = TOKEN_TILE
    grid_spec = pltpu.PrefetchScalarGridSpec(
        num_scalar_prefetch=4,
        grid=(t // tt,),
        in_specs=[
            pl.BlockSpec((tt, D_MODEL), lambda i, *_: (i, 0)),
            pl.BlockSpec((1, 8, tt), lambda i, *_: (i, 0, 0)),
        ],
        out_specs=pl.BlockSpec(memory_space=pl.ANY),
        scratch_shapes=[
            pltpu.VMEM((2, LOCAL_ROWS, D_MODEL), BF16),
            pltpu.VMEM((EXPERT_TILE, D_MODEL), BF16),
            pltpu.SemaphoreType.DMA((2,)),
            pltpu.SemaphoreType.DMA(()),
            pltpu.SemaphoreType.DMA(()),
        ],
    )
    return pl.pallas_call(
        _dispatch_kernel,
        grid_spec=grid_spec,
        out_shape=jax.ShapeDtypeStruct((n_rows, D_MODEL), BF16),
        compiler_params=pltpu.CompilerParams(
            dimension_semantics=("arbitrary",), vmem_limit_bytes=VMEM_LIMIT),
        name="dispatch",
    )(loff, pcnt, gstart, tails, h2, pos)


def _expert_kernel(te_s, nu_s, slot_s, next_s, xs_ref, wgu_hbm, bgu_ref, wd_hbm, bd_ref, ys_ref,
                   wgu_stage, wd_stage, wgu_bf, wd_bf, act_scr, wsem):
    i = pl.program_id(0)
    live = i < nu_s[0]
    expert = te_s[i]
    new_expert = (i == 0) | (expert != te_s[jnp.maximum(i - 1, 0)])

    def weight_copies(e, slot):
        return (pltpu.make_async_copy(wgu_hbm.at[e], wgu_stage.at[slot], wsem.at[0, slot]),
                pltpu.make_async_copy(wd_hbm.at[e], wd_stage.at[slot], wsem.at[1, slot]))

    @pl.when(live & new_expert)
    def _():
        slot = slot_s[expert]

        @pl.when(i == 0)
        def _():
            for cp in weight_copies(expert, slot):
                cp.start()

        for cp in weight_copies(expert, slot):
            cp.wait()
        wgu_bf[...] = wgu_stage[slot].astype(BF16)
        wd_bf[...] = wd_stage[slot].astype(BF16)
        upcoming = next_s[expert]

        @pl.when(upcoming < N_EXPERTS)
        def _():
            for cp in weight_copies(upcoming, 1 - slot):
                cp.start()

    @pl.when(live)
    def _():
        x = xs_ref[...]
        n_chunks = D_EXPERT // EXPERT_CHUNK

        def gate_up(c):
            gc = pl.ds(c * EXPERT_CHUNK, EXPERT_CHUNK)
            uc = pl.ds(D_EXPERT + c * EXPERT_CHUNK, EXPERT_CHUNK)
            return (jnp.dot(x, wgu_bf[:, gc], preferred_element_type=F32) + bgu_ref[0, :, gc],
                    jnp.dot(x, wgu_bf[:, uc], preferred_element_type=F32) + bgu_ref[0, :, uc])

        g, u = gate_up(0)
        for c in range(n_chunks):
            nxt = gate_up(c + 1) if c + 1 < n_chunks else None
            g = jnp.minimum(g, SWIGLU_LIMIT)
            u = jnp.clip(u, -SWIGLU_LIMIT, SWIGLU_LIMIT)
            act_scr[:, pl.ds(c * EXPERT_CHUNK, EXPERT_CHUNK)] = (
                (u + 1.0) * (g * jax.nn.sigmoid(SWIGLU_ALPHA * g))).astype(BF16)
            if nxt is not None:
                g, u = nxt
        y = jnp.dot(act_scr[...], wd_bf[...], preferred_element_type=F32) + bd_ref[0]
        ys_ref[...] = y.astype(BF16)

    @pl.when(jnp.logical_not(live))
    def _():
        ys_ref[...] = jnp.zeros_like(ys_ref)


def _experts(tile_expert, n_used, stage_slot, next_expert, xs, wgu, bgu, wd, bd):
    n_rows = xs.shape[0]
    tm = EXPERT_TILE
    live = lambda i, te, nu: jnp.minimum(i, nu[0] - 1)
    grid_spec = pltpu.PrefetchScalarGridSpec(
        num_scalar_prefetch=4,
        grid=(n_rows // tm,),
        in_specs=[
            pl.BlockSpec((tm, D_MODEL), lambda i, te, nu, *_: (live(i, te, nu), 0)),
            pl.BlockSpec(memory_space=pl.ANY),
            pl.BlockSpec((1, 1, 2 * D_EXPERT), lambda i, te, nu, *_: (te[live(i, te, nu)], 0, 0)),
            pl.BlockSpec(memory_space=pl.ANY),
            pl.BlockSpec((1, 1, D_MODEL), lambda i, te, nu, *_: (te[live(i, te, nu)], 0, 0)),
        ],
        out_specs=pl.BlockSpec((tm, D_MODEL), lambda i, te, nu, *_: (i, 0)),
        scratch_shapes=[
            pltpu.VMEM((2, D_MODEL, 2 * D_EXPERT), F32),
            pltpu.VMEM((2, D_EXPERT, D_MODEL), F32),
            pltpu.VMEM((D_MODEL, 2 * D_EXPERT), BF16),
            pltpu.VMEM((D_EXPERT, D_MODEL), BF16),
            pltpu.VMEM((tm, D_EXPERT), BF16),
            pltpu.SemaphoreType.DMA((2, 2)),
        ],
    )
    return pl.pallas_call(
        _expert_kernel,
        grid_spec=grid_spec,
        out_shape=jax.ShapeDtypeStruct((n_rows, D_MODEL), BF16),
        compiler_params=pltpu.CompilerParams(
            dimension_semantics=("arbitrary",), vmem_limit_bytes=VMEM_LIMIT),
        name="experts",
    )(tile_expert, n_used, stage_slot, next_expert, xs, wgu, bgu, wd, bd)


def _combine_kernel(loff_s, pcnt_s, gstart_s, x1_ref, post_ref, gatet_ref, p_ref, gple_ref, wg_ref, wp_ref,
                    gfin_ref, ys_ref, out_ref, loc_scr, sem):
    i = pl.program_id(0)
    tt = TOKEN_TILE

    @pl.when(i == 0)
    def _():
        loc_scr[...] = jnp.zeros_like(loc_scr)
        _start_all(_chunk_copies(loff_s, pcnt_s, gstart_s, i, loc_scr, ys_ref, sem, False))

    _wait_chunks(loff_s, pcnt_s, i, loc_scr, ys_ref, sem, False)

    @pl.when(i + 1 < pl.num_programs(0))
    def _():
        _start_all(_chunk_copies(loff_s, pcnt_s, gstart_s, i + 1, loc_scr, ys_ref, sem, False))

    slot = i % 2

    post = post_ref[...]
    gatet = gatet_ref[...]

    def gate_block(jb):
        jio = lax.broadcasted_iota(I32, (tt, GATHER_BLOCK), 1) + jb * GATHER_BLOCK
        w = jnp.zeros((tt, GATHER_BLOCK), F32)
        for k in range(TOP_K):
            w = jnp.where(jio == post[:, k:k + 1], gatet[:, k:k + 1], w)
        return w.astype(BF16)

    n_blocks = LOCAL_ROWS // GATHER_BLOCK
    moe = jnp.zeros((tt, D_MODEL), F32)
    w = gate_block(0)
    for jb in range(n_blocks):
        nxt = gate_block(jb + 1) if jb + 1 < n_blocks else None
        y = loc_scr[slot, pl.ds(jb * GATHER_BLOCK, GATHER_BLOCK), :]
        moe = moe + jnp.dot(w, y, preferred_element_type=F32)
        w = nxt

    x2 = x1_ref[...] + moe
    gate = jax.nn.sigmoid(jnp.dot(_rms(x2, gple_ref[...]).astype(BF16), wg_ref[...], preferred_element_type=F32))
    emb = jnp.dot(p_ref[...].astype(BF16), wp_ref[...], preferred_element_type=F32)
    out_ref[...] = _rms(x2 + emb * gate, gfin_ref[...])


def _combine(loff, pcnt, gstart, x1, post, gatet, p2d, gple, wg, wp, gfin, ys):
    t = x1.shape[0]
    tt = TOKEN_TILE
    row = lambda i, *_: (i, 0)
    const2 = lambda i, *_: (0, 0)
    grid_spec = pltpu.PrefetchScalarGridSpec(
        num_scalar_prefetch=3,
        grid=(t // tt,),
        in_specs=[
            pl.BlockSpec((tt, D_MODEL), row),
            pl.BlockSpec((tt, 8), row),
            pl.BlockSpec((tt, 8), row),
            pl.BlockSpec((tt, PLE_DIM), row),
            pl.BlockSpec((1, D_MODEL), const2),
            pl.BlockSpec((D_MODEL, D_MODEL), const2),
            pl.BlockSpec((PLE_DIM, D_MODEL), const2),
            pl.BlockSpec((1, D_MODEL), const2),
            pl.BlockSpec(memory_space=pl.ANY),
        ],
        out_specs=pl.BlockSpec((tt, D_MODEL), row),
        scratch_shapes=[pltpu.VMEM((2, LOCAL_ROWS, D_MODEL), BF16), pltpu.SemaphoreType.DMA((2,))],
    )
    return pl.pallas_call(
        _combine_kernel,
        grid_spec=grid_spec,
        out_shape=jax.ShapeDtypeStruct((t, D_MODEL), F32),
        compiler_params=pltpu.CompilerParams(
            dimension_semantics=("arbitrary",), vmem_limit_bytes=VMEM_LIMIT),
        name="combine",
    )(loff, pcnt, gstart, x1, post, gatet, p2d, gple, wg, wp, gfin, ys)


def _routing_tables(pc16):
    pcnt = pc16 * ROW_ALIGN
    loff = jnp.cumsum(pcnt, axis=1) - pcnt
    seg = jnp.sum(pcnt, axis=0)
    segpad = (seg + EXPERT_TILE - 1) // EXPERT_TILE * EXPERT_TILE
    seg_end = jnp.cumsum(segpad)
    ebase = seg_end - segpad
    gstart = ebase[None, :] + jnp.cumsum(pcnt, axis=0) - pcnt
    tails = jnp.stack([ebase + seg, segpad - seg, jnp.broadcast_to(seg_end[-1] // EXPERT_TILE, seg.shape)])
    return pcnt, loff, gstart, tails, seg_end


def kernel(x, p, g_mix, w_in, w_pool, pool_scale, rel_bias, w_out, g_ffn, w_router, b_router, w_gate_up,
           b_gate_up, w_down, b_down, g_ple, w_ple_gate, w_ple_proj, g_final):
    b, s, d = x.shape
    t = b * s
    x2d = x.reshape(t, d)
    for layer in range(w_in.shape[0]):
        col = jnp.arange(w_in.shape[2])
        is_q = (col >= POOL_WIDTH) & (col < POOL_WIDTH + ATTN_WIDTH)
        w_in_scaled = (w_in[layer] * jnp.where(is_q, QUERY_SCALE, 1.0)).astype(BF16)
        u, *qkv_by_dil = _inproj(x2d, g_mix[layer][None], w_in_scaled)
        o_list, lse_list = [], []
        for (window, dil), qkv_d, (rq, nq) in zip(DILATED_BRANCHES, qkv_by_dil, ATTN_STEP):
            o_d, lse_d = _attn_branch(qkv_d, _bias_tables(rel_bias, window, dil), b, s, dil, rq, nq)
            o_list.append(o_d)
            lse_list.append(lse_d)
        brt = jnp.broadcast_to(b_router[layer][:, None], (N_EXPERTS, 128))
        x1, h2, pos, gates, pc = _mixout(
            x2d, u, o_list, lse_list, w_pool[layer].astype(BF16), pool_scale[layer][None],
            w_out[layer].astype(BF16), g_ffn[layer][None], w_router[layer].T, brt, s)

        pc16 = pc[:, :, 0].astype(I32)
        pcnt, loff, gstart, tails, seg_end = _routing_tables(pc16)
        n_rows = t * TOP_K + N_EXPERTS * (t // TOKEN_TILE) * ROW_ALIGN + N_EXPERTS * (EXPERT_TILE - 1)
        n_rows = (n_rows + EXPERT_TILE - 1) // EXPERT_TILE * EXPERT_TILE
        n_tiles = n_rows // EXPERT_TILE
        tile_start = jnp.arange(n_tiles, dtype=I32) * EXPERT_TILE
        tile_expert = jnp.minimum(jnp.sum(seg_end[None, :] <= tile_start[:, None], axis=1), N_EXPERTS - 1).astype(I32)
        n_used = (seg_end[-1] // EXPERT_TILE).astype(I32)[None]
        xs = _dispatch(loff, pcnt, gstart, tails, h2, pos, n_rows)
        has_rows = jnp.diff(seg_end, prepend=0) > 0
        experts = jnp.arange(N_EXPERTS, dtype=I32)
        stage_slot = ((jnp.cumsum(has_rows) - 1) % 2).astype(I32)
        next_expert = jnp.min(jnp.where(has_rows[None, :] & (experts[None, :] > experts[:, None]),
                                        experts[None, :], N_EXPERTS), axis=1).astype(I32)
        ys = _experts(tile_expert, n_used, stage_slot, next_expert, xs, w_gate_up[layer],
                      b_gate_up[layer][:, None, :], w_down[layer], b_down[layer][:, None, :])

        post = pos.transpose(0, 2, 1).reshape(t, 8)
        gatet = gates.transpose(0, 2, 1).reshape(t, 8)
        assert layer == w_in.shape[0] - 1, "single-layer pipeline: the final norm is fused into combine"
        x2d = _combine(loff, pcnt, gstart, x1, post, gatet, p[layer].reshape(t, PLE_DIM), g_ple[layer][None],
                       w_ple_gate[layer].astype(BF16), w_ple_proj[layer].astype(BF16), g_final[None], ys)
    return x2d.reshape(b, s, d)
```

```python
import functools
import math

import jax
import jax.numpy as jnp
from jax import lax
from jax.experimental import pallas as pl
from jax.experimental.pallas import tpu as pltpu

F32 = jnp.float32
BF16 = jnp.bfloat16
I32 = jnp.int32

D_MODEL = 1024
POOL_WIDTH = 512
POOL_WINDOWS = (2, 4, 8, 16)
POOL_GROUP_DIM = 128
ATTN_WIDTH = 512
QKV_WIDTH = 3 * ATTN_WIDTH
HEAD_DIM = 64
N_HEADS = 8
DILATED_BRANCHES = ((128, 1), (512, 4), (2048, 16))
ATTN_BLOCK = 128
N_REL_BUCKETS = 32
REL_MAX_EXACT = 16
REL_MAX_DISTANCE = 2048
N_EXPERTS = 32
TOP_K = 4
D_EXPERT = 1024
SWIGLU_LIMIT = 7.0
SWIGLU_ALPHA = 1.702
PLE_DIM = 256
NORM_EPS = 1e-6
NEG_INF = -1e30
LOG2_E = math.log2(math.e)
QUERY_SCALE = LOG2_E / math.sqrt(HEAD_DIM)

LANES = 128
POOL_HALO = 16
TOKEN_TILE = 512
ROW_ALIGN = 16
EXPERT_TILE = 1024
EXPERT_HALF = EXPERT_TILE // 2
EXPERT_CHUNK = 256
LOCAL_ROWS = TOKEN_TILE * TOP_K + N_EXPERTS * ROW_ALIGN
SORT_BLOCK = 256
GATHER_BLOCK = 512
ATTN_STEP = ((1, 8), (1, 8), (4, 2))
VMEM_LIMIT = 56 * 1024 * 1024


def _rms(x, g):
    return x * lax.rsqrt(jnp.mean(x * x, axis=-1, keepdims=True) + NORM_EPS) * g


def _inproj_kernel(x_ref, g_ref, w_ref, u_ref, nat_ref, d4_ref, d16_ref, z_scr, g_scr):
    h = _rms(x_ref[...], g_ref[...]).astype(BF16)
    part = ATTN_WIDTH
    slabs = part // LANES

    def project(i):
        return jnp.dot(h, w_ref[:, pl.ds(i * part, part)], preferred_element_type=F32)

    def regroup(z, i):
        col0 = (i - 1) * part
        nat_ref[:, pl.ds(col0, part)] = z.astype(BF16)
        for c in range(slabs):
            slab = (i - 1) * slabs + c
            cols = col0 + c * LANES
            z_scr[slab] = z[:, c * LANES:(c + 1) * LANES]
            for rl in range(4):
                grp = z_scr[slab, pl.ds(rl, TOKEN_TILE // 4, stride=4), :]
                d4_ref[:, pl.ds(rl * QKV_WIDTH + cols, LANES)] = grp.astype(BF16)
                g_scr[rl * n_slabs + slab] = grp
            for rl in range(4):
                for rh in range(4):
                    d16_ref[:, pl.ds((4 * rh + rl) * QKV_WIDTH + cols, LANES)] = (
                        g_scr[rl * n_slabs + slab, pl.ds(rh, TOKEN_TILE // 16, stride=4), :].astype(BF16))

    assert POOL_WIDTH == part and QKV_WIDTH == 3 * part and [d for _, d in DILATED_BRANCHES] == [1, 4, 16]
    n_slabs = QKV_WIDTH // LANES
    z_prev = project(0)
    for i in range(1, 4):
        z_next = project(i)
        if i == 1:
            u_ref[...] = z_prev
        else:
            regroup(z_prev, i - 1)
        z_prev = z_next
    regroup(z_prev, 3)


def _inproj(x2d, g, w_bf16):
    t = x2d.shape[0]
    in_w = w_bf16.shape[1]
    tt = TOKEN_TILE
    return pl.pallas_call(
        _inproj_kernel,
        grid=(t // tt,),
        in_specs=[
            pl.BlockSpec((tt, D_MODEL), lambda i: (i, 0)),
            pl.BlockSpec((1, D_MODEL), lambda i: (0, 0)),
            pl.BlockSpec((D_MODEL, in_w), lambda i: (0, 0)),
        ],
        out_specs=[
            pl.BlockSpec((tt, POOL_WIDTH), lambda i: (i, 0)),
            pl.BlockSpec((tt, QKV_WIDTH), lambda i: (i, 0)),
            pl.BlockSpec((tt // 4, 4 * QKV_WIDTH), lambda i: (i, 0)),
            pl.BlockSpec((tt // 16, 16 * QKV_WIDTH), lambda i: (i, 0)),
        ],
        out_shape=[
            jax.ShapeDtypeStruct((t, POOL_WIDTH), F32),
            jax.ShapeDtypeStruct((t, QKV_WIDTH), BF16),
            jax.ShapeDtypeStruct((t // 4, 4 * QKV_WIDTH), BF16),
            jax.ShapeDtypeStruct((t // 16, 16 * QKV_WIDTH), BF16),
        ],
        scratch_shapes=[pltpu.VMEM((QKV_WIDTH // LANES, tt, LANES), F32),
                        pltpu.VMEM((4 * QKV_WIDTH // LANES, tt // 4, LANES), F32)],
        compiler_params=pltpu.CompilerParams(
            dimension_semantics=("parallel",), vmem_limit_bytes=VMEM_LIMIT),
        name="inproj",
    )(x2d, g, w_bf16)


def _attn_kernel(main_ref, prev_ref, tab_ref, o_ref, lse_ref, k_scr, vt_scr, *, rq, nq):
    blk = ATTN_BLOCK
    first_group = pl.program_id(2) == 0
    nt = (((1,), (1,)), ((), ()))

    for ri in range(rq):
        base = ri * QKV_WIDTH
        k_scr[0:blk, :] = prev_ref[:, pl.ds(base + ATTN_WIDTH, ATTN_WIDTH)]
        k_scr[blk:, :] = main_ref[:, pl.ds(base + ATTN_WIDTH, ATTN_WIDTH)]
        vt_scr[:, 0:blk] = prev_ref[:, pl.ds(base + 2 * ATTN_WIDTH, ATTN_WIDTH)].astype(F32).T.astype(BF16)
        for j in range(nq):
            vt_scr[:, pl.ds((j + 1) * blk, blk)] = (
                main_ref[pl.ds(j * blk, blk), pl.ds(base + 2 * ATTN_WIDTH, ATTN_WIDTH)].astype(F32).T.astype(BF16))
        for j in range(nq):
            rows = pl.ds(j * blk, blk)
            keys = pl.ds(j * blk, 2 * blk)
            scores = []
            for h in range(N_HEADS):
                q = main_ref[rows, pl.ds(base + h * HEAD_DIM, HEAD_DIM)]
                k2 = k_scr[keys, pl.ds(h * HEAD_DIM, HEAD_DIM)]
                scores.append(lax.dot_general(k2, q, nt, preferred_element_type=F32))
            probs, inv_l, lse_parts = [], [], []
            for h in range(N_HEADS):
                s = scores[h] + tab_ref[h]
                if j == 0:
                    pen = jnp.where(first_group, NEG_INF, 0.0).astype(F32)
                    s = jnp.concatenate([s[:blk] + pen, s[blk:]], axis=0)
                m = jnp.max(s, axis=0, keepdims=True)
                p = jnp.exp2(s - m)
                l = jnp.sum(p, axis=0, keepdims=True)
                probs.append(p.astype(BF16))
                inv_l.append(1.0 / l)
                lse_parts.append(m * math.log(2.0) + jnp.log(l))
            o_parts = []
            for h in range(N_HEADS):
                vt2 = vt_scr[pl.ds(h * HEAD_DIM, HEAD_DIM), keys]
                o_parts.append(jnp.dot(vt2, probs[h], preferred_element_type=F32) * inv_l[h])
            o_t = jnp.concatenate(o_parts, axis=0)
            o_ref[rows, pl.ds(ri * ATTN_WIDTH, ATTN_WIDTH)] = o_t.T.astype(BF16)
            lse_t = jnp.concatenate(lse_parts + [jnp.zeros((LANES - N_HEADS, blk), F32)], axis=0)
            lse_ref[rows, pl.ds(ri * LANES, LANES)] = lse_t.T


def _attn_branch(qkv_d, tab, batch, seq, dil, rq, nq):
    sub = seq // dil
    nb = sub // ATTN_BLOCK
    groups = nb // nq
    return pl.pallas_call(
        functools.partial(_attn_kernel, rq=rq, nq=nq),
        grid=(batch, dil // rq, groups),
        in_specs=[
            pl.BlockSpec((nq * ATTN_BLOCK, rq * QKV_WIDTH), lambda b, r, g: (b * groups + g, r)),
            pl.BlockSpec((ATTN_BLOCK, rq * QKV_WIDTH), lambda b, r, g: (b * nb + jnp.maximum(g * nq - 1, 0), r)),
            pl.BlockSpec((N_HEADS, 2 * ATTN_BLOCK, ATTN_BLOCK), lambda b, r, g: (0, 0, 0)),
        ],
        out_specs=[
            pl.BlockSpec((nq * ATTN_BLOCK, rq * ATTN_WIDTH), lambda b, r, g: (b * groups + g, r)),
            pl.BlockSpec((nq * ATTN_BLOCK, rq * LANES), lambda b, r, g: (b * groups + g, r)),
        ],
        out_shape=[
            jax.ShapeDtypeStruct((batch * sub, dil * ATTN_WIDTH), BF16),
            jax.ShapeDtypeStruct((batch * sub, dil * LANES), F32),
        ],
        scratch_shapes=[
            pltpu.VMEM(((nq + 1) * ATTN_BLOCK, ATTN_WIDTH), BF16),
            pltpu.VMEM((ATTN_WIDTH, (nq + 1) * ATTN_BLOCK), BF16),
        ],
        compiler_params=pltpu.CompilerParams(
            dimension_semantics=("parallel", "parallel", "parallel"), vmem_limit_bytes=VMEM_LIMIT),
        name="attn",
    )(qkv_d, qkv_d, tab)


def _t5_bucket(dist):
    n = jnp.maximum(dist, 1).astype(F32)
    large = REL_MAX_EXACT + (jnp.log(n / REL_MAX_EXACT) / math.log(REL_MAX_DISTANCE / REL_MAX_EXACT)
                             * (N_REL_BUCKETS - REL_MAX_EXACT)).astype(I32)
    large = jnp.minimum(large, N_REL_BUCKETS - 1)
    return jnp.where(dist < REL_MAX_EXACT, dist, large)


def _shifted_rows(w, n):
    lead = w.shape[:-1]
    width = w.shape[-1]
    flat = jnp.tile(w, (1,) * len(lead) + (n + 1,))[..., :n * (width + 1)]
    return flat.reshape(lead + (n, width + 1))[..., :n]


def _bias_tables(rel_bias, window, dil):
    blk = ATTN_BLOCK
    assert window // dil == blk
    f = rel_bias[_t5_bucket(jnp.arange(blk + 1) * dil)].T.astype(F32)
    neg = jnp.full((N_HEADS, blk - 1), NEG_INF, F32)
    neg1 = jnp.full((N_HEADS, 1), NEG_INF, F32)
    w_prev = jnp.concatenate([neg, f[:, :0:-1], neg1], axis=1)
    prev = _shifted_rows(w_prev, blk)[:, :, ::-1]
    w_cur = jnp.concatenate([neg, f[:, :blk], neg1], axis=1)
    cur = _shifted_rows(w_cur, blk)[:, ::-1, :]
    return jnp.concatenate([prev, cur], axis=1) * LOG2_E


def _mixout_kernel(x_ref, u_ref, uh_ref, o1_ref, o4_ref, o16_ref, l1_ref, l4_ref, l16_ref,
                   wpool_ref, pscale_ref, wout_ref, gffn_ref, wrt_ref, brt_ref, spread_ref,
                   x1_ref, h2_ref, pos_ref, gate_ref, pc_ref,
                   ext_scr, mix_scr, o_scr, l_scr, *, tiles_per_seq):
    i = pl.program_id(0)
    tt = TOKEN_TILE
    seq_tile = i % tiles_per_seq

    halo = uh_ref[...]
    ext_scr[0:POOL_HALO, :] = jnp.where(seq_tile == 0, jnp.zeros_like(halo), halo)
    ext_scr[POOL_HALO:, :] = u_ref[...]
    tpos = seq_tile * tt + lax.broadcasted_iota(I32, (tt, 1), 0)
    for gi, w in enumerate(POOL_WINDOWS):
        cols = pl.ds(gi * POOL_GROUP_DIM, POOL_GROUP_DIM)
        tok = ext_scr[pl.ds(POOL_HALO, tt), cols]
        acc = tok
        for j in range(1, w):
            acc = acc + ext_scr[pl.ds(POOL_HALO - j, tt), cols]
        cnt = jnp.minimum(tpos + 1, w).astype(F32)
        pooled = (acc / cnt - tok).astype(BF16)
        mixed = jnp.dot(pooled, wpool_ref[gi], preferred_element_type=F32) * pscale_ref[:, cols]
        mix_scr[:, cols] = mixed.astype(BF16)

    o_slabs = ATTN_WIDTH // LANES
    for bi, (dil, o_ref, l_ref) in enumerate(((4, o4_ref, l4_ref), (16, o16_ref, l16_ref))):
        rows = tt // dil
        for r in range(dil):
            l_scr[bi, pl.ds(r, rows, stride=dil), :] = l_ref[:, pl.ds(r * LANES, LANES)]
            for c in range(o_slabs):
                o_scr[bi * o_slabs + c, pl.ds(r, rows, stride=dil), :] = (
                    o_ref[:, pl.ds(r * ATTN_WIDTH + c * LANES, LANES)].astype(F32))
    l1, l4, l16 = l1_ref[...], l_scr[0], l_scr[1]
    lm = jnp.maximum(jnp.maximum(l1, l4), l16)
    e1, e4, e16 = jnp.exp(l1 - lm), jnp.exp(l4 - lm), jnp.exp(l16 - lm)
    inv = 1.0 / (e1 + e4 + e16)

    def per_feature(w):
        hi = w.astype(BF16)
        lo = (w - hi.astype(F32)).astype(BF16)
        return (jnp.dot(hi, spread_ref[...], preferred_element_type=F32)
                + jnp.dot(lo, spread_ref[...], preferred_element_type=F32))

    w1, w4, w16 = per_feature(e1 * inv), per_feature(e4 * inv), per_feature(e16 * inv)
    for c in range(o_slabs):
        sl = pl.ds(c * LANES, LANES)
        cs = slice(c * LANES, (c + 1) * LANES)
        o = w1[:, cs] * o1_ref[:, sl].astype(F32) + w4[:, cs] * o_scr[c] + w16[:, cs] * o_scr[o_slabs + c]
        mix_scr[:, pl.ds(POOL_WIDTH + c * LANES, LANES)] = o.astype(BF16)

    x1 = x_ref[...] + jnp.dot(mix_scr[...], wout_ref[...], preferred_element_type=F32)
    x1_ref[...] = x1

    h2 = _rms(x1, gffn_ref[...])
    h2_ref[...] = h2.astype(BF16)
    nt = (((1,), (1,)), ((), ()))
    wr = wrt_ref[...]
    wr_hi = wr.astype(BF16)
    wr_lo = (wr - wr_hi.astype(F32)).astype(BF16)
    h2_hi = h2.astype(BF16)
    h2_lo = (h2 - h2_hi.astype(F32)).astype(BF16)
    logits = (lax.dot_general(wr_hi, h2_hi, nt, preferred_element_type=F32)
              + lax.dot_general(wr_hi, h2_lo, nt, preferred_element_type=F32)
              + lax.dot_general(wr_lo, h2_hi, nt, preferred_element_type=F32))
    logits = logits + brt_ref[:, 0:1]
    eio = lax.broadcasted_iota(I32, (N_EXPERTS, tt), 0)
    sel, val = [], []
    for _ in range(TOP_K):
        m = jnp.max(logits, axis=0, keepdims=True)
        idx = jnp.min(jnp.where(logits == m, eio, N_EXPERTS), axis=0, keepdims=True)
        sel.append(idx)
        val.append(m)
        logits = jnp.where(eio == idx, -jnp.inf, logits)
    ex = [jnp.exp(v - val[0]) for v in val]
    den = ex[0] + ex[1] + ex[2] + ex[3]
    gates = [e / den for e in ex]

    hot = [(eio == s) for s in sel]
    onehot = (hot[0] | hot[1] | hot[2] | hot[3]).astype(F32)
    ti = lax.broadcasted_iota(I32, (tt, tt), 0)
    tj = lax.broadcasted_iota(I32, (tt, tt), 1)
    before = (ti < tj).astype(BF16)
    rank = jnp.dot(onehot.astype(BF16), before, preferred_element_type=F32)
    cnt = jnp.sum(onehot, axis=1, keepdims=True)
    pc = jnp.maximum(jnp.floor((cnt + (ROW_ALIGN - 1)) * (1.0 / ROW_ALIGN)), 1.0)
    pcb = jnp.broadcast_to(pc, (N_EXPERTS, 128))
    pc_ref[0] = pcb
    pcm = jnp.concatenate([pcb, jnp.zeros((128 - N_EXPERTS, 128), F32)], axis=0).astype(BF16)
    li = lax.broadcasted_iota(I32, (N_EXPERTS, 128), 0)
    lj = lax.broadcasted_iota(I32, (N_EXPERTS, 128), 1)
    lower = (lj < li).astype(BF16)
    off = jnp.dot(lower, pcm, preferred_element_type=F32)[:, 0:1] * float(ROW_ALIGN)
    where_to = off + rank
    zero_i = jnp.zeros((1, tt), I32)
    zero_f = jnp.zeros((1, tt), F32)
    pos_rows = [jnp.sum(jnp.where(hk, where_to, 0.0), axis=0, keepdims=True).astype(I32) for hk in hot]
    pos_ref[0] = jnp.concatenate(pos_rows + [zero_i] * (8 - TOP_K), axis=0)
    gate_ref[0] = jnp.concatenate(gates + [zero_f] * (8 - TOP_K), axis=0)


def _mixout(x2d, u, o_list, lse_list, wpool, pscale, wout, gffn, wrt, brt, seq):
    t = x2d.shape[0]
    tt = TOKEN_TILE
    n_tt = t // tt
    halo_blocks = tt // POOL_HALO
    row = lambda i: (i, 0)
    const2 = lambda i: (0, 0)
    dils = [dil for _, dil in DILATED_BRANCHES]
    spread = (jnp.arange(LANES)[:, None] == jnp.arange(ATTN_WIDTH)[None, :] // HEAD_DIM).astype(BF16)
    return pl.pallas_call(
        functools.partial(_mixout_kernel, tiles_per_seq=seq // tt),
        grid=(n_tt,),
        in_specs=[
            pl.BlockSpec((tt, D_MODEL), row),
            pl.BlockSpec((tt, POOL_WIDTH), row),
            pl.BlockSpec((POOL_HALO, POOL_WIDTH), lambda i: (jnp.maximum(i * halo_blocks - 1, 0), 0)),
            *[pl.BlockSpec((tt // dil, dil * ATTN_WIDTH), row) for dil in dils],
            *[pl.BlockSpec((tt // dil, dil * LANES), row) for dil in dils],
            pl.BlockSpec((len(POOL_WINDOWS), POOL_GROUP_DIM, POOL_GROUP_DIM), lambda i: (0, 0, 0)),
            pl.BlockSpec((1, POOL_WIDTH), const2),
            pl.BlockSpec((D_MODEL, D_MODEL), const2),
            pl.BlockSpec((1, D_MODEL), const2),
            pl.BlockSpec((N_EXPERTS, D_MODEL), const2),
            pl.BlockSpec((N_EXPERTS, 128), const2),
            pl.BlockSpec((LANES, ATTN_WIDTH), const2),
        ],
        out_specs=[
            pl.BlockSpec((tt, D_MODEL), row),
            pl.BlockSpec((tt, D_MODEL), row),
            pl.BlockSpec((1, 8, tt), lambda i: (i, 0, 0)),
            pl.BlockSpec((1, 8, tt), lambda i: (i, 0, 0)),
            pl.BlockSpec((1, N_EXPERTS, 128), lambda i: (i, 0, 0)),
        ],
        out_shape=[
            jax.ShapeDtypeStruct((t, D_MODEL), F32),
            jax.ShapeDtypeStruct((t, D_MODEL), BF16),
            jax.ShapeDtypeStruct((n_tt, 8, tt), I32),
            jax.ShapeDtypeStruct((n_tt, 8, tt), F32),
            jax.ShapeDtypeStruct((n_tt, N_EXPERTS, 128), F32),
        ],
        scratch_shapes=[
            pltpu.VMEM((tt + POOL_HALO, POOL_WIDTH), F32),
            pltpu.VMEM((tt, D_MODEL), BF16),
            pltpu.VMEM((2 * ATTN_WIDTH // LANES, tt, LANES), F32),
            pltpu.VMEM((2, tt, LANES), F32),
        ],
        compiler_params=pltpu.CompilerParams(
            dimension_semantics=("parallel",), vmem_limit_bytes=VMEM_LIMIT),
        name="mixout",
    )(x2d, u, u, *o_list, *lse_list, wpool, pscale, wout, gffn, wrt, brt, spread)


def _chunk_copies(loff_s, pcnt_s, gstart_s, tile, local_ref, global_ref, sems, to_global):
    slot = tile % 2
    out = []
    for e in range(N_EXPERTS):
        n = pl.multiple_of(pcnt_s[tile, e], ROW_ALIGN)
        lo = pl.multiple_of(loff_s[tile, e], ROW_ALIGN)
        go = pl.multiple_of(gstart_s[tile, e], ROW_ALIGN)
        loc = local_ref.at[slot, pl.ds(lo, n)]
        glo = global_ref.at[pl.ds(go, n)]
        src, dst = (loc, glo) if to_global else (glo, loc)
        out.append(pltpu.make_async_copy(src, dst, sems.at[slot]))
    return out


def _start_all(copies):
    for cp in copies:
        cp.start()


def _wait_chunks(loff_s, pcnt_s, tile, local_ref, global_ref, sems, to_global):
    slot = tile % 2
    rows = pl.multiple_of(loff_s[tile, N_EXPERTS - 1] + pcnt_s[tile, N_EXPERTS - 1], ROW_ALIGN)
    loc = local_ref.at[slot, pl.ds(0, rows)]
    glo = global_ref.at[pl.ds(0, rows)]
    src, dst = (loc, glo) if to_global else (glo, loc)
    pltpu.make_async_copy(src, dst, sems.at[slot]).wait()


def _dispatch_kernel(loff_s, pcnt_s, gstart_s, tail_s, h2_ref, pos_ref, xs_ref, loc_scr, zero_scr, sem, tail_sem,
                     spare_sem):
    i = pl.program_id(0)
    tt = TOKEN_TILE

    def tail_copies():
        out = []
        for e in range(N_EXPERTS):
            n = pl.multiple_of(tail_s[1, e], ROW_ALIGN)
            start = pl.multiple_of(tail_s[0, e], ROW_ALIGN)
            out.append((n > 0, pltpu.make_async_copy(zero_scr.at[pl.ds(0, n)], xs_ref.at[pl.ds(start, n)], tail_sem)))
        return out

    @pl.when(i == 0)
    def _():
        zero_scr[...] = jnp.zeros_like(zero_scr)
        for cond, cp in tail_copies():
            pl.when(cond)(cp.start)

    @pl.when(i == pl.num_programs(0) - 1)
    def _():
        for cond, cp in tail_copies():
            pl.when(cond)(cp.wait)

    def spare_tile(j):
        return pltpu.make_async_copy(
            zero_scr, xs_ref.at[pl.ds(pl.multiple_of(j * EXPERT_TILE, EXPERT_TILE), EXPERT_TILE)], spare_sem)

    n_tiles = xs_ref.shape[0] // EXPERT_TILE

    @pl.when(i == 0)
    def _():
        lax.fori_loop(tail_s[2, 0], n_tiles, lambda j, c: (spare_tile(j).start(), c)[1], 0)

    @pl.when(i == pl.num_programs(0) - 1)
    def _():
        lax.fori_loop(tail_s[2, 0], n_tiles, lambda j, c: (spare_tile(j).wait(), c)[1], 0)

    pos = pos_ref[0]
    h2 = h2_ref[...]
    slot = i % 2
    used_rows = loff_s[i, N_EXPERTS - 1] + pcnt_s[i, N_EXPERTS - 1]
    def one_hot(jb):
        jio = lax.broadcasted_iota(I32, (SORT_BLOCK, tt), 0) + jb * SORT_BLOCK
        hit = (jio == pos[0:1]) | (jio == pos[1:2]) | (jio == pos[2:3]) | (jio == pos[3:4])
        return jnp.where(hit, 1.0, 0.0).astype(BF16)

    def sort_block(jb, onehot):
        loc_scr[slot, pl.ds(jb * SORT_BLOCK, SORT_BLOCK), :] = jnp.dot(
            onehot, h2, preferred_element_type=F32).astype(BF16)

    n_sure = TOKEN_TILE * TOP_K // SORT_BLOCK
    onehot = one_hot(0)
    for jb in range(n_sure):
        nxt = one_hot(jb + 1) if jb + 1 < n_sure else None
        sort_block(jb, onehot)
        onehot = nxt
    for jb in range(n_sure, LOCAL_ROWS // SORT_BLOCK):
        pl.when(jb * SORT_BLOCK < used_rows)(lambda jb=jb: sort_block(jb, one_hot(jb)))

    @pl.when(i > 0)
    def _():
        _wait_chunks(loff_s, pcnt_s, i - 1, loc_scr, xs_ref, sem, True)

    _start_all(_chunk_copies(loff_s, pcnt_s, gstart_s, i, loc_scr, xs_ref, sem, True))

    @pl.when(i == pl.num_programs(0) - 1)
    def _():
        _wait_chunks(loff_s, pcnt_s, i, loc_scr, xs_ref, sem, True)


def _dispatch(loff, pcnt, gstart, tails, h2, pos, n_rows):
    t = h2.shape[0]
    tt = TOKEN_TILE
    grid_spec = pltpu.PrefetchScalarGridSpec(
        num_scalar_prefetch=4,
        grid=(t // tt,),
        in_specs=[
            pl.BlockSpec((tt, D_MODEL), lambda i, *_: (i, 0)),
            pl.BlockSpec((1, 8, tt), lambda i, *_: (i, 0, 0)),
        ],
        out_specs=pl.BlockSpec(memory_space=pl.ANY),
        scratch_shapes=[
            pltpu.VMEM((2, LOCAL_ROWS, D_MODEL), BF16),
            pltpu.VMEM((EXPERT_TILE, D_MODEL), BF16),
            pltpu.SemaphoreType.DMA((2,)),
            pltpu.SemaphoreType.DMA(()),
            pltpu.SemaphoreType.DMA(()),
        ],
    )
    return pl.pallas_call(
        _dispatch_kernel,
        grid_spec=grid_spec,
        out_shape=jax.ShapeDtypeStruct((n_rows, D_MODEL), BF16),
        compiler_params=pltpu.CompilerParams(
            dimension_semantics=("arbitrary",), vmem_limit_bytes=VMEM_LIMIT),
        name="dispatch",
    )(loff, pcnt, gstart, tails, h2, pos)


def _expert_kernel(te_s, nu_s, slot_s, next_s, full_s, xs_ref, wgu_hbm, bgu_ref, wd_hbm, bd_ref, ys_ref,
                   wgu_stage, wd_stage, wgu_bf, wd_bf, act_scr, wsem):
    i = pl.program_id(0)
    live = i < nu_s[0]
    expert = te_s[i]
    new_expert = (i == 0) | (expert != te_s[jnp.maximum(i - 1, 0)])

    def weight_copies(e, slot):
        return (pltpu.make_async_copy(wgu_hbm.at[e], wgu_stage.at[slot], wsem.at[0, slot]),
                pltpu.make_async_copy(wd_hbm.at[e], wd_stage.at[slot], wsem.at[1, slot]))

    @pl.when(live & new_expert)
    def _():
        slot = slot_s[expert]

        @pl.when(i == 0)
        def _():
            for cp in weight_copies(expert, slot):
                cp.start()

        for cp in weight_copies(expert, slot):
            cp.wait()
        wgu_bf[...] = wgu_stage[slot].astype(BF16)
        wd_bf[...] = wd_stage[slot].astype(BF16)
        upcoming = next_s[expert]

        @pl.when(upcoming < N_EXPERTS)
        def _():
            for cp in weight_copies(upcoming, 1 - slot):
                cp.start()

    def mlp(half):
        rows = pl.ds(half * EXPERT_HALF, EXPERT_HALF)
        x = xs_ref[rows, :]
        n_chunks = D_EXPERT // EXPERT_CHUNK

        def gate_up(c):
            gc = pl.ds(c * EXPERT_CHUNK, EXPERT_CHUNK)
            uc = pl.ds(D_EXPERT + c * EXPERT_CHUNK, EXPERT_CHUNK)
            return (jnp.dot(x, wgu_bf[:, gc], preferred_element_type=F32) + bgu_ref[0, :, gc],
                    jnp.dot(x, wgu_bf[:, uc], preferred_element_type=F32) + bgu_ref[0, :, uc])

        g, u = gate_up(0)
        for c in range(n_chunks):
            nxt = gate_up(c + 1) if c + 1 < n_chunks else None
            g = jnp.minimum(g, SWIGLU_LIMIT)
            u = jnp.clip(u, -SWIGLU_LIMIT, SWIGLU_LIMIT)
            act_scr[:, pl.ds(c * EXPERT_CHUNK, EXPERT_CHUNK)] = (
                (u + 1.0) * (g * jax.nn.sigmoid(SWIGLU_ALPHA * g))).astype(BF16)
            if nxt is not None:
                g, u = nxt
        y = jnp.dot(act_scr[...], wd_bf[...], preferred_element_type=F32) + bd_ref[0]
        ys_ref[rows, :] = y.astype(BF16)

    def blank(half):
        ys_ref[pl.ds(half * EXPERT_HALF, EXPERT_HALF), :] = jnp.zeros((EXPERT_HALF, D_MODEL), BF16)

    second = live & (full_s[i] > 0)
    pl.when(live)(lambda: mlp(0))
    pl.when(second)(lambda: mlp(1))
    pl.when(jnp.logical_not(live))(lambda: blank(0))
    pl.when(jnp.logical_not(second))(lambda: blank(1))


def _experts(tile_expert, n_used, stage_slot, next_expert, tile_full, xs, wgu, bgu, wd, bd):
    n_rows = xs.shape[0]
    tm = EXPERT_TILE
    live = lambda i, te, nu: jnp.minimum(i, nu[0] - 1)
    grid_spec = pltpu.PrefetchScalarGridSpec(
        num_scalar_prefetch=5,
        grid=(n_rows // tm,),
        in_specs=[
            pl.BlockSpec((tm, D_MODEL), lambda i, te, nu, *_: (live(i, te, nu), 0)),
            pl.BlockSpec(memory_space=pl.ANY),
            pl.BlockSpec((1, 1, 2 * D_EXPERT), lambda i, te, nu, *_: (te[live(i, te, nu)], 0, 0)),
            pl.BlockSpec(memory_space=pl.ANY),
            pl.BlockSpec((1, 1, D_MODEL), lambda i, te, nu, *_: (te[live(i, te, nu)], 0, 0)),
        ],
        out_specs=pl.BlockSpec((tm, D_MODEL), lambda i, te, nu, *_: (i, 0)),
        scratch_shapes=[
            pltpu.VMEM((2, D_MODEL, 2 * D_EXPERT), F32),
            pltpu.VMEM((2, D_EXPERT, D_MODEL), F32),
            pltpu.VMEM((D_MODEL, 2 * D_EXPERT), BF16),
            pltpu.VMEM((D_EXPERT, D_MODEL), BF16),
            pltpu.VMEM((EXPERT_HALF, D_EXPERT), BF16),
            pltpu.SemaphoreType.DMA((2, 2)),
        ],
    )
    return pl.pallas_call(
        _expert_kernel,
        grid_spec=grid_spec,
        out_shape=jax.ShapeDtypeStruct((n_rows, D_MODEL), BF16),
        compiler_params=pltpu.CompilerParams(
            dimension_semantics=("arbitrary",), vmem_limit_bytes=VMEM_LIMIT),
        name="experts",
    )(tile_expert, n_used, stage_slot, next_expert, tile_full, xs, wgu, bgu, wd, bd)


def _combine_kernel(loff_s, pcnt_s, gstart_s, x1_ref, post_ref, gatet_ref, p_ref, gple_ref, wg_ref, wp_ref,
                    gfin_ref, ys_ref, out_ref, loc_scr, sem):
    i = pl.program_id(0)
    tt = TOKEN_TILE

    @pl.when(i == 0)
    def _():
        loc_scr[...] = jnp.zeros_like(loc_scr)
        _start_all(_chunk_copies(loff_s, pcnt_s, gstart_s, i, loc_scr, ys_ref, sem, False))

    _wait_chunks(loff_s, pcnt_s, i, loc_scr, ys_ref, sem, False)

    @pl.when(i + 1 < pl.num_programs(0))
    def _():
        _start_all(_chunk_copies(loff_s, pcnt_s, gstart_s, i + 1, loc_scr, ys_ref, sem, False))

    slot = i % 2

    post = post_ref[...]
    gatet = gatet_ref[...]

    def gate_block(jb):
        jio = lax.broadcasted_iota(I32, (tt, GATHER_BLOCK), 1) + jb * GATHER_BLOCK
        w = jnp.zeros((tt, GATHER_BLOCK), F32)
        for k in range(TOP_K):
            w = jnp.where(jio == post[:, k:k + 1], gatet[:, k:k + 1], w)
        return w.astype(BF16)

    n_blocks = LOCAL_ROWS // GATHER_BLOCK
    moe = jnp.zeros((tt, D_MODEL), F32)
    w = gate_block(0)
    for jb in range(n_blocks):
        nxt = gate_block(jb + 1) if jb + 1 < n_blocks else None
        y = loc_scr[slot, pl.ds(jb * GATHER_BLOCK, GATHER_BLOCK), :]
        moe = moe + jnp.dot(w, y, preferred_element_type=F32)
        w = nxt

    x2 = x1_ref[...] + moe
    gate = jax.nn.sigmoid(jnp.dot(_rms(x2, gple_ref[...]).astype(BF16), wg_ref[...], preferred_element_type=F32))
    emb = jnp.dot(p_ref[...].astype(BF16), wp_ref[...], preferred_element_type=F32)
    out_ref[...] = _rms(x2 + emb * gate, gfin_ref[...])


def _combine(loff, pcnt, gstart, x1, post, gatet, p2d, gple, wg, wp, gfin, ys):
    t = x1.shape[0]
    tt = TOKEN_TILE
    row = lambda i, *_: (i, 0)
    const2 = lambda i, *_: (0, 0)
    grid_spec = pltpu.PrefetchScalarGridSpec(
        num_scalar_prefetch=3,
        grid=(t // tt,),
        in_specs=[
            pl.BlockSpec((tt, D_MODEL), row),
            pl.BlockSpec((tt, 8), row),
            pl.BlockSpec((tt, 8), row),
            pl.BlockSpec((tt, PLE_DIM), row),
            pl.BlockSpec((1, D_MODEL), const2),
            pl.BlockSpec((D_MODEL, D_MODEL), const2),
            pl.BlockSpec((PLE_DIM, D_MODEL), const2),
            pl.BlockSpec((1, D_MODEL), const2),
            pl.BlockSpec(memory_space=pl.ANY),
        ],
        out_specs=pl.BlockSpec((tt, D_MODEL), row),
        scratch_shapes=[pltpu.VMEM((2, LOCAL_ROWS, D_MODEL), BF16), pltpu.SemaphoreType.DMA((2,))],
    )
    return pl.pallas_call(
        _combine_kernel,
        grid_spec=grid_spec,
        out_shape=jax.ShapeDtypeStruct((t, D_MODEL), F32),
        compiler_params=pltpu.CompilerParams(
            dimension_semantics=("arbitrary",), vmem_limit_bytes=VMEM_LIMIT),
        name="combine",
    )(loff, pcnt, gstart, x1, post, gatet, p2d, gple, wg, wp, gfin, ys)


def _routing_tables(pc16):
    pcnt = pc16 * ROW_ALIGN
    loff = jnp.cumsum(pcnt, axis=1) - pcnt
    seg = jnp.sum(pcnt, axis=0)
    segpad = (seg + EXPERT_TILE - 1) // EXPERT_TILE * EXPERT_TILE
    seg_end = jnp.cumsum(segpad)
    ebase = seg_end - segpad
    gstart = ebase[None, :] + jnp.cumsum(pcnt, axis=0) - pcnt
    tails = jnp.stack([ebase + seg, segpad - seg, jnp.broadcast_to(seg_end[-1] // EXPERT_TILE, seg.shape)])
    return pcnt, loff, gstart, tails, seg, ebase, seg_end


def kernel(x, p, g_mix, w_in, w_pool, pool_scale, rel_bias, w_out, g_ffn, w_router, b_router, w_gate_up,
           b_gate_up, w_down, b_down, g_ple, w_ple_gate, w_ple_proj, g_final):
    b, s, d = x.shape
    t = b * s
    x2d = x.reshape(t, d)
    for layer in range(w_in.shape[0]):
        col = jnp.arange(w_in.shape[2])
        is_q = (col >= POOL_WIDTH) & (col < POOL_WIDTH + ATTN_WIDTH)
        w_in_scaled = (w_in[layer] * jnp.where(is_q, QUERY_SCALE, 1.0)).astype(BF16)
        u, *qkv_by_dil = _inproj(x2d, g_mix[layer][None], w_in_scaled)
        o_list, lse_list = [], []
        for (window, dil), qkv_d, (rq, nq) in zip(DILATED_BRANCHES, qkv_by_dil, ATTN_STEP):
            o_d, lse_d = _attn_branch(qkv_d, _bias_tables(rel_bias, window, dil), b, s, dil, rq, nq)
            o_list.append(o_d)
            lse_list.append(lse_d)
        brt = jnp.broadcast_to(b_router[layer][:, None], (N_EXPERTS, 128))
        x1, h2, pos, gates, pc = _mixout(
            x2d, u, o_list, lse_list, w_pool[layer].astype(BF16), pool_scale[layer][None],
            w_out[layer].astype(BF16), g_ffn[layer][None], w_router[layer].T, brt, s)

        pc16 = pc[:, :, 0].astype(I32)
        pcnt, loff, gstart, tails, seg, ebase, seg_end = _routing_tables(pc16)
        n_rows = t * TOP_K + N_EXPERTS * (t // TOKEN_TILE) * ROW_ALIGN + N_EXPERTS * (EXPERT_TILE - 1)
        n_rows = (n_rows + EXPERT_TILE - 1) // EXPERT_TILE * EXPERT_TILE
        n_tiles = n_rows // EXPERT_TILE
        tile_start = jnp.arange(n_tiles, dtype=I32) * EXPERT_TILE
        tile_expert = jnp.minimum(jnp.sum(seg_end[None, :] <= tile_start[:, None], axis=1), N_EXPERTS - 1).astype(I32)
        n_used = (seg_end[-1] // EXPERT_TILE).astype(I32)[None]
        xs = _dispatch(loff, pcnt, gstart, tails, h2, pos, n_rows)
        has_rows = seg > 0
        experts = jnp.arange(N_EXPERTS, dtype=I32)
        stage_slot = ((jnp.cumsum(has_rows) - 1) % 2).astype(I32)
        next_expert = jnp.min(jnp.where(has_rows[None, :] & (experts[None, :] > experts[:, None]),
                                        experts[None, :], N_EXPERTS), axis=1).astype(I32)
        rows_left = seg[tile_expert] - (tile_start - ebase[tile_expert])
        tile_full = (rows_left > EXPERT_HALF).astype(I32)
        ys = _experts(tile_expert, n_used, stage_slot, next_expert, tile_full, xs, w_gate_up[layer],
                      b_gate_up[layer][:, None, :], w_down[layer], b_down[layer][:, None, :])

        post = pos.transpose(0, 2, 1).reshape(t, 8)
        gatet = gates.transpose(0, 2, 1).reshape(t, 8)
        assert layer == w_in.shape[0] - 1, "single-layer pipeline: the final norm is fused into combine"
        x2d = _combine(loff, pcnt, gstart, x1, post, gatet, p[layer].reshape(t, PLE_DIM), g_ple[layer][None],
                       w_ple_gate[layer].astype(BF16), w_ple_proj[layer].astype(BF16), g_final[None], ys)
    return x2d.reshape(b, s, d)
```

```python
import functools
import math

import jax
import jax.numpy as jnp
from jax import lax
from jax.experimental import pallas as pl
from jax.experimental.pallas import tpu as pltpu

F32 = jnp.float32
BF16 = jnp.bfloat16
I32 = jnp.int32

D_MODEL = 1024
POOL_WIDTH = 512
POOL_WINDOWS = (2, 4, 8, 16)
POOL_GROUP_DIM = 128
ATTN_WIDTH = 512
QKV_WIDTH = 3 * ATTN_WIDTH
HEAD_DIM = 64
N_HEADS = 8
DILATED_BRANCHES = ((128, 1), (512, 4), (2048, 16))
ATTN_BLOCK = 128
N_REL_BUCKETS = 32
REL_MAX_EXACT = 16
REL_MAX_DISTANCE = 2048
N_EXPERTS = 32
TOP_K = 4
D_EXPERT = 1024
SWIGLU_LIMIT = 7.0
SWIGLU_ALPHA = 1.702
PLE_DIM = 256
NORM_EPS = 1e-6
NEG_INF = -1e30
LOG2_E = math.log2(math.e)
QUERY_SCALE = LOG2_E / math.sqrt(HEAD_DIM)

LANES = 128
POOL_HALO = 16
TOKEN_TILE = 512
ROW_ALIGN = 16
EXPERT_TILE = 512
EXPERT_CHUNK = 256
LOCAL_ROWS = TOKEN_TILE * TOP_K + N_EXPERTS * ROW_ALIGN
SORT_BLOCK = 256
GATHER_BLOCK = 512
ATTN_STEP = ((1, 8), (1, 8), (4, 2))
VMEM_LIMIT = 56 * 1024 * 1024


def _rms(x, g):
    return x * lax.rsqrt(jnp.mean(x * x, axis=-1, keepdims=True) + NORM_EPS) * g


def _inproj_kernel(x_ref, g_ref, w_ref, u_ref, nat_ref, d4_ref, d16_ref, z_scr, g_scr):
    h = _rms(x_ref[...], g_ref[...]).astype(BF16)
    part = ATTN_WIDTH
    slabs = part // LANES

    def project(i):
        return jnp.dot(h, w_ref[:, pl.ds(i * part, part)], preferred_element_type=F32)

    def regroup(z, i):
        col0 = (i - 1) * part
        nat_ref[:, pl.ds(col0, part)] = z.astype(BF16)
        for c in range(slabs):
            slab = (i - 1) * slabs + c
            cols = col0 + c * LANES
            z_scr[slab] = z[:, c * LANES:(c + 1) * LANES]
            for rl in range(4):
                grp = z_scr[slab, pl.ds(rl, TOKEN_TILE // 4, stride=4), :]
                d4_ref[:, pl.ds(rl * QKV_WIDTH + cols, LANES)] = grp.astype(BF16)
                g_scr[rl * n_slabs + slab] = grp
            for rl in range(4):
                for rh in range(4):
                    d16_ref[:, pl.ds((4 * rh + rl) * QKV_WIDTH + cols, LANES)] = (
                        g_scr[rl * n_slabs + slab, pl.ds(rh, TOKEN_TILE // 16, stride=4), :].astype(BF16))

    assert POOL_WIDTH == part and QKV_WIDTH == 3 * part and [d for _, d in DILATED_BRANCHES] == [1, 4, 16]
    n_slabs = QKV_WIDTH // LANES
    z_prev = project(0)
    for i in range(1, 4):
        z_next = project(i)
        if i == 1:
            u_ref[...] = z_prev
        else:
            regroup(z_prev, i - 1)
        z_prev = z_next
    regroup(z_prev, 3)


def _inproj(x2d, g, w_bf16):
    t = x2d.shape[0]
    in_w = w_bf16.shape[1]
    tt = TOKEN_TILE
    return pl.pallas_call(
        _inproj_kernel,
        grid=(t // tt,),
        in_specs=[
            pl.BlockSpec((tt, D_MODEL), lambda i: (i, 0)),
            pl.BlockSpec((1, D_MODEL), lambda i: (0, 0)),
            pl.BlockSpec((D_MODEL, in_w), lambda i: (0, 0)),
        ],
        out_specs=[
            pl.BlockSpec((tt, POOL_WIDTH), lambda i: (i, 0)),
            pl.BlockSpec((tt, QKV_WIDTH), lambda i: (i, 0)),
            pl.BlockSpec((tt // 4, 4 * QKV_WIDTH), lambda i: (i, 0)),
            pl.BlockSpec((tt // 16, 16 * QKV_WIDTH), lambda i: (i, 0)),
        ],
        out_shape=[
            jax.ShapeDtypeStruct((t, POOL_WIDTH), F32),
            jax.ShapeDtypeStruct((t, QKV_WIDTH), BF16),
            jax.ShapeDtypeStruct((t // 4, 4 * QKV_WIDTH), BF16),
            jax.ShapeDtypeStruct((t // 16, 16 * QKV_WIDTH), BF16),
        ],
        scratch_shapes=[pltpu.VMEM((QKV_WIDTH // LANES, tt, LANES), F32),
                        pltpu.VMEM((4 * QKV_WIDTH // LANES, tt // 4, LANES), F32)],
        compiler_params=pltpu.CompilerParams(
            dimension_semantics=("parallel",), vmem_limit_bytes=VMEM_LIMIT),
        name="inproj",
    )(x2d, g, w_bf16)


def _attn_kernel(main_ref, prev_ref, tab_ref, o_ref, lse_ref, k_scr, vt_scr, *, rq, nq):
    blk = ATTN_BLOCK
    first_group = pl.program_id(2) == 0
    nt = (((1,), (1,)), ((), ()))

    for ri in range(rq):
        base = ri * QKV_WIDTH
        k_scr[0:blk, :] = prev_ref[:, pl.ds(base + ATTN_WIDTH, ATTN_WIDTH)]
        k_scr[blk:, :] = main_ref[:, pl.ds(base + ATTN_WIDTH, ATTN_WIDTH)]
        vt_scr[:, 0:blk] = prev_ref[:, pl.ds(base + 2 * ATTN_WIDTH, ATTN_WIDTH)].astype(F32).T.astype(BF16)
        for j in range(nq):
            vt_scr[:, pl.ds((j + 1) * blk, blk)] = (
                main_ref[pl.ds(j * blk, blk), pl.ds(base + 2 * ATTN_WIDTH, ATTN_WIDTH)].astype(F32).T.astype(BF16))
        for j in range(nq):
            rows = pl.ds(j * blk, blk)
            keys = pl.ds(j * blk, 2 * blk)
            scores = []
            for h in range(N_HEADS):
                q = main_ref[rows, pl.ds(base + h * HEAD_DIM, HEAD_DIM)]
                k2 = k_scr[keys, pl.ds(h * HEAD_DIM, HEAD_DIM)]
                scores.append(lax.dot_general(k2, q, nt, preferred_element_type=F32))
            probs, inv_l, lse_parts = [], [], []
            for h in range(N_HEADS):
                s = scores[h] + tab_ref[h]
                if j == 0:
                    pen = jnp.where(first_group, NEG_INF, 0.0).astype(F32)
                    s = jnp.concatenate([s[:blk] + pen, s[blk:]], axis=0)
                m = jnp.max(s, axis=0, keepdims=True)
                p = jnp.exp2(s - m)
                l = jnp.sum(p, axis=0, keepdims=True)
                probs.append(p.astype(BF16))
                inv_l.append(1.0 / l)
                lse_parts.append(m * math.log(2.0) + jnp.log(l))
            o_parts = []
            for h in range(N_HEADS):
                vt2 = vt_scr[pl.ds(h * HEAD_DIM, HEAD_DIM), keys]
                o_parts.append(jnp.dot(vt2, probs[h], preferred_element_type=F32) * inv_l[h])
            o_t = jnp.concatenate(o_parts, axis=0)
            o_ref[rows, pl.ds(ri * ATTN_WIDTH, ATTN_WIDTH)] = o_t.T.astype(BF16)
            lse_t = jnp.concatenate(lse_parts + [jnp.zeros((LANES - N_HEADS, blk), F32)], axis=0)
            lse_ref[rows, pl.ds(ri * LANES, LANES)] = lse_t.T


def _attn_branch(qkv_d, tab, batch, seq, dil, rq, nq):
    sub = seq // dil
    nb = sub // ATTN_BLOCK
    groups = nb // nq
    return pl.pallas_call(
        functools.partial(_attn_kernel, rq=rq, nq=nq),
        grid=(batch, dil // rq, groups),
        in_specs=[
            pl.BlockSpec((nq * ATTN_BLOCK, rq * QKV_WIDTH), lambda b, r, g: (b * groups + g, r)),
            pl.BlockSpec((ATTN_BLOCK, rq * QKV_WIDTH), lambda b, r, g: (b * nb + jnp.maximum(g * nq - 1, 0), r)),
            pl.BlockSpec((N_HEADS, 2 * ATTN_BLOCK, ATTN_BLOCK), lambda b, r, g: (0, 0, 0)),
        ],
        out_specs=[
            pl.BlockSpec((nq * ATTN_BLOCK, rq * ATTN_WIDTH), lambda b, r, g: (b * groups + g, r)),
            pl.BlockSpec((nq * ATTN_BLOCK, rq * LANES), lambda b, r, g: (b * groups + g, r)),
        ],
        out_shape=[
            jax.ShapeDtypeStruct((batch * sub, dil * ATTN_WIDTH), BF16),
            jax.ShapeDtypeStruct((batch * sub, dil * LANES), F32),
        ],
        scratch_shapes=[
            pltpu.VMEM(((nq + 1) * ATTN_BLOCK, ATTN_WIDTH), BF16),
            pltpu.VMEM((ATTN_WIDTH, (nq + 1) * ATTN_BLOCK), BF16),
        ],
        compiler_params=pltpu.CompilerParams(
            dimension_semantics=("parallel", "parallel", "parallel"), vmem_limit_bytes=VMEM_LIMIT),
        name="attn",
    )(qkv_d, qkv_d, tab)


def _t5_bucket(dist):
    n = jnp.maximum(dist, 1).astype(F32)
    large = REL_MAX_EXACT + (jnp.log(n / REL_MAX_EXACT) / math.log(REL_MAX_DISTANCE / REL_MAX_EXACT)
                             * (N_REL_BUCKETS - REL_MAX_EXACT)).astype(I32)
    large = jnp.minimum(large, N_REL_BUCKETS - 1)
    return jnp.where(dist < REL_MAX_EXACT, dist, large)


def _shifted_rows(w, n):
    lead = w.shape[:-1]
    width = w.shape[-1]
    flat = jnp.tile(w, (1,) * len(lead) + (n + 1,))[..., :n * (width + 1)]
    return flat.reshape(lead + (n, width + 1))[..., :n]


def _bias_tables(rel_bias):
    blk = ATTN_BLOCK
    assert all(window // dil == blk for window, dil in DILATED_BRANCHES)
    n_br = len(DILATED_BRANCHES)
    dist = jnp.stack([jnp.arange(blk + 1) * dil for _, dil in DILATED_BRANCHES])
    f = jnp.swapaxes(rel_bias[_t5_bucket(dist)], 1, 2).astype(F32)
    neg = jnp.full((n_br, N_HEADS, blk - 1), NEG_INF, F32)
    neg1 = jnp.full((n_br, N_HEADS, 1), NEG_INF, F32)
    w_prev = jnp.concatenate([neg, f[..., :0:-1], neg1], axis=-1)
    prev = _shifted_rows(w_prev, blk)[..., ::-1]
    w_cur = jnp.concatenate([neg, f[..., :blk], neg1], axis=-1)
    cur = _shifted_rows(w_cur, blk)[..., ::-1, :]
    return jnp.concatenate([prev, cur], axis=-2) * LOG2_E


def _mixout_kernel(x_ref, u_ref, uh_ref, o1_ref, o4_ref, o16_ref, l1_ref, l4_ref, l16_ref,
                   wpool_ref, pscale_ref, wout_ref, gffn_ref, wrt_ref, brt_ref, spread_ref,
                   x1_ref, h2_ref, pos_ref, gate_ref, pc_ref,
                   ext_scr, mix_scr, o_scr, l_scr, *, tiles_per_seq):
    i = pl.program_id(0)
    tt = TOKEN_TILE
    seq_tile = i % tiles_per_seq

    halo = uh_ref[...]
    ext_scr[0:POOL_HALO, :] = jnp.where(seq_tile == 0, jnp.zeros_like(halo), halo)
    ext_scr[POOL_HALO:, :] = u_ref[...]
    tpos = seq_tile * tt + lax.broadcasted_iota(I32, (tt, 1), 0)
    for gi, w in enumerate(POOL_WINDOWS):
        cols = pl.ds(gi * POOL_GROUP_DIM, POOL_GROUP_DIM)
        run = ext_scr[:, cols]
        shift = 1
        while shift < w:
            run = run + pltpu.roll(run, shift, axis=0)
            shift *= 2
        assert shift == w and w - 1 < POOL_HALO
        acc = run[POOL_HALO:]
        tok = ext_scr[pl.ds(POOL_HALO, tt), cols]
        cnt = jnp.minimum(tpos + 1, w).astype(F32)
        pooled = (acc / cnt - tok).astype(BF16)
        mixed = jnp.dot(pooled, wpool_ref[gi], preferred_element_type=F32) * pscale_ref[:, cols]
        mix_scr[:, cols] = mixed.astype(BF16)

    o_slabs = ATTN_WIDTH // LANES
    for bi, (dil, o_ref, l_ref) in enumerate(((4, o4_ref, l4_ref), (16, o16_ref, l16_ref))):
        rows = tt // dil
        for r in range(dil):
            l_scr[bi, pl.ds(r, rows, stride=dil), :] = l_ref[:, pl.ds(r * LANES, LANES)]
            for c in range(o_slabs):
                o_scr[bi * o_slabs + c, pl.ds(r, rows, stride=dil), :] = (
                    o_ref[:, pl.ds(r * ATTN_WIDTH + c * LANES, LANES)].astype(F32))
    l1, l4, l16 = l1_ref[...], l_scr[0], l_scr[1]
    lm = jnp.maximum(jnp.maximum(l1, l4), l16)
    e1, e4, e16 = jnp.exp(l1 - lm), jnp.exp(l4 - lm), jnp.exp(l16 - lm)
    inv = 1.0 / (e1 + e4 + e16)

    def per_feature(w):
        hi = w.astype(BF16)
        lo = (w - hi.astype(F32)).astype(BF16)
        return (jnp.dot(hi, spread_ref[...], preferred_element_type=F32)
                + jnp.dot(lo, spread_ref[...], preferred_element_type=F32))

    w1, w4, w16 = per_feature(e1 * inv), per_feature(e4 * inv), per_feature(e16 * inv)
    for c in range(o_slabs):
        sl = pl.ds(c * LANES, LANES)
        cs = slice(c * LANES, (c + 1) * LANES)
        o = w1[:, cs] * o1_ref[:, sl].astype(F32) + w4[:, cs] * o_scr[c] + w16[:, cs] * o_scr[o_slabs + c]
        mix_scr[:, pl.ds(POOL_WIDTH + c * LANES, LANES)] = o.astype(BF16)

    x1 = x_ref[...] + jnp.dot(mix_scr[...], wout_ref[...], preferred_element_type=F32)
    x1_ref[...] = x1

    h2 = _rms(x1, gffn_ref[...])
    h2_ref[...] = h2.astype(BF16)
    nt = (((1,), (1,)), ((), ()))
    wr = wrt_ref[...]
    wr_hi = wr.astype(BF16)
    wr_lo = (wr - wr_hi.astype(F32)).astype(BF16)
    h2_hi = h2.astype(BF16)
    h2_lo = (h2 - h2_hi.astype(F32)).astype(BF16)
    logits = (lax.dot_general(wr_hi, h2_hi, nt, preferred_element_type=F32)
              + lax.dot_general(wr_hi, h2_lo, nt, preferred_element_type=F32)
              + lax.dot_general(wr_lo, h2_hi, nt, preferred_element_type=F32))
    logits = logits + brt_ref[:, 0:1]
    eio = lax.broadcasted_iota(I32, (N_EXPERTS, tt), 0)
    sel, val = [], []
    for _ in range(TOP_K):
        m = jnp.max(logits, axis=0, keepdims=True)
        idx = jnp.min(jnp.where(logits == m, eio, N_EXPERTS), axis=0, keepdims=True)
        sel.append(idx)
        val.append(m)
        logits = jnp.where(eio == idx, -jnp.inf, logits)
    ex = [jnp.exp(v - val[0]) for v in val]
    den = ex[0] + ex[1] + ex[2] + ex[3]
    gates = [e / den for e in ex]

    hot = [(eio == s) for s in sel]
    onehot = (hot[0] | hot[1] | hot[2] | hot[3]).astype(F32)
    ti = lax.broadcasted_iota(I32, (tt, tt), 0)
    tj = lax.broadcasted_iota(I32, (tt, tt), 1)
    before = (ti < tj).astype(BF16)
    rank = jnp.dot(onehot.astype(BF16), before, preferred_element_type=F32)
    cnt = jnp.sum(onehot, axis=1, keepdims=True)
    pc = jnp.maximum(jnp.floor((cnt + (ROW_ALIGN - 1)) * (1.0 / ROW_ALIGN)), 1.0)
    pcb = jnp.broadcast_to(pc, (N_EXPERTS, 128))
    pc_ref[0] = pcb
    pcm = jnp.concatenate([pcb, jnp.zeros((128 - N_EXPERTS, 128), F32)], axis=0).astype(BF16)
    li = lax.broadcasted_iota(I32, (N_EXPERTS, 128), 0)
    lj = lax.broadcasted_iota(I32, (N_EXPERTS, 128), 1)
    lower = (lj < li).astype(BF16)
    off = jnp.dot(lower, pcm, preferred_element_type=F32)[:, 0:1] * float(ROW_ALIGN)
    where_to = off + rank
    zero_i = jnp.zeros((1, tt), I32)
    zero_f = jnp.zeros((1, tt), F32)
    pos_rows = [jnp.sum(jnp.where(hk, where_to, 0.0), axis=0, keepdims=True).astype(I32) for hk in hot]
    pos_ref[0] = jnp.concatenate(pos_rows + [zero_i] * (8 - TOP_K), axis=0)
    gate_ref[0] = jnp.concatenate(gates + [zero_f] * (8 - TOP_K), axis=0)


def _mixout(x2d, u, o_list, lse_list, wpool, pscale, wout, gffn, wrt, brt, seq):
    t = x2d.shape[0]
    tt = TOKEN_TILE
    n_tt = t // tt
    halo_blocks = tt // POOL_HALO
    row = lambda i: (i, 0)
    const2 = lambda i: (0, 0)
    dils = [dil for _, dil in DILATED_BRANCHES]
    spread = (jnp.arange(LANES)[:, None] == jnp.arange(ATTN_WIDTH)[None, :] // HEAD_DIM).astype(BF16)
    return pl.pallas_call(
        functools.partial(_mixout_kernel, tiles_per_seq=seq // tt),
        grid=(n_tt,),
        in_specs=[
            pl.BlockSpec((tt, D_MODEL), row),
            pl.BlockSpec((tt, POOL_WIDTH), row),
            pl.BlockSpec((POOL_HALO, POOL_WIDTH), lambda i: (jnp.maximum(i * halo_blocks - 1, 0), 0)),
            *[pl.BlockSpec((tt // dil, dil * ATTN_WIDTH), row) for dil in dils],
            *[pl.BlockSpec((tt // dil, dil * LANES), row) for dil in dils],
            pl.BlockSpec((len(POOL_WINDOWS), POOL_GROUP_DIM, POOL_GROUP_DIM), lambda i: (0, 0, 0)),
            pl.BlockSpec((1, POOL_WIDTH), const2),
            pl.BlockSpec((D_MODEL, D_MODEL), const2),
            pl.BlockSpec((1, D_MODEL), const2),
            pl.BlockSpec((N_EXPERTS, D_MODEL), const2),
            pl.BlockSpec((N_EXPERTS, 128), const2),
            pl.BlockSpec((LANES, ATTN_WIDTH), const2),
        ],
        out_specs=[
            pl.BlockSpec((tt, D_MODEL), row),
            pl.BlockSpec((tt, D_MODEL), row),
            pl.BlockSpec((1, 8, tt), lambda i: (i, 0, 0)),
            pl.BlockSpec((1, 8, tt), lambda i: (i, 0, 0)),
            pl.BlockSpec((1, N_EXPERTS, 128), lambda i: (i, 0, 0)),
        ],
        out_shape=[
            jax.ShapeDtypeStruct((t, D_MODEL), F32),
            jax.ShapeDtypeStruct((t, D_MODEL), BF16),
            jax.ShapeDtypeStruct((n_tt, 8, tt), I32),
            jax.ShapeDtypeStruct((n_tt, 8, tt), F32),
            jax.ShapeDtypeStruct((n_tt, N_EXPERTS, 128), F32),
        ],
        scratch_shapes=[
            pltpu.VMEM((tt + POOL_HALO, POOL_WIDTH), F32),
            pltpu.VMEM((tt, D_MODEL), BF16),
            pltpu.VMEM((2 * ATTN_WIDTH // LANES, tt, LANES), F32),
            pltpu.VMEM((2, tt, LANES), F32),
        ],
        compiler_params=pltpu.CompilerParams(
            dimension_semantics=("parallel",), vmem_limit_bytes=VMEM_LIMIT),
        name="mixout",
    )(x2d, u, u, *o_list, *lse_list, wpool, pscale, wout, gffn, wrt, brt, spread)


def _chunk_copies(loff_s, pcnt_s, gstart_s, tile, local_ref, global_ref, sems, to_global):
    slot = tile % 2
    out = []
    for e in range(N_EXPERTS):
        n = pl.multiple_of(pcnt_s[tile, e], ROW_ALIGN)
        lo = pl.multiple_of(loff_s[tile, e], ROW_ALIGN)
        go = pl.multiple_of(gstart_s[tile, e], ROW_ALIGN)
        loc = local_ref.at[slot, pl.ds(lo, n)]
        glo = global_ref.at[pl.ds(go, n)]
        src, dst = (loc, glo) if to_global else (glo, loc)
        out.append(pltpu.make_async_copy(src, dst, sems.at[slot]))
    return out


def _start_all(copies):
    for cp in copies:
        cp.start()


def _wait_chunks(loff_s, pcnt_s, tile, local_ref, global_ref, sems, to_global):
    slot = tile % 2
    rows = pl.multiple_of(loff_s[tile, N_EXPERTS - 1] + pcnt_s[tile, N_EXPERTS - 1], ROW_ALIGN)
    loc = local_ref.at[slot, pl.ds(0, rows)]
    glo = global_ref.at[pl.ds(0, rows)]
    src, dst = (loc, glo) if to_global else (glo, loc)
    pltpu.make_async_copy(src, dst, sems.at[slot]).wait()


def _dispatch_kernel(loff_s, pcnt_s, gstart_s, tail_s, h2_ref, pos_ref, xs_ref, loc_scr, zero_scr, sem, tail_sem,
                     spare_sem):
    i = pl.program_id(0)
    tt = TOKEN_TILE

    @pl.when(i == 0)
    def _():
        zero_scr[...] = jnp.zeros_like(zero_scr)
        tails = []
        for e in range(N_EXPERTS):
            n = pl.multiple_of(tail_s[1, e], ROW_ALIGN)
            start = pl.multiple_of(tail_s[0, e], ROW_ALIGN)
            tails.append((n > 0, pltpu.make_async_copy(zero_scr.at[pl.ds(0, n)], xs_ref.at[pl.ds(start, n)], tail_sem)))
        for cond, cp in tails:
            pl.when(cond)(cp.start)
        for cond, cp in tails:
            pl.when(cond)(cp.wait)

    def spare_tile(j):
        return pltpu.make_async_copy(
            zero_scr, xs_ref.at[pl.ds(pl.multiple_of(j * EXPERT_TILE, EXPERT_TILE), EXPERT_TILE)], spare_sem)

    n_tiles = xs_ref.shape[0] // EXPERT_TILE

    @pl.when(i == 0)
    def _():
        lax.fori_loop(tail_s[2, 0], n_tiles, lambda j, c: (spare_tile(j).start(), c)[1], 0)

    @pl.when(i == pl.num_programs(0) - 1)
    def _():
        lax.fori_loop(tail_s[2, 0], n_tiles, lambda j, c: (spare_tile(j).wait(), c)[1], 0)

    pos = pos_ref[0]
    h2 = h2_ref[...]
    slot = i % 2
    used_rows = loff_s[i, N_EXPERTS - 1] + pcnt_s[i, N_EXPERTS - 1]
    def one_hot(jb):
        jio = lax.broadcasted_iota(I32, (SORT_BLOCK, tt), 0) + jb * SORT_BLOCK
        hit = (jio == pos[0:1]) | (jio == pos[1:2]) | (jio == pos[2:3]) | (jio == pos[3:4])
        return jnp.where(hit, 1.0, 0.0).astype(BF16)

    def sort_block(jb, onehot):
        loc_scr[slot, pl.ds(jb * SORT_BLOCK, SORT_BLOCK), :] = jnp.dot(
            onehot, h2, preferred_element_type=F32).astype(BF16)

    n_sure = TOKEN_TILE * TOP_K // SORT_BLOCK
    onehot = one_hot(0)
    for jb in range(n_sure):
        nxt = one_hot(jb + 1) if jb + 1 < n_sure else None
        sort_block(jb, onehot)
        onehot = nxt
    for jb in range(n_sure, LOCAL_ROWS // SORT_BLOCK):
        pl.when(jb * SORT_BLOCK < used_rows)(lambda jb=jb: sort_block(jb, one_hot(jb)))

    @pl.when(i > 0)
    def _():
        _wait_chunks(loff_s, pcnt_s, i - 1, loc_scr, xs_ref, sem, True)

    _start_all(_chunk_copies(loff_s, pcnt_s, gstart_s, i, loc_scr, xs_ref, sem, True))

    @pl.when(i == pl.num_programs(0) - 1)
    def _():
        _wait_chunks(loff_s, pcnt_s, i, loc_scr, xs_ref, sem, True)


def _dispatch(loff, pcnt, gstart, tails, h2, pos, n_rows):
    t = h2.shape[0]
    tt = TOKEN_TILE
    grid_spec = pltpu.PrefetchScalarGridSpec(
        num_scalar_prefetch=4,
        grid=(t // tt,),
        in_specs=[
            pl.BlockSpec((tt, D_MODEL), lambda i, *_: (i, 0)),
            pl.BlockSpec((1, 8, tt), lambda i, *_: (i, 0, 0)),
        ],
        out_specs=pl.BlockSpec(memory_space=pl.ANY),
        scratch_shapes=[
            pltpu.VMEM((2, LOCAL_ROWS, D_MODEL), BF16),
            pltpu.VMEM((EXPERT_TILE, D_MODEL), BF16),
            pltpu.SemaphoreType.DMA((2,)),
            pltpu.SemaphoreType.DMA(()),
            pltpu.SemaphoreType.DMA(()),
        ],
    )
    return pl.pallas_call(
        _dispatch_kernel,
        grid_spec=grid_spec,
        out_shape=jax.ShapeDtypeStruct((n_rows, D_MODEL), BF16),
        compiler_params=pltpu.CompilerParams(
            dimension_semantics=("arbitrary",), vmem_limit_bytes=VMEM_LIMIT),
        name="dispatch",
    )(loff, pcnt, gstart, tails, h2, pos)


def _expert_kernel(te_s, nu_s, slot_s, next_s, xs_ref, wgu_hbm, bgu_ref, wd_hbm, bd_ref, ys_ref,
                   wgu_stage, wd_stage, wgu_bf, wd_bf, act_scr, wsem):
    i = pl.program_id(0)
    live = i < nu_s[0]
    expert = te_s[i]
    new_expert = (i == 0) | (expert != te_s[jnp.maximum(i - 1, 0)])

    def weight_copies(e, slot):
        return (pltpu.make_async_copy(wgu_hbm.at[e], wgu_stage.at[slot], wsem.at[0, slot]),
                pltpu.make_async_copy(wd_hbm.at[e], wd_stage.at[slot], wsem.at[1, slot]))

    @pl.when(live & new_expert)
    def _():
        slot = slot_s[expert]

        @pl.when(i == 0)
        def _():
            for cp in weight_copies(expert, slot):
                cp.start()

        for cp in weight_copies(expert, slot):
            cp.wait()
        wgu_bf[...] = wgu_stage[slot].astype(BF16)
        wd_bf[...] = wd_stage[slot].astype(BF16)
        upcoming = next_s[expert]

        @pl.when(upcoming < N_EXPERTS)
        def _():
            for cp in weight_copies(upcoming, 1 - slot):
                cp.start()

    @pl.when(live)
    def _():
        x = xs_ref[...]
        n_chunks = D_EXPERT // EXPERT_CHUNK

        def gate_up(c):
            gc = pl.ds(c * EXPERT_CHUNK, EXPERT_CHUNK)
            uc = pl.ds(D_EXPERT + c * EXPERT_CHUNK, EXPERT_CHUNK)
            return (jnp.dot(x, wgu_bf[:, gc], preferred_element_type=F32) + bgu_ref[0, :, gc],
                    jnp.dot(x, wgu_bf[:, uc], preferred_element_type=F32) + bgu_ref[0, :, uc])

        g, u = gate_up(0)
        for c in range(n_chunks):
            nxt = gate_up(c + 1) if c + 1 < n_chunks else None
            g = jnp.minimum(g, SWIGLU_LIMIT)
            u = jnp.clip(u, -SWIGLU_LIMIT, SWIGLU_LIMIT)
            act_scr[:, pl.ds(c * EXPERT_CHUNK, EXPERT_CHUNK)] = (
                (u + 1.0) * (g * jax.nn.sigmoid(SWIGLU_ALPHA * g))).astype(BF16)
            if nxt is not None:
                g, u = nxt
        y = jnp.dot(act_scr[...], wd_bf[...], preferred_element_type=F32) + bd_ref[0]
        ys_ref[...] = y.astype(BF16)

    @pl.when(jnp.logical_not(live))
    def _():
        ys_ref[...] = jnp.zeros_like(ys_ref)


def _experts(tile_expert, n_used, stage_slot, next_expert, xs, wgu, bgu, wd, bd):
    n_rows = xs.shape[0]
    tm = EXPERT_TILE
    live = lambda i, te, nu: jnp.minimum(i, nu[0] - 1)
    grid_spec = pltpu.PrefetchScalarGridSpec(
        num_scalar_prefetch=4,
        grid=(n_rows // tm,),
        in_specs=[
            pl.BlockSpec((tm, D_MODEL), lambda i, te, nu, *_: (live(i, te, nu), 0)),
            pl.BlockSpec(memory_space=pl.ANY),
            pl.BlockSpec((1, 1, 2 * D_EXPERT), lambda i, te, nu, *_: (te[live(i, te, nu)], 0, 0)),
            pl.BlockSpec(memory_space=pl.ANY),
            pl.BlockSpec((1, 1, D_MODEL), lambda i, te, nu, *_: (te[live(i, te, nu)], 0, 0)),
        ],
        out_specs=pl.BlockSpec((tm, D_MODEL), lambda i, te, nu, *_: (i, 0)),
        scratch_shapes=[
            pltpu.VMEM((2, D_MODEL, 2 * D_EXPERT), F32),
            pltpu.VMEM((2, D_EXPERT, D_MODEL), F32),
            pltpu.VMEM((D_MODEL, 2 * D_EXPERT), BF16),
            pltpu.VMEM((D_EXPERT, D_MODEL), BF16),
            pltpu.VMEM((tm, D_EXPERT), BF16),
            pltpu.SemaphoreType.DMA((2, 2)),
        ],
    )
    return pl.pallas_call(
        _expert_kernel,
        grid_spec=grid_spec,
        out_shape=jax.ShapeDtypeStruct((n_rows, D_MODEL), BF16),
        compiler_params=pltpu.CompilerParams(
            dimension_semantics=("arbitrary",), vmem_limit_bytes=VMEM_LIMIT),
        name="experts",
    )(tile_expert, n_used, stage_slot, next_expert, xs, wgu, bgu, wd, bd)


def _combine_kernel(loff_s, pcnt_s, gstart_s, x1_ref, post_ref, gatet_ref, p_ref, gple_ref, wg_ref, wp_ref,
                    gfin_ref, ys_ref, out_ref, loc_scr, sem):
    i = pl.program_id(0)
    tt = TOKEN_TILE

    @pl.when(i == 0)
    def _():
        loc_scr[...] = jnp.zeros_like(loc_scr)
        _start_all(_chunk_copies(loff_s, pcnt_s, gstart_s, i, loc_scr, ys_ref, sem, False))

    _wait_chunks(loff_s, pcnt_s, i, loc_scr, ys_ref, sem, False)

    @pl.when(i + 1 < pl.num_programs(0))
    def _():
        _start_all(_chunk_copies(loff_s, pcnt_s, gstart_s, i + 1, loc_scr, ys_ref, sem, False))

    slot = i % 2

    post = post_ref[...]
    gatet = gatet_ref[...]

    def gate_block(jb):
        jio = lax.broadcasted_iota(I32, (tt, GATHER_BLOCK), 1) + jb * GATHER_BLOCK
        w = jnp.zeros((tt, GATHER_BLOCK), F32)
        for k in range(TOP_K):
            w = jnp.where(jio == post[:, k:k + 1], gatet[:, k:k + 1], w)
        return w.astype(BF16)

    n_blocks = LOCAL_ROWS // GATHER_BLOCK
    moe = jnp.zeros((tt, D_MODEL), F32)
    w = gate_block(0)
    for jb in range(n_blocks):
        nxt = gate_block(jb + 1) if jb + 1 < n_blocks else None
        y = loc_scr[slot, pl.ds(jb * GATHER_BLOCK, GATHER_BLOCK), :]
        moe = moe + jnp.dot(w, y, preferred_element_type=F32)
        w = nxt

    x2 = x1_ref[...] + moe
    gate = jax.nn.sigmoid(jnp.dot(_rms(x2, gple_ref[...]).astype(BF16), wg_ref[...], preferred_element_type=F32))
    emb = jnp.dot(p_ref[...].astype(BF16), wp_ref[...], preferred_element_type=F32)
    out_ref[...] = _rms(x2 + emb * gate, gfin_ref[...])


def _combine(loff, pcnt, gstart, x1, post, gatet, p2d, gple, wg, wp, gfin, ys):
    t = x1.shape[0]
    tt = TOKEN_TILE
    row = lambda i, *_: (i, 0)
    const2 = lambda i, *_: (0, 0)
    grid_spec = pltpu.PrefetchScalarGridSpec(
        num_scalar_prefetch=3,
        grid=(t // tt,),
        in_specs=[
            pl.BlockSpec((tt, D_MODEL), row),
            pl.BlockSpec((tt, 8), row),
            pl.BlockSpec((tt, 8), row),
            pl.BlockSpec((tt, PLE_DIM), row),
            pl.BlockSpec((1, D_MODEL), const2),
            pl.BlockSpec((D_MODEL, D_MODEL), const2),
            pl.BlockSpec((PLE_DIM, D_MODEL), const2),
            pl.BlockSpec((1, D_MODEL), const2),
            pl.BlockSpec(memory_space=pl.ANY),
        ],
        out_specs=pl.BlockSpec((tt, D_MODEL), row),
        scratch_shapes=[pltpu.VMEM((2, LOCAL_ROWS, D_MODEL), BF16), pltpu.SemaphoreType.DMA((2,))],
    )
    return pl.pallas_call(
        _combine_kernel,
        grid_spec=grid_spec,
        out_shape=jax.ShapeDtypeStruct((t, D_MODEL), F32),
        compiler_params=pltpu.CompilerParams(
            dimension_semantics=("arbitrary",), vmem_limit_bytes=VMEM_LIMIT),
        name="combine",
    )(loff, pcnt, gstart, x1, post, gatet, p2d, gple, wg, wp, gfin, ys)


def _routing_tables(pc16):
    pcnt = pc16 * ROW_ALIGN
    loff = jnp.cumsum(pcnt, axis=1) - pcnt
    seg = jnp.sum(pcnt, axis=0)
    segpad = (seg + EXPERT_TILE - 1) // EXPERT_TILE * EXPERT_TILE
    seg_end = jnp.cumsum(segpad)
    ebase = seg_end - segpad
    gstart = ebase[None, :] + jnp.cumsum(pcnt, axis=0) - pcnt
    tails = jnp.stack([ebase + seg, segpad - seg, jnp.broadcast_to(seg_end[-1] // EXPERT_TILE, seg.shape)])
    return pcnt, loff, gstart, tails, seg_end


def kernel(x, p, g_mix, w_in, w_pool, pool_scale, rel_bias, w_out, g_ffn, w_router, b_router, w_gate_up,
           b_gate_up, w_down, b_down, g_ple, w_ple_gate, w_ple_proj, g_final):
    b, s, d = x.shape
    t = b * s
    x2d = x.reshape(t, d)
    for layer in range(w_in.shape[0]):
        col = jnp.arange(w_in.shape[2])
        is_q = (col >= POOL_WIDTH) & (col < POOL_WIDTH + ATTN_WIDTH)
        w_in_scaled = (w_in[layer] * jnp.where(is_q, QUERY_SCALE, 1.0)).astype(BF16)
        u, *qkv_by_dil = _inproj(x2d, g_mix[layer][None], w_in_scaled)
        o_list, lse_list = [], []
        tabs = _bias_tables(rel_bias)
        for bi, ((_, dil), qkv_d, (rq, nq)) in enumerate(zip(DILATED_BRANCHES, qkv_by_dil, ATTN_STEP)):
            o_d, lse_d = _attn_branch(qkv_d, tabs[bi], b, s, dil, rq, nq)
            o_list.append(o_d)
            lse_list.append(lse_d)
        brt = jnp.broadcast_to(b_router[layer][:, None], (N_EXPERTS, 128))
        x1, h2, pos, gates, pc = _mixout(
            x2d, u, o_list, lse_list, w_pool[layer].astype(BF16), pool_scale[layer][None],
            w_out[layer].astype(BF16), g_ffn[layer][None], w_router[layer].T, brt, s)

        pc16 = pc[:, :, 0].astype(I32)
        pcnt, loff, gstart, tails, seg_end = _routing_tables(pc16)
        n_rows = t * TOP_K + N_EXPERTS * (t // TOKEN_TILE) * ROW_ALIGN + N_EXPERTS * (EXPERT_TILE - 1)
        n_rows = (n_rows + EXPERT_TILE - 1) // EXPERT_TILE * EXPERT_TILE
        n_tiles = n_rows // EXPERT_TILE
        tile_start = jnp.arange(n_tiles, dtype=I32) * EXPERT_TILE
        tile_expert = jnp.minimum(jnp.sum(seg_end[None, :] <= tile_start[:, None], axis=1), N_EXPERTS - 1).astype(I32)
        n_used = (seg_end[-1] // EXPERT_TILE).astype(I32)[None]
        xs = _dispatch(loff, pcnt, gstart, tails, h2, pos, n_rows)
        has_rows = jnp.diff(seg_end, prepend=0) > 0
        experts = jnp.arange(N_EXPERTS, dtype=I32)
        stage_slot = ((jnp.cumsum(has_rows) - 1) % 2).astype(I32)
        next_expert = jnp.min(jnp.where(has_rows[None, :] & (experts[None, :] > experts[:, None]),
                                        experts[None, :], N_EXPERTS), axis=1).astype(I32)
        ys = _experts(tile_expert, n_used, stage_slot, next_expert, xs, w_gate_up[layer],
                      b_gate_up[layer][:, None, :], w_down[layer], b_down[layer][:, None, :])

        post = pos.transpose(0, 2, 1).reshape(t, 8)
        gatet = gates.transpose(0, 2, 1).reshape(t, 8)
        assert layer == w_in.shape[0] - 1, "single-layer pipeline: the final norm is fused into combine"
        x2d = _combine(loff, pcnt, gstart, x1, post, gatet, p[layer].reshape(t, PLE_DIM), g_ple[layer][None],
                       w_ple_gate[layer].astype(BF16), w_ple_proj[layer].astype(BF16), g_final[None], ys)
    return x2d.reshape(b, s, d)
```

```python
import functools
import math

import jax
import jax.numpy as jnp
from jax import lax
from jax.experimental import pallas as pl
from jax.experimental.pallas import tpu as pltpu

F32 = jnp.float32
BF16 = jnp.bfloat16
I32 = jnp.int32

D_MODEL = 1024
POOL_WIDTH = 512
POOL_WINDOWS = (2, 4, 8, 16)
POOL_GROUP_DIM = 128
ATTN_WIDTH = 512
QKV_WIDTH = 3 * ATTN_WIDTH
HEAD_DIM = 64
N_HEADS = 8
DILATED_BRANCHES = ((128, 1), (512, 4), (2048, 16))
ATTN_BLOCK = 128
N_REL_BUCKETS = 32
REL_MAX_EXACT = 16
REL_MAX_DISTANCE = 2048
N_EXPERTS = 32
TOP_K = 4
D_EXPERT = 1024
SWIGLU_LIMIT = 7.0
SWIGLU_ALPHA = 1.702
PLE_DIM = 256
NORM_EPS = 1e-6
NEG_INF = -1e30
LOG2_E = math.log2(math.e)
QUERY_SCALE = LOG2_E / math.sqrt(HEAD_DIM)

LANES = 128
POOL_HALO = 16
TOKEN_TILE = 512
ROW_ALIGN = 16
EXPERT_TILE = 512
EXPERT_CHUNK = 256
LOCAL_ROWS = TOKEN_TILE * TOP_K + N_EXPERTS * ROW_ALIGN
SORT_BLOCK = 256
GATHER_BLOCK = 512
ATTN_STEP = ((1, 8), (1, 8), (4, 2))
VMEM_LIMIT = 56 * 1024 * 1024


def _rms(x, g):
    return x * lax.rsqrt(jnp.mean(x * x, axis=-1, keepdims=True) + NORM_EPS) * g


def _inproj_kernel(x_ref, g_ref, w_ref, u_ref, nat_ref, d4_ref, d16_ref, z_scr, g_scr):
    h = _rms(x_ref[...], g_ref[...]).astype(BF16)
    part = ATTN_WIDTH
    slabs = part // LANES

    def project(i):
        return jnp.dot(h, w_ref[:, pl.ds(i * part, part)], preferred_element_type=F32)

    def regroup(z, i):
        col0 = (i - 1) * part
        nat_ref[:, pl.ds(col0, part)] = z.astype(BF16)
        for c in range(slabs):
            slab = (i - 1) * slabs + c
            cols = col0 + c * LANES
            z_scr[slab] = z[:, c * LANES:(c + 1) * LANES]
            for rl in range(4):
                grp = z_scr[slab, pl.ds(rl, TOKEN_TILE // 4, stride=4), :]
                d4_ref[:, pl.ds(rl * QKV_WIDTH + cols, LANES)] = grp.astype(BF16)
                g_scr[rl * n_slabs + slab] = grp
            for rl in range(4):
                for rh in range(4):
                    d16_ref[:, pl.ds((4 * rh + rl) * QKV_WIDTH + cols, LANES)] = (
                        g_scr[rl * n_slabs + slab, pl.ds(rh, TOKEN_TILE // 16, stride=4), :].astype(BF16))

    assert POOL_WIDTH == part and QKV_WIDTH == 3 * part and [d for _, d in DILATED_BRANCHES] == [1, 4, 16]
    n_slabs = QKV_WIDTH // LANES
    z_prev = project(0)
    for i in range(1, 4):
        z_next = project(i)
        if i == 1:
            u_ref[...] = z_prev
        else:
            regroup(z_prev, i - 1)
        z_prev = z_next
    regroup(z_prev, 3)


def _inproj(x2d, g, w_bf16):
    t = x2d.shape[0]
    in_w = w_bf16.shape[1]
    tt = TOKEN_TILE
    return pl.pallas_call(
        _inproj_kernel,
        grid=(t // tt,),
        in_specs=[
            pl.BlockSpec((tt, D_MODEL), lambda i: (i, 0)),
            pl.BlockSpec((1, D_MODEL), lambda i: (0, 0)),
            pl.BlockSpec((D_MODEL, in_w), lambda i: (0, 0)),
        ],
        out_specs=[
            pl.BlockSpec((tt, POOL_WIDTH), lambda i: (i, 0)),
            pl.BlockSpec((tt, QKV_WIDTH), lambda i: (i, 0)),
            pl.BlockSpec((tt // 4, 4 * QKV_WIDTH), lambda i: (i, 0)),
            pl.BlockSpec((tt // 16, 16 * QKV_WIDTH), lambda i: (i, 0)),
        ],
        out_shape=[
            jax.ShapeDtypeStruct((t, POOL_WIDTH), F32),
            jax.ShapeDtypeStruct((t, QKV_WIDTH), BF16),
            jax.ShapeDtypeStruct((t // 4, 4 * QKV_WIDTH), BF16),
            jax.ShapeDtypeStruct((t // 16, 16 * QKV_WIDTH), BF16),
        ],
        scratch_shapes=[pltpu.VMEM((QKV_WIDTH // LANES, tt, LANES), F32),
                        pltpu.VMEM((4 * QKV_WIDTH // LANES, tt // 4, LANES), F32)],
        compiler_params=pltpu.CompilerParams(
            dimension_semantics=("parallel",), vmem_limit_bytes=VMEM_LIMIT),
        name="inproj",
    )(x2d, g, w_bf16)


def _attn_kernel(main_ref, prev_ref, tab_ref, o_ref, lse_ref, k_scr, vt_scr, *, rq, nq):
    blk = ATTN_BLOCK
    first_group = pl.program_id(2) == 0
    nt = (((1,), (1,)), ((), ()))

    for ri in range(rq):
        base = ri * QKV_WIDTH
        k_scr[0:blk, :] = prev_ref[:, pl.ds(base + ATTN_WIDTH, ATTN_WIDTH)]
        k_scr[blk:, :] = main_ref[:, pl.ds(base + ATTN_WIDTH, ATTN_WIDTH)]
        vt_scr[:, 0:blk] = prev_ref[:, pl.ds(base + 2 * ATTN_WIDTH, ATTN_WIDTH)].astype(F32).T.astype(BF16)
        for j in range(nq):
            vt_scr[:, pl.ds((j + 1) * blk, blk)] = (
                main_ref[pl.ds(j * blk, blk), pl.ds(base + 2 * ATTN_WIDTH, ATTN_WIDTH)].astype(F32).T.astype(BF16))
        for j in range(nq):
            rows = pl.ds(j * blk, blk)
            keys = pl.ds(j * blk, 2 * blk)
            scores = []
            for h in range(N_HEADS):
                q = main_ref[rows, pl.ds(base + h * HEAD_DIM, HEAD_DIM)]
                k2 = k_scr[keys, pl.ds(h * HEAD_DIM, HEAD_DIM)]
                scores.append(lax.dot_general(k2, q, nt, preferred_element_type=F32))
            probs, inv_l, lse_parts = [], [], []
            for h in range(N_HEADS):
                s = scores[h] + tab_ref[h]
                if j == 0:
                    pen = jnp.where(first_group, NEG_INF, 0.0).astype(F32)
                    s = jnp.concatenate([s[:blk] + pen, s[blk:]], axis=0)
                m = jnp.max(s, axis=0, keepdims=True)
                p = jnp.exp2(s - m)
                l = jnp.sum(p, axis=0, keepdims=True)
                probs.append(p.astype(BF16))
                inv_l.append(1.0 / l)
                lse_parts.append(m * math.log(2.0) + jnp.log(l))
            o_parts = []
            for h in range(N_HEADS):
                vt2 = vt_scr[pl.ds(h * HEAD_DIM, HEAD_DIM), keys]
                o_parts.append(jnp.dot(vt2, probs[h], preferred_element_type=F32) * inv_l[h])
            o_t = jnp.concatenate(o_parts, axis=0)
            o_ref[rows, pl.ds(ri * ATTN_WIDTH, ATTN_WIDTH)] = o_t.T.astype(BF16)
            lse_t = jnp.concatenate(lse_parts + [jnp.zeros((LANES - N_HEADS, blk), F32)], axis=0)
            lse_ref[rows, pl.ds(ri * LANES, LANES)] = lse_t.T


def _attn_branch(qkv_d, tab, batch, seq, dil, rq, nq):
    sub = seq // dil
    nb = sub // ATTN_BLOCK
    groups = nb // nq
    return pl.pallas_call(
        functools.partial(_attn_kernel, rq=rq, nq=nq),
        grid=(batch, dil // rq, groups),
        in_specs=[
            pl.BlockSpec((nq * ATTN_BLOCK, rq * QKV_WIDTH), lambda b, r, g: (b * groups + g, r)),
            pl.BlockSpec((ATTN_BLOCK, rq * QKV_WIDTH), lambda b, r, g: (b * nb + jnp.maximum(g * nq - 1, 0), r)),
            pl.BlockSpec((N_HEADS, 2 * ATTN_BLOCK, ATTN_BLOCK), lambda b, r, g: (0, 0, 0)),
        ],
        out_specs=[
            pl.BlockSpec((nq * ATTN_BLOCK, rq * ATTN_WIDTH), lambda b, r, g: (b * groups + g, r)),
            pl.BlockSpec((nq * ATTN_BLOCK, rq * LANES), lambda b, r, g: (b * groups + g, r)),
        ],
        out_shape=[
            jax.ShapeDtypeStruct((batch * sub, dil * ATTN_WIDTH), BF16),
            jax.ShapeDtypeStruct((batch * sub, dil * LANES), F32),
        ],
        scratch_shapes=[
            pltpu.VMEM(((nq + 1) * ATTN_BLOCK, ATTN_WIDTH), BF16),
            pltpu.VMEM((ATTN_WIDTH, (nq + 1) * ATTN_BLOCK), BF16),
        ],
        compiler_params=pltpu.CompilerParams(
            dimension_semantics=("parallel", "parallel", "parallel"), vmem_limit_bytes=VMEM_LIMIT),
        name="attn",
    )(qkv_d, qkv_d, tab)


def _t5_bucket(dist):
    n = jnp.maximum(dist, 1).astype(F32)
    large = REL_MAX_EXACT + (jnp.log(n / REL_MAX_EXACT) / math.log(REL_MAX_DISTANCE / REL_MAX_EXACT)
                             * (N_REL_BUCKETS - REL_MAX_EXACT)).astype(I32)
    large = jnp.minimum(large, N_REL_BUCKETS - 1)
    return jnp.where(dist < REL_MAX_EXACT, dist, large)


def _shifted_rows(w, n):
    lead = w.shape[:-1]
    width = w.shape[-1]
    flat = jnp.tile(w, (1,) * len(lead) + (n + 1,))[..., :n * (width + 1)]
    return flat.reshape(lead + (n, width + 1))[..., :n]


def _bias_tables(rel_bias):
    blk = ATTN_BLOCK
    assert all(window // dil == blk for window, dil in DILATED_BRANCHES)
    n_br = len(DILATED_BRANCHES)
    dist = jnp.stack([jnp.arange(blk + 1) * dil for _, dil in DILATED_BRANCHES])
    f = jnp.swapaxes(rel_bias[_t5_bucket(dist)], 1, 2).astype(F32)
    neg = jnp.full((n_br, N_HEADS, blk - 1), NEG_INF, F32)
    neg1 = jnp.full((n_br, N_HEADS, 1), NEG_INF, F32)
    w_prev = jnp.concatenate([neg, f[..., :0:-1], neg1], axis=-1)
    prev = _shifted_rows(w_prev, blk)[..., ::-1]
    w_cur = jnp.concatenate([neg, f[..., :blk], neg1], axis=-1)
    cur = _shifted_rows(w_cur, blk)[..., ::-1, :]
    return jnp.concatenate([prev, cur], axis=-2) * LOG2_E


def _mixout_kernel(x_ref, u_ref, uh_ref, o1_ref, o4_ref, o16_ref, l1_ref, l4_ref, l16_ref,
                   wpool_ref, pscale_ref, wout_ref, gffn_ref, wrt_ref, brt_ref, spread_ref,
                   x1_ref, h2_ref, pos_ref, gate_ref, pc_ref,
                   ext_scr, mix_scr, o_scr, l_scr, *, tiles_per_seq):
    i = pl.program_id(0)
    tt = TOKEN_TILE
    seq_tile = i % tiles_per_seq

    halo = uh_ref[...]
    ext_scr[0:POOL_HALO, :] = jnp.where(seq_tile == 0, jnp.zeros_like(halo), halo)
    ext_scr[POOL_HALO:, :] = u_ref[...]
    tpos = seq_tile * tt + lax.broadcasted_iota(I32, (tt, 1), 0)
    for gi, w in enumerate(POOL_WINDOWS):
        cols = pl.ds(gi * POOL_GROUP_DIM, POOL_GROUP_DIM)
        run = ext_scr[:, cols]
        shift = 1
        while shift < w:
            run = run + pltpu.roll(run, shift, axis=0)
            shift *= 2
        assert shift == w and w - 1 < POOL_HALO
        acc = run[POOL_HALO:]
        tok = ext_scr[pl.ds(POOL_HALO, tt), cols]
        cnt = jnp.minimum(tpos + 1, w).astype(F32)
        pooled = (acc / cnt - tok).astype(BF16)
        mixed = jnp.dot(pooled, wpool_ref[gi], preferred_element_type=F32) * pscale_ref[:, cols]
        mix_scr[:, cols] = mixed.astype(BF16)

    o_slabs = ATTN_WIDTH // LANES
    for bi, (dil, o_ref, l_ref) in enumerate(((4, o4_ref, l4_ref), (16, o16_ref, l16_ref))):
        rows = tt // dil
        for r in range(dil):
            l_scr[bi, pl.ds(r, rows, stride=dil), :] = l_ref[:, pl.ds(r * LANES, LANES)]
            for c in range(o_slabs):
                o_scr[bi * o_slabs + c, pl.ds(r, rows, stride=dil), :] = (
                    o_ref[:, pl.ds(r * ATTN_WIDTH + c * LANES, LANES)].astype(F32))
    l1, l4, l16 = l1_ref[...], l_scr[0], l_scr[1]
    lm = jnp.maximum(jnp.maximum(l1, l4), l16)
    e1, e4, e16 = jnp.exp(l1 - lm), jnp.exp(l4 - lm), jnp.exp(l16 - lm)
    inv = 1.0 / (e1 + e4 + e16)

    def per_feature(w):
        hi = w.astype(BF16)
        lo = (w - hi.astype(F32)).astype(BF16)
        return (jnp.dot(hi, spread_ref[...], preferred_element_type=F32)
                + jnp.dot(lo, spread_ref[...], preferred_element_type=F32))

    w1, w4, w16 = per_feature(e1 * inv), per_feature(e4 * inv), per_feature(e16 * inv)
    for c in range(o_slabs):
        sl = pl.ds(c * LANES, LANES)
        cs = slice(c * LANES, (c + 1) * LANES)
        o = w1[:, cs] * o1_ref[:, sl].astype(F32) + w4[:, cs] * o_scr[c] + w16[:, cs] * o_scr[o_slabs + c]
        mix_scr[:, pl.ds(POOL_WIDTH + c * LANES, LANES)] = o.astype(BF16)

    x1 = x_ref[...] + jnp.dot(mix_scr[...], wout_ref[...], preferred_element_type=F32)
    x1_ref[...] = x1

    h2 = _rms(x1, gffn_ref[...])
    h2_ref[...] = h2.astype(BF16)
    nt = (((1,), (1,)), ((), ()))
    wr = wrt_ref[...]
    wr_hi = wr.astype(BF16)
    wr_lo = (wr - wr_hi.astype(F32)).astype(BF16)
    h2_hi = h2.astype(BF16)
    h2_lo = (h2 - h2_hi.astype(F32)).astype(BF16)
    logits = (lax.dot_general(wr_hi, h2_hi, nt, preferred_element_type=F32)
              + lax.dot_general(wr_hi, h2_lo, nt, preferred_element_type=F32)
              + lax.dot_general(wr_lo, h2_hi, nt, preferred_element_type=F32))
    logits = logits + brt_ref[:, 0:1]
    eio = lax.broadcasted_iota(I32, (N_EXPERTS, tt), 0)
    sel, val = [], []
    for _ in range(TOP_K):
        m = jnp.max(logits, axis=0, keepdims=True)
        idx = jnp.min(jnp.where(logits == m, eio, N_EXPERTS), axis=0, keepdims=True)
        sel.append(idx)
        val.append(m)
        logits = jnp.where(eio == idx, -jnp.inf, logits)
    ex = [jnp.exp(v - val[0]) for v in val]
    den = ex[0] + ex[1] + ex[2] + ex[3]
    gates = [e / den for e in ex]

    hot = [(eio == s) for s in sel]
    onehot = (hot[0] | hot[1] | hot[2] | hot[3]).astype(F32)
    ti = lax.broadcasted_iota(I32, (tt, tt), 0)
    tj = lax.broadcasted_iota(I32, (tt, tt), 1)
    before = (ti < tj).astype(BF16)
    rank = jnp.dot(onehot.astype(BF16), before, preferred_element_type=F32)
    cnt = jnp.sum(onehot, axis=1, keepdims=True)
    pc = jnp.maximum(jnp.floor((cnt + (ROW_ALIGN - 1)) * (1.0 / ROW_ALIGN)), 1.0)
    pcb = jnp.broadcast_to(pc, (N_EXPERTS, 128))
    pc_ref[0] = pcb
    pcm = jnp.concatenate([pcb, jnp.zeros((128 - N_EXPERTS, 128), F32)], axis=0).astype(BF16)
    li = lax.broadcasted_iota(I32, (N_EXPERTS, 128), 0)
    lj = lax.broadcasted_iota(I32, (N_EXPERTS, 128), 1)
    lower = (lj < li).astype(BF16)
    off = jnp.dot(lower, pcm, preferred_element_type=F32)[:, 0:1] * float(ROW_ALIGN)
    where_to = off + rank
    zero_i = jnp.zeros((1, tt), I32)
    zero_f = jnp.zeros((1, tt), F32)
    pos_rows = [jnp.sum(jnp.where(hk, where_to, 0.0), axis=0, keepdims=True).astype(I32) for hk in hot]
    pos_ref[0] = jnp.concatenate(pos_rows + [zero_i] * (8 - TOP_K), axis=0)
    gate_ref[0] = jnp.concatenate(gates + [zero_f] * (8 - TOP_K), axis=0)


def _mixout(x2d, u, o_list, lse_list, wpool, pscale, wout, gffn, wrt, brt, seq):
    t = x2d.shape[0]
    tt = TOKEN_TILE
    n_tt = t // tt
    halo_blocks = tt // POOL_HALO
    row = lambda i: (i, 0)
    const2 = lambda i: (0, 0)
    dils = [dil for _, dil in DILATED_BRANCHES]
    spread = (jnp.arange(LANES)[:, None] == jnp.arange(ATTN_WIDTH)[None, :] // HEAD_DIM).astype(BF16)
    return pl.pallas_call(
        functools.partial(_mixout_kernel, tiles_per_seq=seq // tt),
        grid=(n_tt,),
        in_specs=[
            pl.BlockSpec((tt, D_MODEL), row),
            pl.BlockSpec((tt, POOL_WIDTH), row),
            pl.BlockSpec((POOL_HALO, POOL_WIDTH), lambda i: (jnp.maximum(i * halo_blocks - 1, 0), 0)),
            *[pl.BlockSpec((tt // dil, dil * ATTN_WIDTH), row) for dil in dils],
            *[pl.BlockSpec((tt // dil, dil * LANES), row) for dil in dils],
            pl.BlockSpec((len(POOL_WINDOWS), POOL_GROUP_DIM, POOL_GROUP_DIM), lambda i: (0, 0, 0)),
            pl.BlockSpec((1, POOL_WIDTH), const2),
            pl.BlockSpec((D_MODEL, D_MODEL), const2),
            pl.BlockSpec((1, D_MODEL), const2),
            pl.BlockSpec((N_EXPERTS, D_MODEL), const2),
            pl.BlockSpec((N_EXPERTS, 128), const2),
            pl.BlockSpec((LANES, ATTN_WIDTH), const2),
        ],
        out_specs=[
            pl.BlockSpec((tt, D_MODEL), row),
            pl.BlockSpec((tt, D_MODEL), row),
            pl.BlockSpec((1, 8, tt), lambda i: (i, 0, 0)),
            pl.BlockSpec((1, 8, tt), lambda i: (i, 0, 0)),
            pl.BlockSpec((1, N_EXPERTS, 128), lambda i: (i, 0, 0)),
        ],
        out_shape=[
            jax.ShapeDtypeStruct((t, D_MODEL), F32),
            jax.ShapeDtypeStruct((t, D_MODEL), BF16),
            jax.ShapeDtypeStruct((n_tt, 8, tt), I32),
            jax.ShapeDtypeStruct((n_tt, 8, tt), F32),
            jax.ShapeDtypeStruct((n_tt, N_EXPERTS, 128), F32),
        ],
        scratch_shapes=[
            pltpu.VMEM((tt + POOL_HALO, POOL_WIDTH), F32),
            pltpu.VMEM((tt, D_MODEL), BF16),
            pltpu.VMEM((2 * ATTN_WIDTH // LANES, tt, LANES), F32),
            pltpu.VMEM((2, tt, LANES), F32),
        ],
        compiler_params=pltpu.CompilerParams(
            dimension_semantics=("parallel",), vmem_limit_bytes=VMEM_LIMIT),
        name="mixout",
    )(x2d, u, u, *o_list, *lse_list, wpool, pscale, wout, gffn, wrt, brt, spread)


def _chunk_copies(loff_s, pcnt_s, gstart_s, tile, local_ref, global_ref, sems, to_global):
    slot = tile % 2
    out = []
    for e in range(N_EXPERTS):
        n = pl.multiple_of(pcnt_s[tile, e], ROW_ALIGN)
        lo = pl.multiple_of(loff_s[tile, e], ROW_ALIGN)
        go = pl.multiple_of(gstart_s[tile, e], ROW_ALIGN)
        loc = local_ref.at[slot, pl.ds(lo, n)]
        glo = global_ref.at[pl.ds(go, n)]
        src, dst = (loc, glo) if to_global else (glo, loc)
        out.append(pltpu.make_async_copy(src, dst, sems.at[slot]))
    return out


def _start_all(copies):
    for cp in copies:
        cp.start()


def _wait_chunks(loff_s, pcnt_s, tile, local_ref, global_ref, sems, to_global):
    slot = tile % 2
    rows = pl.multiple_of(loff_s[tile, N_EXPERTS - 1] + pcnt_s[tile, N_EXPERTS - 1], ROW_ALIGN)
    loc = local_ref.at[slot, pl.ds(0, rows)]
    glo = global_ref.at[pl.ds(0, rows)]
    src, dst = (loc, glo) if to_global else (glo, loc)
    pltpu.make_async_copy(src, dst, sems.at[slot]).wait()


def _dispatch_kernel(loff_s, pcnt_s, gstart_s, tail_s, h2_ref, pos_ref, xs_ref, loc_scr, zero_scr, sem, tail_sem,
                     spare_sem):
    i = pl.program_id(0)
    tt = TOKEN_TILE

    @pl.when(i == 0)
    def _():
        zero_scr[...] = jnp.zeros_like(zero_scr)
        tails = []
        for e in range(N_EXPERTS):
            n = pl.multiple_of(tail_s[1, e], ROW_ALIGN)
            start = pl.multiple_of(tail_s[0, e], ROW_ALIGN)
            tails.append((n > 0, pltpu.make_async_copy(zero_scr.at[pl.ds(0, n)], xs_ref.at[pl.ds(start, n)], tail_sem)))
        for cond, cp in tails:
            pl.when(cond)(cp.start)
        for cond, cp in tails:
            pl.when(cond)(cp.wait)

    def spare_tile(j):
        return pltpu.make_async_copy(
            zero_scr, xs_ref.at[pl.ds(pl.multiple_of(j * EXPERT_TILE, EXPERT_TILE), EXPERT_TILE)], spare_sem)

    n_tiles = xs_ref.shape[0] // EXPERT_TILE

    @pl.when(i == 0)
    def _():
        lax.fori_loop(tail_s[2, 0], n_tiles, lambda j, c: (spare_tile(j).start(), c)[1], 0)

    @pl.when(i == pl.num_programs(0) - 1)
    def _():
        lax.fori_loop(tail_s[2, 0], n_tiles, lambda j, c: (spare_tile(j).wait(), c)[1], 0)

    pos = pos_ref[0]
    h2 = h2_ref[...]
    slot = i % 2
    used_rows = loff_s[i, N_EXPERTS - 1] + pcnt_s[i, N_EXPERTS - 1]
    def one_hot(jb):
        jio = lax.broadcasted_iota(I32, (SORT_BLOCK, tt), 0) + jb * SORT_BLOCK
        hit = (jio == pos[0:1]) | (jio == pos[1:2]) | (jio == pos[2:3]) | (jio == pos[3:4])
        return jnp.where(hit, 1.0, 0.0).astype(BF16)

    def sort_block(jb, onehot):
        loc_scr[slot, pl.ds(jb * SORT_BLOCK, SORT_BLOCK), :] = jnp.dot(
            onehot, h2, preferred_element_type=F32).astype(BF16)

    n_sure = TOKEN_TILE * TOP_K // SORT_BLOCK
    onehot = one_hot(0)
    for jb in range(n_sure):
        nxt = one_hot(jb + 1) if jb + 1 < n_sure else None
        sort_block(jb, onehot)
        onehot = nxt
    for jb in range(n_sure, LOCAL_ROWS // SORT_BLOCK):
        pl.when(jb * SORT_BLOCK < used_rows)(lambda jb=jb: sort_block(jb, one_hot(jb)))

    @pl.when(i > 0)
    def _():
        _wait_chunks(loff_s, pcnt_s, i - 1, loc_scr, xs_ref, sem, True)

    _start_all(_chunk_copies(loff_s, pcnt_s, gstart_s, i, loc_scr, xs_ref, sem, True))

    @pl.when(i == pl.num_programs(0) - 1)
    def _():
        _wait_chunks(loff_s, pcnt_s, i, loc_scr, xs_ref, sem, True)


def _dispatch(loff, pcnt, gstart, tails, h2, pos, n_rows):
    t = h2.shape[0]
    tt = TOKEN_TILE
    grid_spec = pltpu.PrefetchScalarGridSpec(
        num_scalar_prefetch=4,
        grid=(t // tt,),
        in_specs=[
            pl.BlockSpec((tt, D_MODEL), lambda i, *_: (i, 0)),
            pl.BlockSpec((1, 8, tt), lambda i, *_: (i, 0, 0)),
        ],
        out_specs=pl.BlockSpec(memory_space=pl.ANY),
        scratch_shapes=[
            pltpu.VMEM((2, LOCAL_ROWS, D_MODEL), BF16),
            pltpu.VMEM((EXPERT_TILE, D_MODEL), BF16),
            pltpu.SemaphoreType.DMA((2,)),
            pltpu.SemaphoreType.DMA(()),
            pltpu.SemaphoreType.DMA(()),
        ],
    )
    return pl.pallas_call(
        _dispatch_kernel,
        grid_spec=grid_spec,
        out_shape=jax.ShapeDtypeStruct((n_rows, D_MODEL), BF16),
        compiler_params=pltpu.CompilerParams(
            dimension_semantics=("arbitrary",), vmem_limit_bytes=VMEM_LIMIT),
        name="dispatch",
    )(loff, pcnt, gstart, tails, h2, pos)


def _expert_kernel(te_s, nu_s, slot_s, next_s, xs_ref, wgu_hbm, bgu_ref, wd_hbm, bd_ref, ys_ref,
                   wgu_stage, wd_stage, wgu_bf, wd_bf, act_scr, wsem):
    i = pl.program_id(0)
    live = i < nu_s[0]
    expert = te_s[i]
    new_expert = (i == 0) | (expert != te_s[jnp.maximum(i - 1, 0)])

    def weight_copies(e, slot):
        return (pltpu.make_async_copy(wgu_hbm.at[e], wgu_stage.at[slot], wsem.at[0, slot]),
                pltpu.make_async_copy(wd_hbm.at[e], wd_stage.at[slot], wsem.at[1, slot]))

    @pl.when(live & new_expert)
    def _():
        slot = slot_s[expert]

        @pl.when(i == 0)
        def _():
            for cp in weight_copies(expert, slot):
                cp.start()

        for cp in weight_copies(expert, slot):
            cp.wait()
        wgu_bf[...] = wgu_stage[slot].astype(BF16)
        wd_bf[...] = wd_stage[slot].astype(BF16)
        upcoming = next_s[expert]

        @pl.when(upcoming < N_EXPERTS)
        def _():
            for cp in weight_copies(upcoming, 1 - slot):
                cp.start()

    @pl.when(live)
    def _():
        x = xs_ref[...]
        n_chunks = D_EXPERT // EXPERT_CHUNK

        def gate_up(c):
            gc = pl.ds(c * EXPERT_CHUNK, EXPERT_CHUNK)
            uc = pl.ds(D_EXPERT + c * EXPERT_CHUNK, EXPERT_CHUNK)
            return (jnp.dot(x, wgu_bf[:, gc], preferred_element_type=F32) + bgu_ref[0, :, gc],
                    jnp.dot(x, wgu_bf[:, uc], preferred_element_type=F32) + bgu_ref[0, :, uc])

        g, u = gate_up(0)
        for c in range(n_chunks):
            nxt = gate_up(c + 1) if c + 1 < n_chunks else None
            g = jnp.minimum(g, SWIGLU_LIMIT)
            u = jnp.clip(u, -SWIGLU_LIMIT, SWIGLU_LIMIT)
            act_scr[:, pl.ds(c * EXPERT_CHUNK, EXPERT_CHUNK)] = (
                (u + 1.0) * (g * jax.nn.sigmoid(SWIGLU_ALPHA * g))).astype(BF16)
            if nxt is not None:
                g, u = nxt
        y = jnp.dot(act_scr[...], wd_bf[...], preferred_element_type=F32) + bd_ref[0]
        ys_ref[...] = y.astype(BF16)

    @pl.when(jnp.logical_not(live))
    def _():
        ys_ref[...] = jnp.zeros_like(ys_ref)


def _experts(tile_expert, n_used, stage_slot, next_expert, xs, wgu, bgu, wd, bd):
    n_rows = xs.shape[0]
    tm = EXPERT_TILE
    live = lambda i, te, nu: jnp.minimum(i, nu[0] - 1)
    grid_spec = pltpu.PrefetchScalarGridSpec(
        num_scalar_prefetch=4,
        grid=(n_rows // tm,),
        in_specs=[
            pl.BlockSpec((tm, D_MODEL), lambda i, te, nu, *_: (live(i, te, nu), 0)),
            pl.BlockSpec(memory_space=pl.ANY),
            pl.BlockSpec((1, 1, 2 * D_EXPERT), lambda i, te, nu, *_: (te[live(i, te, nu)], 0, 0)),
            pl.BlockSpec(memory_space=pl.ANY),
            pl.BlockSpec((1, 1, D_MODEL), lambda i, te, nu, *_: (te[live(i, te, nu)], 0, 0)),
        ],
        out_specs=pl.BlockSpec((tm, D_MODEL), lambda i, te, nu, *_: (i, 0)),
        scratch_shapes=[
            pltpu.VMEM((2, D_MODEL, 2 * D_EXPERT), F32),
            pltpu.VMEM((2, D_EXPERT, D_MODEL), F32),
            pltpu.VMEM((D_MODEL, 2 * D_EXPERT), BF16),
            pltpu.VMEM((D_EXPERT, D_MODEL), BF16),
            pltpu.VMEM((tm, D_EXPERT), BF16),
            pltpu.SemaphoreType.DMA((2, 2)),
        ],
    )
    return pl.pallas_call(
        _expert_kernel,
        grid_spec=grid_spec,
        out_shape=jax.ShapeDtypeStruct((n_rows, D_MODEL), BF16),
        compiler_params=pltpu.CompilerParams(
            dimension_semantics=("arbitrary",), vmem_limit_bytes=VMEM_LIMIT),
        name="experts",
    )(tile_expert, n_used, stage_slot, next_expert, xs, wgu, bgu, wd, bd)


def _combine_kernel(loff_s, pcnt_s, gstart_s, x1_ref, pos_ref, gate_ref, p_ref, gple_ref, wg_ref, wp_ref,
                    gfin_ref, ys_ref, out_ref, loc_scr, sem):
    i = pl.program_id(0)
    tt = TOKEN_TILE

    @pl.when(i == 0)
    def _():
        loc_scr[...] = jnp.zeros_like(loc_scr)
        _start_all(_chunk_copies(loff_s, pcnt_s, gstart_s, i, loc_scr, ys_ref, sem, False))

    _wait_chunks(loff_s, pcnt_s, i, loc_scr, ys_ref, sem, False)

    @pl.when(i + 1 < pl.num_programs(0))
    def _():
        _start_all(_chunk_copies(loff_s, pcnt_s, gstart_s, i + 1, loc_scr, ys_ref, sem, False))

    slot = i % 2

    def to_columns(rows):
        return jnp.concatenate([rows, jnp.zeros((LANES - rows.shape[0], tt), F32)], axis=0).T

    post = to_columns(pos_ref[0].astype(F32))
    gatet = to_columns(gate_ref[0])

    def gate_block(jb):
        jio = (lax.broadcasted_iota(I32, (tt, GATHER_BLOCK), 1) + jb * GATHER_BLOCK).astype(F32)
        w = jnp.zeros((tt, GATHER_BLOCK), F32)
        for k in range(TOP_K):
            w = jnp.where(jio == post[:, k:k + 1], gatet[:, k:k + 1], w)
        return w.astype(BF16)

    n_blocks = LOCAL_ROWS // GATHER_BLOCK
    moe = jnp.zeros((tt, D_MODEL), F32)
    w = gate_block(0)
    for jb in range(n_blocks):
        nxt = gate_block(jb + 1) if jb + 1 < n_blocks else None
        y = loc_scr[slot, pl.ds(jb * GATHER_BLOCK, GATHER_BLOCK), :]
        moe = moe + jnp.dot(w, y, preferred_element_type=F32)
        w = nxt

    x2 = x1_ref[...] + moe
    gate = jax.nn.sigmoid(jnp.dot(_rms(x2, gple_ref[...]).astype(BF16), wg_ref[...], preferred_element_type=F32))
    emb = jnp.dot(p_ref[...].astype(BF16), wp_ref[...], preferred_element_type=F32)
    out_ref[...] = _rms(x2 + emb * gate, gfin_ref[...])


def _combine(loff, pcnt, gstart, x1, pos, gates, p2d, gple, wg, wp, gfin, ys):
    t = x1.shape[0]
    tt = TOKEN_TILE
    row = lambda i, *_: (i, 0)
    const2 = lambda i, *_: (0, 0)
    grid_spec = pltpu.PrefetchScalarGridSpec(
        num_scalar_prefetch=3,
        grid=(t // tt,),
        in_specs=[
            pl.BlockSpec((tt, D_MODEL), row),
            pl.BlockSpec((1, 8, tt), lambda i, *_: (i, 0, 0)),
            pl.BlockSpec((1, 8, tt), lambda i, *_: (i, 0, 0)),
            pl.BlockSpec((tt, PLE_DIM), row),
            pl.BlockSpec((1, D_MODEL), const2),
            pl.BlockSpec((D_MODEL, D_MODEL), const2),
            pl.BlockSpec((PLE_DIM, D_MODEL), const2),
            pl.BlockSpec((1, D_MODEL), const2),
            pl.BlockSpec(memory_space=pl.ANY),
        ],
        out_specs=pl.BlockSpec((tt, D_MODEL), row),
        scratch_shapes=[pltpu.VMEM((2, LOCAL_ROWS, D_MODEL), BF16), pltpu.SemaphoreType.DMA((2,))],
    )
    return pl.pallas_call(
        _combine_kernel,
        grid_spec=grid_spec,
        out_shape=jax.ShapeDtypeStruct((t, D_MODEL), F32),
        compiler_params=pltpu.CompilerParams(
            dimension_semantics=("arbitrary",), vmem_limit_bytes=VMEM_LIMIT),
        name="combine",
    )(loff, pcnt, gstart, x1, pos, gates, p2d, gple, wg, wp, gfin, ys)


def _routing_tables(pc16):
    pcnt = pc16 * ROW_ALIGN
    loff = jnp.cumsum(pcnt, axis=1) - pcnt
    seg = jnp.sum(pcnt, axis=0)
    segpad = (seg + EXPERT_TILE - 1) // EXPERT_TILE * EXPERT_TILE
    seg_end = jnp.cumsum(segpad)
    ebase = seg_end - segpad
    gstart = ebase[None, :] + jnp.cumsum(pcnt, axis=0) - pcnt
    tails = jnp.stack([ebase + seg, segpad - seg, jnp.broadcast_to(seg_end[-1] // EXPERT_TILE, seg.shape)])
    return pcnt, loff, gstart, tails, seg_end


def kernel(x, p, g_mix, w_in, w_pool, pool_scale, rel_bias, w_out, g_ffn, w_router, b_router, w_gate_up,
           b_gate_up, w_down, b_down, g_ple, w_ple_gate, w_ple_proj, g_final):
    b, s, d = x.shape
    t = b * s
    x2d = x.reshape(t, d)
    for layer in range(w_in.shape[0]):
        col = jnp.arange(w_in.shape[2])
        is_q = (col >= POOL_WIDTH) & (col < POOL_WIDTH + ATTN_WIDTH)
        w_in_scaled = (w_in[layer] * jnp.where(is_q, QUERY_SCALE, 1.0)).astype(BF16)
        u, *qkv_by_dil = _inproj(x2d, g_mix[layer][None], w_in_scaled)
        o_list, lse_list = [], []
        tabs = _bias_tables(rel_bias)
        for bi, ((_, dil), qkv_d, (rq, nq)) in enumerate(zip(DILATED_BRANCHES, qkv_by_dil, ATTN_STEP)):
            o_d, lse_d = _attn_branch(qkv_d, tabs[bi], b, s, dil, rq, nq)
            o_list.append(o_d)
            lse_list.append(lse_d)
        brt = jnp.broadcast_to(b_router[layer][:, None], (N_EXPERTS, 128))
        x1, h2, pos, gates, pc = _mixout(
            x2d, u, o_list, lse_list, w_pool[layer].astype(BF16), pool_scale[layer][None],
            w_out[layer].astype(BF16), g_ffn[layer][None], w_router[layer].T, brt, s)

        pc16 = pc[:, :, 0].astype(I32)
        pcnt, loff, gstart, tails, seg_end = _routing_tables(pc16)
        n_rows = t * TOP_K + N_EXPERTS * (t // TOKEN_TILE) * ROW_ALIGN + N_EXPERTS * (EXPERT_TILE - 1)
        n_rows = (n_rows + EXPERT_TILE - 1) // EXPERT_TILE * EXPERT_TILE
        n_tiles = n_rows // EXPERT_TILE
        tile_start = jnp.arange(n_tiles, dtype=I32) * EXPERT_TILE
        tile_expert = jnp.minimum(jnp.sum(seg_end[None, :] <= tile_start[:, None], axis=1), N_EXPERTS - 1).astype(I32)
        n_used = (seg_end[-1] // EXPERT_TILE).astype(I32)[None]
        xs = _dispatch(loff, pcnt, gstart, tails, h2, pos, n_rows)
        has_rows = jnp.diff(seg_end, prepend=0) > 0
        experts = jnp.arange(N_EXPERTS, dtype=I32)
        stage_slot = ((jnp.cumsum(has_rows) - 1) % 2).astype(I32)
        next_expert = jnp.min(jnp.where(has_rows[None, :] & (experts[None, :] > experts[:, None]),
                                        experts[None, :], N_EXPERTS), axis=1).astype(I32)
        ys = _experts(tile_expert, n_used, stage_slot, next_expert, xs, w_gate_up[layer],
                      b_gate_up[layer][:, None, :], w_down[layer], b_down[layer][:, None, :])

        assert layer == w_in.shape[0] - 1, "single-layer pipeline: the final norm is fused into combine"
        x2d = _combine(loff, pcnt, gstart, x1, pos, gates, p[layer].reshape(t, PLE_DIM), g_ple[layer][None],
                       w_ple_gate[layer].astype(BF16), w_ple_proj[layer].astype(BF16), g_final[None], ys)
    return x2d.reshape(b, s, d)
```

```python
import functools
import math

import jax
import jax.numpy as jnp
from jax import lax
from jax.experimental import pallas as pl
from jax.experimental.pallas import tpu as pltpu

F32 = jnp.float32
BF16 = jnp.bfloat16
I32 = jnp.int32

D_MODEL = 1024
POOL_WIDTH = 512
POOL_WINDOWS = (2, 4, 8, 16)
POOL_GROUP_DIM = 128
ATTN_WIDTH = 512
QKV_WIDTH = 3 * ATTN_WIDTH
HEAD_DIM = 64
N_HEADS = 8
DILATED_BRANCHES = ((128, 1), (512, 4), (2048, 16))
ATTN_BLOCK = 128
N_REL_BUCKETS = 32
REL_MAX_EXACT = 16
REL_MAX_DISTANCE = 2048
N_EXPERTS = 32
TOP_K = 4
D_EXPERT = 1024
SWIGLU_LIMIT = 7.0
SWIGLU_ALPHA = 1.702
PLE_DIM = 256
NORM_EPS = 1e-6
NEG_INF = -1e30
LOG2_E = math.log2(math.e)
QUERY_SCALE = LOG2_E / math.sqrt(HEAD_DIM)

LANES = 128
SUBLANES = 8
POOL_HALO = 16
TOKEN_TILE = 512
ROW_ALIGN = 16
EXPERT_TILE = 512
EXPERT_CHUNK = 256
LOCAL_ROWS = TOKEN_TILE * TOP_K + N_EXPERTS * ROW_ALIGN
SORT_BLOCK = 256
GATHER_BLOCK = 512
ATTN_STEP = ((1, 8), (1, 8), (4, 2))
VMEM_LIMIT = 56 * 1024 * 1024


def _rms(x, g):
    return x * lax.rsqrt(jnp.mean(x * x, axis=-1, keepdims=True) + NORM_EPS) * g


def _inproj_kernel(x_ref, g_ref, w_ref, u_ref, nat_ref, d4_ref, d16_ref, z_scr, g_scr):
    h = _rms(x_ref[...], g_ref[...]).astype(BF16)
    part = ATTN_WIDTH
    slabs = part // LANES

    def project(i):
        return jnp.dot(h, w_ref[:, pl.ds(i * part, part)], preferred_element_type=F32)

    def regroup(z, i):
        col0 = (i - 1) * part
        nat_ref[:, pl.ds(col0, part)] = z.astype(BF16)
        for c in range(slabs):
            slab = (i - 1) * slabs + c
            cols = col0 + c * LANES
            z_scr[slab] = z[:, c * LANES:(c + 1) * LANES]
            for rl in range(4):
                grp = z_scr[slab, pl.ds(rl, TOKEN_TILE // 4, stride=4), :]
                d4_ref[:, pl.ds(rl * QKV_WIDTH + cols, LANES)] = grp.astype(BF16)
                g_scr[rl * n_slabs + slab] = grp
            for rl in range(4):
                for rh in range(4):
                    d16_ref[:, pl.ds((4 * rh + rl) * QKV_WIDTH + cols, LANES)] = (
                        g_scr[rl * n_slabs + slab, pl.ds(rh, TOKEN_TILE // 16, stride=4), :].astype(BF16))

    assert POOL_WIDTH == part and QKV_WIDTH == 3 * part and [d for _, d in DILATED_BRANCHES] == [1, 4, 16]
    n_slabs = QKV_WIDTH // LANES
    z_prev = project(0)
    for i in range(1, 4):
        z_next = project(i)
        if i == 1:
            u_ref[...] = z_prev
        else:
            regroup(z_prev, i - 1)
        z_prev = z_next
    regroup(z_prev, 3)


def _inproj(x2d, g, w_bf16):
    t = x2d.shape[0]
    in_w = w_bf16.shape[1]
    tt = TOKEN_TILE
    return pl.pallas_call(
        _inproj_kernel,
        grid=(t // tt,),
        in_specs=[
            pl.BlockSpec((tt, D_MODEL), lambda i: (i, 0)),
            pl.BlockSpec((1, D_MODEL), lambda i: (0, 0)),
            pl.BlockSpec((D_MODEL, in_w), lambda i: (0, 0)),
        ],
        out_specs=[
            pl.BlockSpec((tt, POOL_WIDTH), lambda i: (i, 0)),
            pl.BlockSpec((tt, QKV_WIDTH), lambda i: (i, 0)),
            pl.BlockSpec((tt // 4, 4 * QKV_WIDTH), lambda i: (i, 0)),
            pl.BlockSpec((tt // 16, 16 * QKV_WIDTH), lambda i: (i, 0)),
        ],
        out_shape=[
            jax.ShapeDtypeStruct((t, POOL_WIDTH), F32),
            jax.ShapeDtypeStruct((t, QKV_WIDTH), BF16),
            jax.ShapeDtypeStruct((t // 4, 4 * QKV_WIDTH), BF16),
            jax.ShapeDtypeStruct((t // 16, 16 * QKV_WIDTH), BF16),
        ],
        scratch_shapes=[pltpu.VMEM((QKV_WIDTH // LANES, tt, LANES), F32),
                        pltpu.VMEM((4 * QKV_WIDTH // LANES, tt // 4, LANES), F32)],
        compiler_params=pltpu.CompilerParams(
            dimension_semantics=("parallel",), vmem_limit_bytes=VMEM_LIMIT),
        name="inproj",
    )(x2d, g, w_bf16)


def _attn_kernel(main_ref, prev_ref, tab_ref, o_ref, lse_ref, k_scr, vt_scr, *, rq, nq):
    blk = ATTN_BLOCK
    first_group = pl.program_id(2) == 0
    nt = (((1,), (1,)), ((), ()))

    for ri in range(rq):
        base = ri * QKV_WIDTH
        k_scr[0:blk, :] = prev_ref[:, pl.ds(base + ATTN_WIDTH, ATTN_WIDTH)]
        k_scr[blk:, :] = main_ref[:, pl.ds(base + ATTN_WIDTH, ATTN_WIDTH)]
        vt_scr[:, 0:blk] = prev_ref[:, pl.ds(base + 2 * ATTN_WIDTH, ATTN_WIDTH)].astype(F32).T.astype(BF16)
        for j in range(nq):
            vt_scr[:, pl.ds((j + 1) * blk, blk)] = (
                main_ref[pl.ds(j * blk, blk), pl.ds(base + 2 * ATTN_WIDTH, ATTN_WIDTH)].astype(F32).T.astype(BF16))
        for j in range(nq):
            rows = pl.ds(j * blk, blk)
            keys = pl.ds(j * blk, 2 * blk)
            scores = []
            for h in range(N_HEADS):
                q = main_ref[rows, pl.ds(base + h * HEAD_DIM, HEAD_DIM)]
                k2 = k_scr[keys, pl.ds(h * HEAD_DIM, HEAD_DIM)]
                scores.append(lax.dot_general(k2, q, nt, preferred_element_type=F32))
            probs, inv_l, lse_parts = [], [], []
            for h in range(N_HEADS):
                s = scores[h] + tab_ref[h]
                if j == 0:
                    pen = jnp.where(first_group, NEG_INF, 0.0).astype(F32)
                    s = jnp.concatenate([s[:blk] + pen, s[blk:]], axis=0)
                m = jnp.max(s, axis=0, keepdims=True)
                p = jnp.exp2(s - m)
                l = jnp.sum(p, axis=0, keepdims=True)
                probs.append(p.astype(BF16))
                inv_l.append(1.0 / l)
                lse_parts.append(m * math.log(2.0) + jnp.log(l))
            o_parts = []
            for h in range(N_HEADS):
                vt2 = vt_scr[pl.ds(h * HEAD_DIM, HEAD_DIM), keys]
                o_parts.append(jnp.dot(vt2, probs[h], preferred_element_type=F32) * inv_l[h])
            o_t = jnp.concatenate(o_parts, axis=0)
            o_ref[rows, pl.ds(ri * ATTN_WIDTH, ATTN_WIDTH)] = o_t.T.astype(BF16)
            lse_t = jnp.concatenate(lse_parts + [jnp.zeros((LANES - N_HEADS, blk), F32)], axis=0)
            lse_ref[rows, pl.ds(ri * LANES, LANES)] = lse_t.T


def _attn_branch(qkv_d, tab, batch, seq, dil, rq, nq):
    sub = seq // dil
    nb = sub // ATTN_BLOCK
    groups = nb // nq
    return pl.pallas_call(
        functools.partial(_attn_kernel, rq=rq, nq=nq),
        grid=(batch, dil // rq, groups),
        in_specs=[
            pl.BlockSpec((nq * ATTN_BLOCK, rq * QKV_WIDTH), lambda b, r, g: (b * groups + g, r)),
            pl.BlockSpec((ATTN_BLOCK, rq * QKV_WIDTH), lambda b, r, g: (b * nb + jnp.maximum(g * nq - 1, 0), r)),
            pl.BlockSpec((N_HEADS, 2 * ATTN_BLOCK, ATTN_BLOCK), lambda b, r, g: (0, 0, 0)),
        ],
        out_specs=[
            pl.BlockSpec((nq * ATTN_BLOCK, rq * ATTN_WIDTH), lambda b, r, g: (b * groups + g, r)),
            pl.BlockSpec((nq * ATTN_BLOCK, rq * LANES), lambda b, r, g: (b * groups + g, r)),
        ],
        out_shape=[
            jax.ShapeDtypeStruct((batch * sub, dil * ATTN_WIDTH), BF16),
            jax.ShapeDtypeStruct((batch * sub, dil * LANES), F32),
        ],
        scratch_shapes=[
            pltpu.VMEM(((nq + 1) * ATTN_BLOCK, ATTN_WIDTH), BF16),
            pltpu.VMEM((ATTN_WIDTH, (nq + 1) * ATTN_BLOCK), BF16),
        ],
        compiler_params=pltpu.CompilerParams(
            dimension_semantics=("parallel", "parallel", "parallel"), vmem_limit_bytes=VMEM_LIMIT),
        name="attn",
    )(qkv_d, qkv_d, tab)


def _t5_bucket(dist):
    n = jnp.maximum(dist, 1).astype(F32)
    large = REL_MAX_EXACT + (jnp.log(n / REL_MAX_EXACT) / math.log(REL_MAX_DISTANCE / REL_MAX_EXACT)
                             * (N_REL_BUCKETS - REL_MAX_EXACT)).astype(I32)
    large = jnp.minimum(large, N_REL_BUCKETS - 1)
    return jnp.where(dist < REL_MAX_EXACT, dist, large)


def _shifted_rows(w, n):
    lead = w.shape[:-1]
    width = w.shape[-1]
    flat = jnp.tile(w, (1,) * len(lead) + (n + 1,))[..., :n * (width + 1)]
    return flat.reshape(lead + (n, width + 1))[..., :n]


def _bias_tables(rel_bias):
    blk = ATTN_BLOCK
    assert all(window // dil == blk for window, dil in DILATED_BRANCHES)
    n_br = len(DILATED_BRANCHES)
    dist = jnp.stack([jnp.arange(blk + 1) * dil for _, dil in DILATED_BRANCHES])
    f = jnp.swapaxes(rel_bias[_t5_bucket(dist)], 1, 2).astype(F32)
    neg = jnp.full((n_br, N_HEADS, blk - 1), NEG_INF, F32)
    neg1 = jnp.full((n_br, N_HEADS, 1), NEG_INF, F32)
    w_prev = jnp.concatenate([neg, f[..., :0:-1], neg1], axis=-1)
    prev = _shifted_rows(w_prev, blk)[..., ::-1]
    w_cur = jnp.concatenate([neg, f[..., :blk], neg1], axis=-1)
    cur = _shifted_rows(w_cur, blk)[..., ::-1, :]
    return jnp.concatenate([prev, cur], axis=-2) * LOG2_E


def _mixout_kernel(x_ref, u_ref, uh_ref, o1_ref, o4_ref, o16_ref, l1_ref, l4_ref, l16_ref,
                   wpool_ref, pscale_ref, wout_ref, gffn_ref, wrt_ref, brt_ref, spread_ref,
                   x1_ref, h2_ref, pos_ref, gate_ref, pc_ref,
                   ext_scr, mix_scr, o_scr, l_scr, *, tiles_per_seq):
    i = pl.program_id(0)
    tt = TOKEN_TILE
    seq_tile = i % tiles_per_seq

    halo = uh_ref[...]
    ext_scr[0:POOL_HALO, :] = jnp.where(seq_tile == 0, jnp.zeros_like(halo), halo)
    ext_scr[POOL_HALO:, :] = u_ref[...]
    tpos = seq_tile * tt + lax.broadcasted_iota(I32, (tt, 1), 0)
    for gi, w in enumerate(POOL_WINDOWS):
        cols = pl.ds(gi * POOL_GROUP_DIM, POOL_GROUP_DIM)
        run = ext_scr[:, cols]
        shift = 1
        while shift < w:
            run = run + pltpu.roll(run, shift, axis=0)
            shift *= 2
        assert shift == w and w - 1 < POOL_HALO
        acc = run[POOL_HALO:]
        tok = ext_scr[pl.ds(POOL_HALO, tt), cols]
        cnt = jnp.minimum(tpos + 1, w).astype(F32)
        pooled = (acc / cnt - tok).astype(BF16)
        mixed = jnp.dot(pooled, wpool_ref[gi], preferred_element_type=F32) * pscale_ref[:, cols]
        mix_scr[:, cols] = mixed.astype(BF16)

    o_slabs = ATTN_WIDTH // LANES
    for bi, (dil, o_ref, l_ref) in enumerate(((4, o4_ref, l4_ref), (16, o16_ref, l16_ref))):
        rows = tt // dil
        for r in range(dil):
            l_scr[bi, pl.ds(r, rows, stride=dil), :] = l_ref[:, pl.ds(r * LANES, LANES)]
            for c in range(o_slabs):
                o_scr[bi * o_slabs + c, pl.ds(r, rows, stride=dil), :] = (
                    o_ref[:, pl.ds(r * ATTN_WIDTH + c * LANES, LANES)].astype(F32))
    l1, l4, l16 = l1_ref[...], l_scr[0], l_scr[1]
    lm = jnp.maximum(jnp.maximum(l1, l4), l16)
    e1, e4, e16 = jnp.exp(l1 - lm), jnp.exp(l4 - lm), jnp.exp(l16 - lm)
    inv = 1.0 / (e1 + e4 + e16)

    def per_feature(w):
        hi = w.astype(BF16)
        lo = (w - hi.astype(F32)).astype(BF16)
        return (jnp.dot(hi, spread_ref[...], preferred_element_type=F32)
                + jnp.dot(lo, spread_ref[...], preferred_element_type=F32))

    w1, w4, w16 = per_feature(e1 * inv), per_feature(e4 * inv), per_feature(e16 * inv)
    for c in range(o_slabs):
        sl = pl.ds(c * LANES, LANES)
        cs = slice(c * LANES, (c + 1) * LANES)
        o = w1[:, cs] * o1_ref[:, sl].astype(F32) + w4[:, cs] * o_scr[c] + w16[:, cs] * o_scr[o_slabs + c]
        mix_scr[:, pl.ds(POOL_WIDTH + c * LANES, LANES)] = o.astype(BF16)

    x1 = x_ref[...] + jnp.dot(mix_scr[...], wout_ref[...], preferred_element_type=F32)
    x1_ref[...] = x1

    h2 = _rms(x1, gffn_ref[...])
    h2_ref[...] = h2.astype(BF16)
    nt = (((1,), (1,)), ((), ()))
    wr = wrt_ref[...]
    wr_hi = wr.astype(BF16)
    wr_lo = (wr - wr_hi.astype(F32)).astype(BF16)
    h2_hi = h2.astype(BF16)
    h2_lo = (h2 - h2_hi.astype(F32)).astype(BF16)
    logits = (lax.dot_general(wr_hi, h2_hi, nt, preferred_element_type=F32)
              + lax.dot_general(wr_hi, h2_lo, nt, preferred_element_type=F32)
              + lax.dot_general(wr_lo, h2_hi, nt, preferred_element_type=F32))
    logits = logits + brt_ref[:, 0:1]
    eio = lax.broadcasted_iota(I32, (N_EXPERTS, tt), 0)
    sel, val = [], []
    for _ in range(TOP_K):
        m = jnp.max(logits, axis=0, keepdims=True)
        idx = jnp.min(jnp.where(logits == m, eio, N_EXPERTS), axis=0, keepdims=True)
        sel.append(idx)
        val.append(m)
        logits = jnp.where(eio == idx, -jnp.inf, logits)
    ex = [jnp.exp(v - val[0]) for v in val]
    den = ex[0] + ex[1] + ex[2] + ex[3]
    gates = [e / den for e in ex]

    hot = [(eio == s) for s in sel]
    onehot = (hot[0] | hot[1] | hot[2] | hot[3]).astype(F32)
    ti = lax.broadcasted_iota(I32, (tt, tt), 0)
    tj = lax.broadcasted_iota(I32, (tt, tt), 1)
    before = (ti < tj).astype(BF16)
    rank = jnp.dot(onehot.astype(BF16), before, preferred_element_type=F32)
    cnt = jnp.sum(onehot, axis=1, keepdims=True)
    pc = jnp.maximum(jnp.floor((cnt + (ROW_ALIGN - 1)) * (1.0 / ROW_ALIGN)), 1.0)
    pcb = jnp.broadcast_to(pc, (N_EXPERTS, LANES))
    pc_ref[0] = pcb
    pcm = jnp.concatenate([pcb, jnp.zeros((LANES - N_EXPERTS, LANES), F32)], axis=0).astype(BF16)
    li = lax.broadcasted_iota(I32, (N_EXPERTS, LANES), 0)
    lj = lax.broadcasted_iota(I32, (N_EXPERTS, LANES), 1)
    lower = (lj < li).astype(BF16)
    off = jnp.dot(lower, pcm, preferred_element_type=F32)[:, 0:1] * float(ROW_ALIGN)
    where_to = off + rank
    zero_i = jnp.zeros((1, tt), I32)
    zero_f = jnp.zeros((1, tt), F32)
    pos_rows = [jnp.sum(jnp.where(hk, where_to, 0.0), axis=0, keepdims=True).astype(I32) for hk in hot]
    pos_ref[0] = jnp.concatenate(pos_rows + [zero_i] * (SUBLANES - TOP_K), axis=0)
    gate_ref[0] = jnp.concatenate(gates + [zero_f] * (SUBLANES - TOP_K), axis=0)


def _mixout(x2d, u, o_list, lse_list, wpool, pscale, wout, gffn, wrt, brt, seq):
    t = x2d.shape[0]
    tt = TOKEN_TILE
    n_tt = t // tt
    halo_blocks = tt // POOL_HALO
    row = lambda i: (i, 0)
    const2 = lambda i: (0, 0)
    dils = [dil for _, dil in DILATED_BRANCHES]
    spread = (jnp.arange(LANES)[:, None] == jnp.arange(ATTN_WIDTH)[None, :] // HEAD_DIM).astype(BF16)
    return pl.pallas_call(
        functools.partial(_mixout_kernel, tiles_per_seq=seq // tt),
        grid=(n_tt,),
        in_specs=[
            pl.BlockSpec((tt, D_MODEL), row),
            pl.BlockSpec((tt, POOL_WIDTH), row),
            pl.BlockSpec((POOL_HALO, POOL_WIDTH), lambda i: (jnp.maximum(i * halo_blocks - 1, 0), 0)),
            *[pl.BlockSpec((tt // dil, dil * ATTN_WIDTH), row) for dil in dils],
            *[pl.BlockSpec((tt // dil, dil * LANES), row) for dil in dils],
            pl.BlockSpec((len(POOL_WINDOWS), POOL_GROUP_DIM, POOL_GROUP_DIM), lambda i: (0, 0, 0)),
            pl.BlockSpec((1, POOL_WIDTH), const2),
            pl.BlockSpec((D_MODEL, D_MODEL), const2),
            pl.BlockSpec((1, D_MODEL), const2),
            pl.BlockSpec((N_EXPERTS, D_MODEL), const2),
            pl.BlockSpec((N_EXPERTS, LANES), const2),
            pl.BlockSpec((LANES, ATTN_WIDTH), const2),
        ],
        out_specs=[
            pl.BlockSpec((tt, D_MODEL), row),
            pl.BlockSpec((tt, D_MODEL), row),
            pl.BlockSpec((1, SUBLANES, tt), lambda i: (i, 0, 0)),
            pl.BlockSpec((1, SUBLANES, tt), lambda i: (i, 0, 0)),
            pl.BlockSpec((1, N_EXPERTS, LANES), lambda i: (i, 0, 0)),
        ],
        out_shape=[
            jax.ShapeDtypeStruct((t, D_MODEL), F32),
            jax.ShapeDtypeStruct((t, D_MODEL), BF16),
            jax.ShapeDtypeStruct((n_tt, SUBLANES, tt), I32),
            jax.ShapeDtypeStruct((n_tt, SUBLANES, tt), F32),
            jax.ShapeDtypeStruct((n_tt, N_EXPERTS, LANES), F32),
        ],
        scratch_shapes=[
            pltpu.VMEM((tt + POOL_HALO, POOL_WIDTH), F32),
            pltpu.VMEM((tt, D_MODEL), BF16),
            pltpu.VMEM((2 * ATTN_WIDTH // LANES, tt, LANES), F32),
            pltpu.VMEM((2, tt, LANES), F32),
        ],
        compiler_params=pltpu.CompilerParams(
            dimension_semantics=("parallel",), vmem_limit_bytes=VMEM_LIMIT),
        name="mixout",
    )(x2d, u, u, *o_list, *lse_list, wpool, pscale, wout, gffn, wrt, brt, spread)


def _chunk_copies(loff_s, pcnt_s, gstart_s, tile, local_ref, global_ref, sems, to_global):
    slot = tile % 2
    out = []
    for e in range(N_EXPERTS):
        n = pl.multiple_of(pcnt_s[tile, e], ROW_ALIGN)
        lo = pl.multiple_of(loff_s[tile, e], ROW_ALIGN)
        go = pl.multiple_of(gstart_s[tile, e], ROW_ALIGN)
        loc = local_ref.at[slot, pl.ds(lo, n)]
        glo = global_ref.at[pl.ds(go, n)]
        src, dst = (loc, glo) if to_global else (glo, loc)
        out.append(pltpu.make_async_copy(src, dst, sems.at[slot]))
    return out


def _start_all(copies):
    for cp in copies:
        cp.start()


def _wait_chunks(loff_s, pcnt_s, tile, local_ref, global_ref, sems, to_global):
    slot = tile % 2
    rows = pl.multiple_of(loff_s[tile, N_EXPERTS - 1] + pcnt_s[tile, N_EXPERTS - 1], ROW_ALIGN)
    loc = local_ref.at[slot, pl.ds(0, rows)]
    glo = global_ref.at[pl.ds(0, rows)]
    src, dst = (loc, glo) if to_global else (glo, loc)
    pltpu.make_async_copy(src, dst, sems.at[slot]).wait()


def _dispatch_kernel(loff_s, pcnt_s, gstart_s, tail_s, h2_ref, pos_ref, xs_ref, loc_scr, zero_scr, sem, tail_sem,
                     spare_sem):
    i = pl.program_id(0)
    tt = TOKEN_TILE

    @pl.when(i == 0)
    def _():
        zero_scr[...] = jnp.zeros_like(zero_scr)
        tails = []
        for e in range(N_EXPERTS):
            n = pl.multiple_of(tail_s[1, e], ROW_ALIGN)
            start = pl.multiple_of(tail_s[0, e], ROW_ALIGN)
            tails.append((n > 0, pltpu.make_async_copy(zero_scr.at[pl.ds(0, n)], xs_ref.at[pl.ds(start, n)], tail_sem)))
        for cond, cp in tails:
            pl.when(cond)(cp.start)
        for cond, cp in tails:
            pl.when(cond)(cp.wait)

    def spare_tile(j):
        return pltpu.make_async_copy(
            zero_scr, xs_ref.at[pl.ds(pl.multiple_of(j * EXPERT_TILE, EXPERT_TILE), EXPERT_TILE)], spare_sem)

    n_tiles = xs_ref.shape[0] // EXPERT_TILE

    @pl.when(i == 0)
    def _():
        lax.fori_loop(tail_s[2, 0], n_tiles, lambda j, c: (spare_tile(j).start(), c)[1], 0)

    @pl.when(i == pl.num_programs(0) - 1)
    def _():
        lax.fori_loop(tail_s[2, 0], n_tiles, lambda j, c: (spare_tile(j).wait(), c)[1], 0)

    pos = pos_ref[0]
    h2 = h2_ref[...]
    slot = i % 2
    used_rows = loff_s[i, N_EXPERTS - 1] + pcnt_s[i, N_EXPERTS - 1]
    def one_hot(jb):
        jio = lax.broadcasted_iota(I32, (SORT_BLOCK, tt), 0) + jb * SORT_BLOCK
        hit = (jio == pos[0:1]) | (jio == pos[1:2]) | (jio == pos[2:3]) | (jio == pos[3:4])
        return jnp.where(hit, 1.0, 0.0).astype(BF16)

    def sort_block(jb, onehot):
        loc_scr[slot, pl.ds(jb * SORT_BLOCK, SORT_BLOCK), :] = jnp.dot(
            onehot, h2, preferred_element_type=F32).astype(BF16)

    n_sure = TOKEN_TILE * TOP_K // SORT_BLOCK
    onehot = one_hot(0)
    for jb in range(n_sure):
        nxt = one_hot(jb + 1) if jb + 1 < n_sure else None
        sort_block(jb, onehot)
        onehot = nxt
    for jb in range(n_sure, LOCAL_ROWS // SORT_BLOCK):
        pl.when(jb * SORT_BLOCK < used_rows)(lambda jb=jb: sort_block(jb, one_hot(jb)))

    @pl.when(i > 0)
    def _():
        _wait_chunks(loff_s, pcnt_s, i - 1, loc_scr, xs_ref, sem, True)

    _start_all(_chunk_copies(loff_s, pcnt_s, gstart_s, i, loc_scr, xs_ref, sem, True))

    @pl.when(i == pl.num_programs(0) - 1)
    def _():
        _wait_chunks(loff_s, pcnt_s, i, loc_scr, xs_ref, sem, True)


def _dispatch(loff, pcnt, gstart, tails, h2, pos, n_rows):
    t = h2.shape[0]
    tt = TOKEN_TILE
    grid_spec = pltpu.PrefetchScalarGridSpec(
        num_scalar_prefetch=4,
        grid=(t // tt,),
        in_specs=[
            pl.BlockSpec((tt, D_MODEL), lambda i, *_: (i, 0)),
            pl.BlockSpec((1, SUBLANES, tt), lambda i, *_: (i, 0, 0)),
        ],
        out_specs=pl.BlockSpec(memory_space=pl.ANY),
        scratch_shapes=[
            pltpu.VMEM((2, LOCAL_ROWS, D_MODEL), BF16),
            pltpu.VMEM((EXPERT_TILE, D_MODEL), BF16),
            pltpu.SemaphoreType.DMA((2,)),
            pltpu.SemaphoreType.DMA(()),
            pltpu.SemaphoreType.DMA(()),
        ],
    )
    return pl.pallas_call(
        _dispatch_kernel,
        grid_spec=grid_spec,
        out_shape=jax.ShapeDtypeStruct((n_rows, D_MODEL), BF16),
        compiler_params=pltpu.CompilerParams(
            dimension_semantics=("arbitrary",), vmem_limit_bytes=VMEM_LIMIT),
        name="dispatch",
    )(loff, pcnt, gstart, tails, h2, pos)


def _expert_kernel(te_s, nu_s, slot_s, next_s, xs_ref, wgu_hbm, bgu_ref, wd_hbm, bd_ref, ys_ref,
                   wgu_stage, wd_stage, wgu_bf, wd_bf, act_scr, wsem):
    i = pl.program_id(0)
    live = i < nu_s[0]
    expert = te_s[i]
    new_expert = (i == 0) | (expert != te_s[jnp.maximum(i - 1, 0)])

    def weight_copies(e, slot):
        return (pltpu.make_async_copy(wgu_hbm.at[e], wgu_stage.at[slot], wsem.at[0, slot]),
                pltpu.make_async_copy(wd_hbm.at[e], wd_stage.at[slot], wsem.at[1, slot]))

    @pl.when(live & new_expert)
    def _():
        slot = slot_s[expert]

        @pl.when(i == 0)
        def _():
            for cp in weight_copies(expert, slot):
                cp.start()

        for cp in weight_copies(expert, slot):
            cp.wait()
        wgu_bf[...] = wgu_stage[slot].astype(BF16)
        wd_bf[...] = wd_stage[slot].astype(BF16)
        upcoming = next_s[expert]

        @pl.when(upcoming < N_EXPERTS)
        def _():
            for cp in weight_copies(upcoming, 1 - slot):
                cp.start()

    @pl.when(live)
    def _():
        x = xs_ref[...]
        n_chunks = D_EXPERT // EXPERT_CHUNK

        def gate_up(c):
            gc = pl.ds(c * EXPERT_CHUNK, EXPERT_CHUNK)
            uc = pl.ds(D_EXPERT + c * EXPERT_CHUNK, EXPERT_CHUNK)
            return (jnp.dot(x, wgu_bf[:, gc], preferred_element_type=F32) + bgu_ref[0, :, gc],
                    jnp.dot(x, wgu_bf[:, uc], preferred_element_type=F32) + bgu_ref[0, :, uc])

        g, u = gate_up(0)
        for c in range(n_chunks):
            nxt = gate_up(c + 1) if c + 1 < n_chunks else None
            g = jnp.minimum(g, SWIGLU_LIMIT)
            u = jnp.clip(u, -SWIGLU_LIMIT, SWIGLU_LIMIT)
            act_scr[:, pl.ds(c * EXPERT_CHUNK, EXPERT_CHUNK)] = (
                (u + 1.0) * (g * jax.nn.sigmoid(SWIGLU_ALPHA * g))).astype(BF16)
            if nxt is not None:
                g, u = nxt
        y = jnp.dot(act_scr[...], wd_bf[...], preferred_element_type=F32) + bd_ref[0]
        ys_ref[...] = y.astype(BF16)

    @pl.when(jnp.logical_not(live))
    def _():
        ys_ref[...] = jnp.zeros_like(ys_ref)


def _experts(tile_expert, n_used, stage_slot, next_expert, xs, wgu, bgu, wd, bd):
    n_rows = xs.shape[0]
    tm = EXPERT_TILE
    live = lambda i, te, nu: jnp.minimum(i, nu[0] - 1)
    grid_spec = pltpu.PrefetchScalarGridSpec(
        num_scalar_prefetch=4,
        grid=(n_rows // tm,),
        in_specs=[
            pl.BlockSpec((tm, D_MODEL), lambda i, te, nu, *_: (live(i, te, nu), 0)),
            pl.BlockSpec(memory_space=pl.ANY),
            pl.BlockSpec((1, 1, 2 * D_EXPERT), lambda i, te, nu, *_: (te[live(i, te, nu)], 0, 0)),
            pl.BlockSpec(memory_space=pl.ANY),
            pl.BlockSpec((1, 1, D_MODEL), lambda i, te, nu, *_: (te[live(i, te, nu)], 0, 0)),
        ],
        out_specs=pl.BlockSpec((tm, D_MODEL), lambda i, te, nu, *_: (i, 0)),
        scratch_shapes=[
            pltpu.VMEM((2, D_MODEL, 2 * D_EXPERT), F32),
            pltpu.VMEM((2, D_EXPERT, D_MODEL), F32),
            pltpu.VMEM((D_MODEL, 2 * D_EXPERT), BF16),
            pltpu.VMEM((D_EXPERT, D_MODEL), BF16),
            pltpu.VMEM((tm, D_EXPERT), BF16),
            pltpu.SemaphoreType.DMA((2, 2)),
        ],
    )
    return pl.pallas_call(
        _expert_kernel,
        grid_spec=grid_spec,
        out_shape=jax.ShapeDtypeStruct((n_rows, D_MODEL), BF16),
        compiler_params=pltpu.CompilerParams(
            dimension_semantics=("arbitrary",), vmem_limit_bytes=VMEM_LIMIT),
        name="experts",
    )(tile_expert, n_used, stage_slot, next_expert, xs, wgu, bgu, wd, bd)


def _combine_kernel(loff_s, pcnt_s, gstart_s, x1_ref, pos_ref, gate_ref, p_ref, gple_ref, wg_ref, wp_ref,
                    gfin_ref, ys_ref, out_ref, loc_scr, sem):
    i = pl.program_id(0)
    tt = TOKEN_TILE

    @pl.when(i == 0)
    def _():
        loc_scr[...] = jnp.zeros_like(loc_scr)
        _start_all(_chunk_copies(loff_s, pcnt_s, gstart_s, i, loc_scr, ys_ref, sem, False))

    _wait_chunks(loff_s, pcnt_s, i, loc_scr, ys_ref, sem, False)

    @pl.when(i + 1 < pl.num_programs(0))
    def _():
        _start_all(_chunk_copies(loff_s, pcnt_s, gstart_s, i + 1, loc_scr, ys_ref, sem, False))

    slot = i % 2

    def to_columns(rows):
        return jnp.concatenate([rows, jnp.zeros((LANES - rows.shape[0], tt), F32)], axis=0).T

    post = to_columns(pos_ref[0].astype(F32))
    gatet = to_columns(gate_ref[0])

    def gate_block(jb):
        jio = (lax.broadcasted_iota(I32, (tt, GATHER_BLOCK), 1) + jb * GATHER_BLOCK).astype(F32)
        w = jnp.zeros((tt, GATHER_BLOCK), F32)
        for k in range(TOP_K):
            w = jnp.where(jio == post[:, k:k + 1], gatet[:, k:k + 1], w)
        return w.astype(BF16)

    n_blocks = LOCAL_ROWS // GATHER_BLOCK
    moe = jnp.zeros((tt, D_MODEL), F32)
    w = gate_block(0)
    for jb in range(n_blocks):
        nxt = gate_block(jb + 1) if jb + 1 < n_blocks else None
        y = loc_scr[slot, pl.ds(jb * GATHER_BLOCK, GATHER_BLOCK), :]
        moe = moe + jnp.dot(w, y, preferred_element_type=F32)
        w = nxt

    x2 = x1_ref[...] + moe
    gate = jax.nn.sigmoid(jnp.dot(_rms(x2, gple_ref[...]).astype(BF16), wg_ref[...], preferred_element_type=F32))
    emb = jnp.dot(p_ref[...].astype(BF16), wp_ref[...], preferred_element_type=F32)
    out_ref[...] = _rms(x2 + emb * gate, gfin_ref[...])


def _combine(loff, pcnt, gstart, x1, pos, gates, p2d, gple, wg, wp, gfin, ys):
    t = x1.shape[0]
    tt = TOKEN_TILE
    row = lambda i, *_: (i, 0)
    const2 = lambda i, *_: (0, 0)
    grid_spec = pltpu.PrefetchScalarGridSpec(
        num_scalar_prefetch=3,
        grid=(t // tt,),
        in_specs=[
            pl.BlockSpec((tt, D_MODEL), row),
            pl.BlockSpec((1, SUBLANES, tt), lambda i, *_: (i, 0, 0)),
            pl.BlockSpec((1, SUBLANES, tt), lambda i, *_: (i, 0, 0)),
            pl.BlockSpec((tt, PLE_DIM), row),
            pl.BlockSpec((1, D_MODEL), const2),
            pl.BlockSpec((D_MODEL, D_MODEL), const2),
            pl.BlockSpec((PLE_DIM, D_MODEL), const2),
            pl.BlockSpec((1, D_MODEL), const2),
            pl.BlockSpec(memory_space=pl.ANY),
        ],
        out_specs=pl.BlockSpec((tt, D_MODEL), row),
        scratch_shapes=[pltpu.VMEM((2, LOCAL_ROWS, D_MODEL), BF16), pltpu.SemaphoreType.DMA((2,))],
    )
    return pl.pallas_call(
        _combine_kernel,
        grid_spec=grid_spec,
        out_shape=jax.ShapeDtypeStruct((t, D_MODEL), F32),
        compiler_params=pltpu.CompilerParams(
            dimension_semantics=("arbitrary",), vmem_limit_bytes=VMEM_LIMIT),
        name="combine",
    )(loff, pcnt, gstart, x1, pos, gates, p2d, gple, wg, wp, gfin, ys)


def _routing_tables(pc16):
    pcnt = pc16 * ROW_ALIGN
    loff = jnp.cumsum(pcnt, axis=1) - pcnt
    seg = jnp.sum(pcnt, axis=0)
    segpad = (seg + EXPERT_TILE - 1) // EXPERT_TILE * EXPERT_TILE
    seg_end = jnp.cumsum(segpad)
    ebase = seg_end - segpad
    gstart = ebase[None, :] + jnp.cumsum(pcnt, axis=0) - pcnt
    tails = jnp.stack([ebase + seg, segpad - seg, jnp.broadcast_to(seg_end[-1] // EXPERT_TILE, seg.shape)])
    return pcnt, loff, gstart, tails, seg_end


def kernel(x, p, g_mix, w_in, w_pool, pool_scale, rel_bias, w_out, g_ffn, w_router, b_router, w_gate_up,
           b_gate_up, w_down, b_down, g_ple, w_ple_gate, w_ple_proj, g_final):
    b, s, d = x.shape
    t = b * s
    x2d = x.reshape(t, d)
    for layer in range(w_in.shape[0]):
        col = jnp.arange(w_in.shape[2])
        is_q = (col >= POOL_WIDTH) & (col < POOL_WIDTH + ATTN_WIDTH)
        w_in_scaled = (w_in[layer] * jnp.where(is_q, QUERY_SCALE, 1.0)).astype(BF16)
        u, *qkv_by_dil = _inproj(x2d, g_mix[layer][None], w_in_scaled)
        o_list, lse_list = [], []
        tabs = _bias_tables(rel_bias)
        for bi, ((_, dil), qkv_d, (rq, nq)) in enumerate(zip(DILATED_BRANCHES, qkv_by_dil, ATTN_STEP)):
            o_d, lse_d = _attn_branch(qkv_d, tabs[bi], b, s, dil, rq, nq)
            o_list.append(o_d)
            lse_list.append(lse_d)
        brt = jnp.broadcast_to(b_router[layer][:, None], (N_EXPERTS, LANES))
        x1, h2, pos, gates, pc = _mixout(
            x2d, u, o_list, lse_list, w_pool[layer].astype(BF16), pool_scale[layer][None],
            w_out[layer].astype(BF16), g_ffn[layer][None], w_router[layer].T, brt, s)

        pc16 = pc[:, :, 0].astype(I32)
        pcnt, loff, gstart, tails, seg_end = _routing_tables(pc16)
        n_rows = t * TOP_K + N_EXPERTS * (t // TOKEN_TILE) * ROW_ALIGN + N_EXPERTS * (EXPERT_TILE - 1)
        n_rows = (n_rows + EXPERT_TILE - 1) // EXPERT_TILE * EXPERT_TILE
        n_tiles = n_rows // EXPERT_TILE
        tile_start = jnp.arange(n_tiles, dtype=I32) * EXPERT_TILE
        tile_expert = jnp.minimum(jnp.sum(seg_end[None, :] <= tile_start[:, None], axis=1), N_EXPERTS - 1).astype(I32)
        n_used = (seg_end[-1] // EXPERT_TILE).astype(I32)[None]
        xs = _dispatch(loff, pcnt, gstart, tails, h2, pos, n_rows)
        has_rows = jnp.diff(seg_end, prepend=0) > 0
        experts = jnp.arange(N_EXPERTS, dtype=I32)
        stage_slot = ((jnp.cumsum(has_rows) - 1) % 2).astype(I32)
        next_expert = jnp.min(jnp.where(has_rows[None, :] & (experts[None, :] > experts[:, None]),
                                        experts[None, :], N_EXPERTS), axis=1).astype(I32)
        ys = _experts(tile_expert, n_used, stage_slot, next_expert, xs, w_gate_up[layer],
                      b_gate_up[layer][:, None, :], w_down[layer], b_down[layer][:, None, :])

        assert layer == w_in.shape[0] - 1, "single-layer pipeline: the final norm is fused into combine"
        x2d = _combine(loff, pcnt, gstart, x1, pos, gates, p[layer].reshape(t, PLE_DIM), g_ple[layer][None],
                       w_ple_gate[layer].astype(BF16), w_ple_proj[layer].astype(BF16), g_final[None], ys)
    return x2d.reshape(b, s, d)
```

```python
import functools
import math

import jax
import jax.numpy as jnp
from jax import lax
from jax.experimental import pallas as pl
from jax.experimental.pallas import tpu as pltpu

F32 = jnp.float32
BF16 = jnp.bfloat16
I32 = jnp.int32

D_MODEL = 1024
POOL_WIDTH = 512
POOL_WINDOWS = (2, 4, 8, 16)
POOL_GROUP_DIM = 128
ATTN_WIDTH = 512
QKV_WIDTH = 3 * ATTN_WIDTH
HEAD_DIM = 64
N_HEADS = 8
DILATED_BRANCHES = ((128, 1), (512, 4), (2048, 16))
ATTN_BLOCK = 128
N_REL_BUCKETS = 32
REL_MAX_EXACT = 16
REL_MAX_DISTANCE = 2048
N_EXPERTS = 32
TOP_K = 4
D_EXPERT = 1024
SWIGLU_LIMIT = 7.0
SWIGLU_ALPHA = 1.702
PLE_DIM = 256
NORM_EPS = 1e-6
NEG_INF = -1e30
LOG2_E = math.log2(math.e)
QUERY_SCALE = LOG2_E / math.sqrt(HEAD_DIM)

LANES = 128
SUBLANES = 8
POOL_HALO = 16
TOKEN_TILE = 512
ROW_ALIGN = 16
EXPERT_TILE = 512
EXPERT_CHUNK = 256
LOCAL_ROWS = TOKEN_TILE * TOP_K + N_EXPERTS * ROW_ALIGN
SORT_BLOCK = 256
GATHER_BLOCK = 512
ATTN_STEP = ((1, 8), (1, 8), (4, 2))
VMEM_LIMIT = 56 * 1024 * 1024


def _rms(x, g):
    return x * lax.rsqrt(jnp.mean(x * x, axis=-1, keepdims=True) + NORM_EPS) * g


def _inproj_kernel(x_ref, g_ref, w_ref, u_ref, nat_ref, d4_ref, d16_ref, z_scr, g_scr):
    h = _rms(x_ref[...], g_ref[...]).astype(BF16)
    part = ATTN_WIDTH
    slabs = part // LANES

    def project(i):
        return jnp.dot(h, w_ref[:, pl.ds(i * part, part)], preferred_element_type=F32)

    def regroup(z, i):
        col0 = (i - 1) * part
        nat_ref[:, pl.ds(col0, part)] = z.astype(BF16)
        for c in range(slabs):
            slab = (i - 1) * slabs + c
            cols = col0 + c * LANES
            z_scr[slab] = z[:, c * LANES:(c + 1) * LANES]
            for rl in range(4):
                grp = z_scr[slab, pl.ds(rl, TOKEN_TILE // 4, stride=4), :]
                d4_ref[:, pl.ds(rl * QKV_WIDTH + cols, LANES)] = grp.astype(BF16)
                g_scr[rl * n_slabs + slab] = grp
            for rl in range(4):
                for rh in range(4):
                    d16_ref[:, pl.ds((4 * rh + rl) * QKV_WIDTH + cols, LANES)] = (
                        g_scr[rl * n_slabs + slab, pl.ds(rh, TOKEN_TILE // 16, stride=4), :].astype(BF16))

    assert POOL_WIDTH == part and QKV_WIDTH == 3 * part and [d for _, d in DILATED_BRANCHES] == [1, 4, 16]
    n_slabs = QKV_WIDTH // LANES
    z_prev = project(0)
    for i in range(1, 4):
        z_next = project(i)
        if i == 1:
            u_ref[...] = z_prev
        else:
            regroup(z_prev, i - 1)
        z_prev = z_next
    regroup(z_prev, 3)


def _inproj(x2d, g, w_bf16):
    t = x2d.shape[0]
    in_w = w_bf16.shape[1]
    tt = TOKEN_TILE
    return pl.pallas_call(
        _inproj_kernel,
        grid=(t // tt,),
        in_specs=[
            pl.BlockSpec((tt, D_MODEL), lambda i: (i, 0)),
            pl.BlockSpec((1, D_MODEL), lambda i: (0, 0)),
            pl.BlockSpec((D_MODEL, in_w), lambda i: (0, 0)),
        ],
        out_specs=[
            pl.BlockSpec((tt, POOL_WIDTH), lambda i: (i, 0)),
            pl.BlockSpec((tt, QKV_WIDTH), lambda i: (i, 0)),
            pl.BlockSpec((tt // 4, 4 * QKV_WIDTH), lambda i: (i, 0)),
            pl.BlockSpec((tt // 16, 16 * QKV_WIDTH), lambda i: (i, 0)),
        ],
        out_shape=[
            jax.ShapeDtypeStruct((t, POOL_WIDTH), F32),
            jax.ShapeDtypeStruct((t, QKV_WIDTH), BF16),
            jax.ShapeDtypeStruct((t // 4, 4 * QKV_WIDTH), BF16),
            jax.ShapeDtypeStruct((t // 16, 16 * QKV_WIDTH), BF16),
        ],
        scratch_shapes=[pltpu.VMEM((QKV_WIDTH // LANES, tt, LANES), F32),
                        pltpu.VMEM((4 * QKV_WIDTH // LANES, tt // 4, LANES), F32)],
        compiler_params=pltpu.CompilerParams(
            dimension_semantics=("parallel",), vmem_limit_bytes=VMEM_LIMIT),
        name="inproj",
    )(x2d, g, w_bf16)


def _attn_kernel(main_ref, prev_ref, tab_ref, o_ref, lse_ref, k_scr, vt_scr, *, rq, nq):
    blk = ATTN_BLOCK
    first_group = pl.program_id(2) == 0
    nt = (((1,), (1,)), ((), ()))

    for ri in range(rq):
        base = ri * QKV_WIDTH
        k_scr[0:blk, :] = prev_ref[:, pl.ds(base + ATTN_WIDTH, ATTN_WIDTH)]
        k_scr[blk:, :] = main_ref[:, pl.ds(base + ATTN_WIDTH, ATTN_WIDTH)]
        vt_scr[:, 0:blk] = prev_ref[:, pl.ds(base + 2 * ATTN_WIDTH, ATTN_WIDTH)].astype(F32).T.astype(BF16)
        for j in range(nq):
            vt_scr[:, pl.ds((j + 1) * blk, blk)] = (
                main_ref[pl.ds(j * blk, blk), pl.ds(base + 2 * ATTN_WIDTH, ATTN_WIDTH)].astype(F32).T.astype(BF16))
        for j in range(nq):
            rows = pl.ds(j * blk, blk)
            keys = pl.ds(j * blk, 2 * blk)
            scores = []
            for h in range(N_HEADS):
                q = main_ref[rows, pl.ds(base + h * HEAD_DIM, HEAD_DIM)]
                k2 = k_scr[keys, pl.ds(h * HEAD_DIM, HEAD_DIM)]
                scores.append(lax.dot_general(k2, q, nt, preferred_element_type=F32))
            probs, inv_l, lse_parts = [], [], []
            for h in range(N_HEADS):
                s = scores[h] + tab_ref[h]
                if j == 0:
                    pen = jnp.where(first_group, NEG_INF, 0.0).astype(F32)
                    s = jnp.concatenate([s[:blk] + pen, s[blk:]], axis=0)
                m = jnp.max(s, axis=0, keepdims=True)
                p = jnp.exp2(s - m)
                l = jnp.sum(p, axis=0, keepdims=True)
                probs.append(p.astype(BF16))
                inv_l.append(1.0 / l)
                lse_parts.append(m * math.log(2.0) + jnp.log(l))
            o_parts = []
            for h in range(N_HEADS):
                vt2 = vt_scr[pl.ds(h * HEAD_DIM, HEAD_DIM), keys]
                o_parts.append(jnp.dot(vt2, probs[h], preferred_element_type=F32) * inv_l[h])
            o_t = jnp.concatenate(o_parts, axis=0)
            o_ref[rows, pl.ds(ri * ATTN_WIDTH, ATTN_WIDTH)] = o_t.T.astype(BF16)
            lse_t = jnp.concatenate(lse_parts + [jnp.zeros((LANES - N_HEADS, blk), F32)], axis=0)
            lse_ref[rows, pl.ds(ri * LANES, LANES)] = lse_t.T


def _attn_branch(qkv_d, tab, batch, seq, dil, rq, nq):
    sub = seq // dil
    nb = sub // ATTN_BLOCK
    groups = nb // nq
    return pl.pallas_call(
        functools.partial(_attn_kernel, rq=rq, nq=nq),
        grid=(batch, dil // rq, groups),
        in_specs=[
            pl.BlockSpec((nq * ATTN_BLOCK, rq * QKV_WIDTH), lambda b, r, g: (b * groups + g, r)),
            pl.BlockSpec((ATTN_BLOCK, rq * QKV_WIDTH), lambda b, r, g: (b * nb + jnp.maximum(g * nq - 1, 0), r)),
            pl.BlockSpec((N_HEADS, 2 * ATTN_BLOCK, ATTN_BLOCK), lambda b, r, g: (0, 0, 0)),
        ],
        out_specs=[
            pl.BlockSpec((nq * ATTN_BLOCK, rq * ATTN_WIDTH), lambda b, r, g: (b * groups + g, r)),
            pl.BlockSpec((nq * ATTN_BLOCK, rq * LANES), lambda b, r, g: (b * groups + g, r)),
        ],
        out_shape=[
            jax.ShapeDtypeStruct((batch * sub, dil * ATTN_WIDTH), BF16),
            jax.ShapeDtypeStruct((batch * sub, dil * LANES), F32),
        ],
        scratch_shapes=[
            pltpu.VMEM(((nq + 1) * ATTN_BLOCK, ATTN_WIDTH), BF16),
            pltpu.VMEM((ATTN_WIDTH, (nq + 1) * ATTN_BLOCK), BF16),
        ],
        compiler_params=pltpu.CompilerParams(
            dimension_semantics=("parallel", "parallel", "parallel"), vmem_limit_bytes=VMEM_LIMIT),
        name="attn",
    )(qkv_d, qkv_d, tab)


def _t5_bucket(dist):
    n = jnp.maximum(dist, 1).astype(F32)
    large = REL_MAX_EXACT + (jnp.log(n / REL_MAX_EXACT) / math.log(REL_MAX_DISTANCE / REL_MAX_EXACT)
                             * (N_REL_BUCKETS - REL_MAX_EXACT)).astype(I32)
    large = jnp.minimum(large, N_REL_BUCKETS - 1)
    return jnp.where(dist < REL_MAX_EXACT, dist, large)


def _shifted_rows(w, n):
    lead = w.shape[:-1]
    width = w.shape[-1]
    flat = jnp.tile(w, (1,) * len(lead) + (n + 1,))[..., :n * (width + 1)]
    return flat.reshape(lead + (n, width + 1))[..., :n]


def _bias_tables(rel_bias):
    blk = ATTN_BLOCK
    assert all(window // dil == blk for window, dil in DILATED_BRANCHES)
    n_br = len(DILATED_BRANCHES)
    dist = jnp.stack([jnp.arange(blk + 1) * dil for _, dil in DILATED_BRANCHES])
    f = jnp.swapaxes(rel_bias[_t5_bucket(dist)], 1, 2).astype(F32)
    neg = jnp.full((n_br, N_HEADS, blk - 1), NEG_INF, F32)
    neg1 = jnp.full((n_br, N_HEADS, 1), NEG_INF, F32)
    w_prev = jnp.concatenate([neg, f[..., :0:-1], neg1], axis=-1)
    prev = _shifted_rows(w_prev, blk)[..., ::-1]
    w_cur = jnp.concatenate([neg, f[..., :blk], neg1], axis=-1)
    cur = _shifted_rows(w_cur, blk)[..., ::-1, :]
    return jnp.concatenate([prev, cur], axis=-2) * LOG2_E


def _mixout_kernel(x_ref, u_ref, uh_ref, o1_ref, o4_ref, o16_ref, l1_ref, l4_ref, l16_ref,
                   wpool_ref, pscale_ref, wout_ref, gffn_ref, wrt_ref, brt_ref, spread_ref,
                   x1_ref, h2_ref, pos_ref, gate_ref, pc_ref,
                   ext_scr, mix_scr, o_scr, l_scr, *, tiles_per_seq):
    i = pl.program_id(0)
    tt = TOKEN_TILE
    seq_tile = i % tiles_per_seq

    halo = uh_ref[...]
    ext_scr[0:POOL_HALO, :] = jnp.where(seq_tile == 0, jnp.zeros_like(halo), halo)
    ext_scr[POOL_HALO:, :] = u_ref[...]
    tpos = seq_tile * tt + lax.broadcasted_iota(I32, (tt, 1), 0)
    for gi, w in enumerate(POOL_WINDOWS):
        cols = pl.ds(gi * POOL_GROUP_DIM, POOL_GROUP_DIM)
        run = ext_scr[:, cols]
        shift = 1
        while shift < w:
            run = run + pltpu.roll(run, shift, axis=0)
            shift *= 2
        assert shift == w and w - 1 < POOL_HALO
        acc = run[POOL_HALO:]
        tok = ext_scr[pl.ds(POOL_HALO, tt), cols]
        cnt = jnp.minimum(tpos + 1, w).astype(F32)
        pooled = (acc / cnt - tok).astype(BF16)
        mixed = jnp.dot(pooled, wpool_ref[gi], preferred_element_type=F32) * pscale_ref[:, cols]
        mix_scr[:, cols] = mixed.astype(BF16)

    o_slabs = ATTN_WIDTH // LANES
    for bi, (dil, o_ref, l_ref) in enumerate(((4, o4_ref, l4_ref), (16, o16_ref, l16_ref))):
        rows = tt // dil
        for r in range(dil):
            l_scr[bi, pl.ds(r, rows, stride=dil), :] = l_ref[:, pl.ds(r * LANES, LANES)]
            for c in range(o_slabs):
                o_scr[bi * o_slabs + c, pl.ds(r, rows, stride=dil), :] = (
                    o_ref[:, pl.ds(r * ATTN_WIDTH + c * LANES, LANES)].astype(F32))
    l1, l4, l16 = l1_ref[...], l_scr[0], l_scr[1]
    lm = jnp.maximum(jnp.maximum(l1, l4), l16)
    e1, e4, e16 = jnp.exp(l1 - lm), jnp.exp(l4 - lm), jnp.exp(l16 - lm)
    inv = 1.0 / (e1 + e4 + e16)

    def per_feature(w):
        hi = w.astype(BF16)
        lo = (w - hi.astype(F32)).astype(BF16)
        return (jnp.dot(hi, spread_ref[...], preferred_element_type=F32)
                + jnp.dot(lo, spread_ref[...], preferred_element_type=F32))

    w1, w4, w16 = per_feature(e1 * inv), per_feature(e4 * inv), per_feature(e16 * inv)
    for c in range(o_slabs):
        sl = pl.ds(c * LANES, LANES)
        cs = slice(c * LANES, (c + 1) * LANES)
        o = w1[:, cs] * o1_ref[:, sl].astype(F32) + w4[:, cs] * o_scr[c] + w16[:, cs] * o_scr[o_slabs + c]
        mix_scr[:, pl.ds(POOL_WIDTH + c * LANES, LANES)] = o.astype(BF16)

    x1 = x_ref[...] + jnp.dot(mix_scr[...], wout_ref[...], preferred_element_type=F32)
    x1_ref[...] = x1

    h2 = _rms(x1, gffn_ref[...])
    h2_ref[...] = h2.astype(BF16)
    nt = (((1,), (1,)), ((), ()))
    wr = wrt_ref[...]
    wr_hi = wr.astype(BF16)
    wr_lo = (wr - wr_hi.astype(F32)).astype(BF16)
    h2_hi = h2.astype(BF16)
    h2_lo = (h2 - h2_hi.astype(F32)).astype(BF16)
    logits = (lax.dot_general(wr_hi, h2_hi, nt, preferred_element_type=F32)
              + lax.dot_general(wr_hi, h2_lo, nt, preferred_element_type=F32)
              + lax.dot_general(wr_lo, h2_hi, nt, preferred_element_type=F32))
    logits = logits + brt_ref[:, 0:1]
    eio = lax.broadcasted_iota(I32, (N_EXPERTS, tt), 0)
    sel, val = [], []
    for _ in range(TOP_K):
        m = jnp.max(logits, axis=0, keepdims=True)
        idx = jnp.min(jnp.where(logits == m, eio, N_EXPERTS), axis=0, keepdims=True)
        sel.append(idx)
        val.append(m)
        logits = jnp.where(eio == idx, -jnp.inf, logits)
    ex = [jnp.exp(v - val[0]) for v in val]
    den = ex[0] + ex[1] + ex[2] + ex[3]
    gates = [e / den for e in ex]

    hot = [(eio == s) for s in sel]
    onehot = (hot[0] | hot[1] | hot[2] | hot[3]).astype(F32)
    ti = lax.broadcasted_iota(I32, (tt, tt), 0)
    tj = lax.broadcasted_iota(I32, (tt, tt), 1)
    before = (ti < tj).astype(BF16)
    rank = jnp.dot(onehot.astype(BF16), before, preferred_element_type=F32)
    cnt = jnp.sum(onehot, axis=1, keepdims=True)
    pc = jnp.maximum(jnp.floor((cnt + (ROW_ALIGN - 1)) * (1.0 / ROW_ALIGN)), 1.0)
    pcb = jnp.broadcast_to(pc, (N_EXPERTS, LANES))
    pc_ref[0] = pcb
    pcm = jnp.concatenate([pcb, jnp.zeros((LANES - N_EXPERTS, LANES), F32)], axis=0).astype(BF16)
    li = lax.broadcasted_iota(I32, (N_EXPERTS, LANES), 0)
    lj = lax.broadcasted_iota(I32, (N_EXPERTS, LANES), 1)
    lower = (lj < li).astype(BF16)
    off = jnp.dot(lower, pcm, preferred_element_type=F32)[:, 0:1] * float(ROW_ALIGN)
    where_to = off + rank
    zero_i = jnp.zeros((1, tt), I32)
    zero_f = jnp.zeros((1, tt), F32)
    pos_rows = [jnp.sum(jnp.where(hk, where_to, 0.0), axis=0, keepdims=True).astype(I32) for hk in hot]
    pos_ref[0] = jnp.concatenate(pos_rows + [zero_i] * (SUBLANES - TOP_K), axis=0)
    gate_ref[0] = jnp.concatenate(gates + [zero_f] * (SUBLANES - TOP_K), axis=0)


def _mixout(x2d, u, o_list, lse_list, wpool, pscale, wout, gffn, wrt, brt, seq):
    t = x2d.shape[0]
    tt = TOKEN_TILE
    n_tt = t // tt
    halo_blocks = tt // POOL_HALO
    row = lambda i: (i, 0)
    const2 = lambda i: (0, 0)
    dils = [dil for _, dil in DILATED_BRANCHES]
    spread = (jnp.arange(LANES)[:, None] == jnp.arange(ATTN_WIDTH)[None, :] // HEAD_DIM).astype(BF16)
    return pl.pallas_call(
        functools.partial(_mixout_kernel, tiles_per_seq=seq // tt),
        grid=(n_tt,),
        in_specs=[
            pl.BlockSpec((tt, D_MODEL), row),
            pl.BlockSpec((tt, POOL_WIDTH), row),
            pl.BlockSpec((POOL_HALO, POOL_WIDTH), lambda i: (jnp.maximum(i * halo_blocks - 1, 0), 0)),
            *[pl.BlockSpec((tt // dil, dil * ATTN_WIDTH), row) for dil in dils],
            *[pl.BlockSpec((tt // dil, dil * LANES), row) for dil in dils],
            pl.BlockSpec((len(POOL_WINDOWS), POOL_GROUP_DIM, POOL_GROUP_DIM), lambda i: (0, 0, 0)),
            pl.BlockSpec((1, POOL_WIDTH), const2),
            pl.BlockSpec((D_MODEL, D_MODEL), const2),
            pl.BlockSpec((1, D_MODEL), const2),
            pl.BlockSpec((N_EXPERTS, D_MODEL), const2),
            pl.BlockSpec((N_EXPERTS, LANES), const2),
            pl.BlockSpec((LANES, ATTN_WIDTH), const2),
        ],
        out_specs=[
            pl.BlockSpec((tt, D_MODEL), row),
            pl.BlockSpec((tt, D_MODEL), row),
            pl.BlockSpec((1, SUBLANES, tt), lambda i: (i, 0, 0)),
            pl.BlockSpec((1, SUBLANES, tt), lambda i: (i, 0, 0)),
            pl.BlockSpec((1, N_EXPERTS, LANES), lambda i: (i, 0, 0)),
        ],
        out_shape=[
            jax.ShapeDtypeStruct((t, D_MODEL), F32),
            jax.ShapeDtypeStruct((t, D_MODEL), BF16),
            jax.ShapeDtypeStruct((n_tt, SUBLANES, tt), I32),
            jax.ShapeDtypeStruct((n_tt, SUBLANES, tt), F32),
            jax.ShapeDtypeStruct((n_tt, N_EXPERTS, LANES), F32),
        ],
        scratch_shapes=[
            pltpu.VMEM((tt + POOL_HALO, POOL_WIDTH), F32),
            pltpu.VMEM((tt, D_MODEL), BF16),
            pltpu.VMEM((2 * ATTN_WIDTH // LANES, tt, LANES), F32),
            pltpu.VMEM((2, tt, LANES), F32),
        ],
        compiler_params=pltpu.CompilerParams(
            dimension_semantics=("parallel",), vmem_limit_bytes=VMEM_LIMIT),
        name="mixout",
    )(x2d, u, u, *o_list, *lse_list, wpool, pscale, wout, gffn, wrt, brt, spread)


def _chunk_copies(loff_s, pcnt_s, gstart_s, tile, local_ref, global_ref, sems, to_global):
    slot = tile % 2
    out = []
    for e in range(N_EXPERTS):
        n = pl.multiple_of(pcnt_s[tile, e], ROW_ALIGN)
        lo = pl.multiple_of(loff_s[tile, e], ROW_ALIGN)
        go = pl.multiple_of(gstart_s[tile, e], ROW_ALIGN)
        loc = local_ref.at[slot, pl.ds(lo, n)]
        glo = global_ref.at[pl.ds(go, n)]
        src, dst = (loc, glo) if to_global else (glo, loc)
        out.append(pltpu.make_async_copy(src, dst, sems.at[slot]))
    return out


def _start_all(copies):
    for cp in copies:
        cp.start()


def _wait_chunks(loff_s, pcnt_s, tile, local_ref, global_ref, sems, to_global):
    slot = tile % 2
    rows = pl.multiple_of(loff_s[tile, N_EXPERTS - 1] + pcnt_s[tile, N_EXPERTS - 1], ROW_ALIGN)
    loc = local_ref.at[slot, pl.ds(0, rows)]
    glo = global_ref.at[pl.ds(0, rows)]
    src, dst = (loc, glo) if to_global else (glo, loc)
    pltpu.make_async_copy(src, dst, sems.at[slot]).wait()


def _dispatch_kernel(loff_s, pcnt_s, gstart_s, tail_s, h2_ref, pos_ref, xs_ref, loc_scr, zero_scr, sem, tail_sem,
                     spare_sem):
    i = pl.program_id(0)
    tt = TOKEN_TILE

    @pl.when(i == 0)
    def _():
        zero_scr[...] = jnp.zeros_like(zero_scr)
        tails = []
        for e in range(N_EXPERTS):
            n = pl.multiple_of(tail_s[1, e], ROW_ALIGN)
            start = pl.multiple_of(tail_s[0, e], ROW_ALIGN)
            tails.append((n > 0, pltpu.make_async_copy(zero_scr.at[pl.ds(0, n)], xs_ref.at[pl.ds(start, n)], tail_sem)))
        for cond, cp in tails:
            pl.when(cond)(cp.start)
        for cond, cp in tails:
            pl.when(cond)(cp.wait)

    def spare_tile(j):
        return pltpu.make_async_copy(
            zero_scr, xs_ref.at[pl.ds(pl.multiple_of(j * EXPERT_TILE, EXPERT_TILE), EXPERT_TILE)], spare_sem)

    n_tiles = xs_ref.shape[0] // EXPERT_TILE

    @pl.when(i == 0)
    def _():
        lax.fori_loop(tail_s[2, 0], n_tiles, lambda j, c: (spare_tile(j).start(), c)[1], 0)

    @pl.when(i == pl.num_programs(0) - 1)
    def _():
        lax.fori_loop(tail_s[2, 0], n_tiles, lambda j, c: (spare_tile(j).wait(), c)[1], 0)

    pos = pos_ref[0]
    h2 = h2_ref[...]
    slot = i % 2
    used_rows = loff_s[i, N_EXPERTS - 1] + pcnt_s[i, N_EXPERTS - 1]
    i16 = jnp.int16
    row16 = lax.broadcasted_iota(I32, (SORT_BLOCK, tt), 0).astype(i16)
    pos16 = [jnp.broadcast_to(pos[k:k + 1], (SORT_BLOCK, tt)).astype(i16) for k in range(TOP_K)]

    def one_hot(jb):
        jio = row16 + jnp.asarray(jb * SORT_BLOCK, i16)
        hit = (jio == pos16[0]) | (jio == pos16[1]) | (jio == pos16[2]) | (jio == pos16[3])
        return jnp.where(hit, jnp.asarray(1.0, BF16), jnp.asarray(0.0, BF16))

    def sort_block(jb, onehot):
        loc_scr[slot, pl.ds(jb * SORT_BLOCK, SORT_BLOCK), :] = jnp.dot(
            onehot, h2, preferred_element_type=F32).astype(BF16)

    n_sure = TOKEN_TILE * TOP_K // SORT_BLOCK
    onehot = one_hot(0)
    for jb in range(n_sure):
        nxt = one_hot(jb + 1) if jb + 1 < n_sure else None
        sort_block(jb, onehot)
        onehot = nxt
    for jb in range(n_sure, LOCAL_ROWS // SORT_BLOCK):
        pl.when(jb * SORT_BLOCK < used_rows)(lambda jb=jb: sort_block(jb, one_hot(jb)))

    @pl.when(i > 0)
    def _():
        _wait_chunks(loff_s, pcnt_s, i - 1, loc_scr, xs_ref, sem, True)

    _start_all(_chunk_copies(loff_s, pcnt_s, gstart_s, i, loc_scr, xs_ref, sem, True))

    @pl.when(i == pl.num_programs(0) - 1)
    def _():
        _wait_chunks(loff_s, pcnt_s, i, loc_scr, xs_ref, sem, True)


def _dispatch(loff, pcnt, gstart, tails, h2, pos, n_rows):
    t = h2.shape[0]
    tt = TOKEN_TILE
    grid_spec = pltpu.PrefetchScalarGridSpec(
        num_scalar_prefetch=4,
        grid=(t // tt,),
        in_specs=[
            pl.BlockSpec((tt, D_MODEL), lambda i, *_: (i, 0)),
            pl.BlockSpec((1, SUBLANES, tt), lambda i, *_: (i, 0, 0)),
        ],
        out_specs=pl.BlockSpec(memory_space=pl.ANY),
        scratch_shapes=[
            pltpu.VMEM((2, LOCAL_ROWS, D_MODEL), BF16),
            pltpu.VMEM((EXPERT_TILE, D_MODEL), BF16),
            pltpu.SemaphoreType.DMA((2,)),
            pltpu.SemaphoreType.DMA(()),
            pltpu.SemaphoreType.DMA(()),
        ],
    )
    return pl.pallas_call(
        _dispatch_kernel,
        grid_spec=grid_spec,
        out_shape=jax.ShapeDtypeStruct((n_rows, D_MODEL), BF16),
        compiler_params=pltpu.CompilerParams(
            dimension_semantics=("arbitrary",), vmem_limit_bytes=VMEM_LIMIT),
        name="dispatch",
    )(loff, pcnt, gstart, tails, h2, pos)


def _expert_kernel(te_s, nu_s, slot_s, next_s, xs_ref, wgu_hbm, bgu_ref, wd_hbm, bd_ref, ys_ref,
                   wgu_stage, wd_stage, wgu_bf, wd_bf, act_scr, wsem):
    i = pl.program_id(0)
    live = i < nu_s[0]
    expert = te_s[i]
    new_expert = (i == 0) | (expert != te_s[jnp.maximum(i - 1, 0)])

    def weight_copies(e, slot):
        return (pltpu.make_async_copy(wgu_hbm.at[e], wgu_stage.at[slot], wsem.at[0, slot]),
                pltpu.make_async_copy(wd_hbm.at[e], wd_stage.at[slot], wsem.at[1, slot]))

    @pl.when(live & new_expert)
    def _():
        slot = slot_s[expert]

        @pl.when(i == 0)
        def _():
            for cp in weight_copies(expert, slot):
                cp.start()

        for cp in weight_copies(expert, slot):
            cp.wait()
        wgu_bf[...] = wgu_stage[slot].astype(BF16)
        wd_bf[...] = wd_stage[slot].astype(BF16)
        upcoming = next_s[expert]

        @pl.when(upcoming < N_EXPERTS)
        def _():
            for cp in weight_copies(upcoming, 1 - slot):
                cp.start()

    @pl.when(live)
    def _():
        x = xs_ref[...]
        n_chunks = D_EXPERT // EXPERT_CHUNK

        def gate_up(c):
            gc = pl.ds(c * EXPERT_CHUNK, EXPERT_CHUNK)
            uc = pl.ds(D_EXPERT + c * EXPERT_CHUNK, EXPERT_CHUNK)
            return (jnp.dot(x, wgu_bf[:, gc], preferred_element_type=F32) + bgu_ref[0, :, gc],
                    jnp.dot(x, wgu_bf[:, uc], preferred_element_type=F32) + bgu_ref[0, :, uc])

        g, u = gate_up(0)
        for c in range(n_chunks):
            nxt = gate_up(c + 1) if c + 1 < n_chunks else None
            g = jnp.minimum(g, SWIGLU_LIMIT)
            u = jnp.clip(u, -SWIGLU_LIMIT, SWIGLU_LIMIT)
            act_scr[:, pl.ds(c * EXPERT_CHUNK, EXPERT_CHUNK)] = (
                (u + 1.0) * (g * jax.nn.sigmoid(SWIGLU_ALPHA * g))).astype(BF16)
            if nxt is not None:
                g, u = nxt
        y = jnp.dot(act_scr[...], wd_bf[...], preferred_element_type=F32) + bd_ref[0]
        ys_ref[...] = y.astype(BF16)

    @pl.when(jnp.logical_not(live))
    def _():
        ys_ref[...] = jnp.zeros_like(ys_ref)


def _experts(tile_expert, n_used, stage_slot, next_expert, xs, wgu, bgu, wd, bd):
    n_rows = xs.shape[0]
    tm = EXPERT_TILE
    live = lambda i, te, nu: jnp.minimum(i, nu[0] - 1)
    grid_spec = pltpu.PrefetchScalarGridSpec(
        num_scalar_prefetch=4,
        grid=(n_rows // tm,),
        in_specs=[
            pl.BlockSpec((tm, D_MODEL), lambda i, te, nu, *_: (live(i, te, nu), 0)),
            pl.BlockSpec(memory_space=pl.ANY),
            pl.BlockSpec((1, 1, 2 * D_EXPERT), lambda i, te, nu, *_: (te[live(i, te, nu)], 0, 0)),
            pl.BlockSpec(memory_space=pl.ANY),
            pl.BlockSpec((1, 1, D_MODEL), lambda i, te, nu, *_: (te[live(i, te, nu)], 0, 0)),
        ],
        out_specs=pl.BlockSpec((tm, D_MODEL), lambda i, te, nu, *_: (i, 0)),
        scratch_shapes=[
            pltpu.VMEM((2, D_MODEL, 2 * D_EXPERT), F32),
            pltpu.VMEM((2, D_EXPERT, D_MODEL), F32),
            pltpu.VMEM((D_MODEL, 2 * D_EXPERT), BF16),
            pltpu.VMEM((D_EXPERT, D_MODEL), BF16),
            pltpu.VMEM((tm, D_EXPERT), BF16),
            pltpu.SemaphoreType.DMA((2, 2)),
        ],
    )
    return pl.pallas_call(
        _expert_kernel,
        grid_spec=grid_spec,
        out_shape=jax.ShapeDtypeStruct((n_rows, D_MODEL), BF16),
        compiler_params=pltpu.CompilerParams(
            dimension_semantics=("arbitrary",), vmem_limit_bytes=VMEM_LIMIT),
        name="experts",
    )(tile_expert, n_used, stage_slot, next_expert, xs, wgu, bgu, wd, bd)


def _combine_kernel(loff_s, pcnt_s, gstart_s, x1_ref, pos_ref, gate_ref, p_ref, gple_ref, wg_ref, wp_ref,
                    gfin_ref, ys_ref, out_ref, loc_scr, sem):
    i = pl.program_id(0)
    tt = TOKEN_TILE

    @pl.when(i == 0)
    def _():
        loc_scr[...] = jnp.zeros_like(loc_scr)
        _start_all(_chunk_copies(loff_s, pcnt_s, gstart_s, i, loc_scr, ys_ref, sem, False))

    _wait_chunks(loff_s, pcnt_s, i, loc_scr, ys_ref, sem, False)

    @pl.when(i + 1 < pl.num_programs(0))
    def _():
        _start_all(_chunk_copies(loff_s, pcnt_s, gstart_s, i + 1, loc_scr, ys_ref, sem, False))

    slot = i % 2

    def to_columns(rows):
        return jnp.concatenate([rows, jnp.zeros((LANES - rows.shape[0], tt), F32)], axis=0).T

    post = to_columns(pos_ref[0].astype(F32))
    gatet = to_columns(gate_ref[0])

    def gate_block(jb):
        jio = (lax.broadcasted_iota(I32, (tt, GATHER_BLOCK), 1) + jb * GATHER_BLOCK).astype(F32)
        w = jnp.zeros((tt, GATHER_BLOCK), F32)
        for k in range(TOP_K):
            w = jnp.where(jio == post[:, k:k + 1], gatet[:, k:k + 1], w)
        return w.astype(BF16)

    n_blocks = LOCAL_ROWS // GATHER_BLOCK
    moe = jnp.zeros((tt, D_MODEL), F32)
    w = gate_block(0)
    for jb in range(n_blocks):
        nxt = gate_block(jb + 1) if jb + 1 < n_blocks else None
        y = loc_scr[slot, pl.ds(jb * GATHER_BLOCK, GATHER_BLOCK), :]
        moe = moe + jnp.dot(w, y, preferred_element_type=F32)
        w = nxt

    x2 = x1_ref[...] + moe
    gate = jax.nn.sigmoid(jnp.dot(_rms(x2, gple_ref[...]).astype(BF16), wg_ref[...], preferred_element_type=F32))
    emb = jnp.dot(p_ref[...].astype(BF16), wp_ref[...], preferred_element_type=F32)
    out_ref[...] = _rms(x2 + emb * gate, gfin_ref[...])


def _combine(loff, pcnt, gstart, x1, pos, gates, p2d, gple, wg, wp, gfin, ys):
    t = x1.shape[0]
    tt = TOKEN_TILE
    row = lambda i, *_: (i, 0)
    const2 = lambda i, *_: (0, 0)
    grid_spec = pltpu.PrefetchScalarGridSpec(
        num_scalar_prefetch=3,
        grid=(t // tt,),
        in_specs=[
            pl.BlockSpec((tt, D_MODEL), row),
            pl.BlockSpec((1, SUBLANES, tt), lambda i, *_: (i, 0, 0)),
            pl.BlockSpec((1, SUBLANES, tt), lambda i, *_: (i, 0, 0)),
            pl.BlockSpec((tt, PLE_DIM), row),
            pl.BlockSpec((1, D_MODEL), const2),
            pl.BlockSpec((D_MODEL, D_MODEL), const2),
            pl.BlockSpec((PLE_DIM, D_MODEL), const2),
            pl.BlockSpec((1, D_MODEL), const2),
            pl.BlockSpec(memory_space=pl.ANY),
        ],
        out_specs=pl.BlockSpec((tt, D_MODEL), row),
        scratch_shapes=[pltpu.VMEM((2, LOCAL_ROWS, D_MODEL), BF16), pltpu.SemaphoreType.DMA((2,))],
    )
    return pl.pallas_call(
        _combine_kernel,
        grid_spec=grid_spec,
        out_shape=jax.ShapeDtypeStruct((t, D_MODEL), F32),
        compiler_params=pltpu.CompilerParams(
            dimension_semantics=("arbitrary",), vmem_limit_bytes=VMEM_LIMIT),
        name="combine",
    )(loff, pcnt, gstart, x1, pos, gates, p2d, gple, wg, wp, gfin, ys)


def _routing_tables(pc16):
    pcnt = pc16 * ROW_ALIGN
    loff = jnp.cumsum(pcnt, axis=1) - pcnt
    seg = jnp.sum(pcnt, axis=0)
    segpad = (seg + EXPERT_TILE - 1) // EXPERT_TILE * EXPERT_TILE
    seg_end = jnp.cumsum(segpad)
    ebase = seg_end - segpad
    gstart = ebase[None, :] + jnp.cumsum(pcnt, axis=0) - pcnt
    tails = jnp.stack([ebase + seg, segpad - seg, jnp.broadcast_to(seg_end[-1] // EXPERT_TILE, seg.shape)])
    return pcnt, loff, gstart, tails, seg_end


def kernel(x, p, g_mix, w_in, w_pool, pool_scale, rel_bias, w_out, g_ffn, w_router, b_router, w_gate_up,
           b_gate_up, w_down, b_down, g_ple, w_ple_gate, w_ple_proj, g_final):
    b, s, d = x.shape
    t = b * s
    x2d = x.reshape(t, d)
    for layer in range(w_in.shape[0]):
        col = jnp.arange(w_in.shape[2])
        is_q = (col >= POOL_WIDTH) & (col < POOL_WIDTH + ATTN_WIDTH)
        w_in_scaled = (w_in[layer] * jnp.where(is_q, QUERY_SCALE, 1.0)).astype(BF16)
        u, *qkv_by_dil = _inproj(x2d, g_mix[layer][None], w_in_scaled)
        o_list, lse_list = [], []
        tabs = _bias_tables(rel_bias)
        for bi, ((_, dil), qkv_d, (rq, nq)) in enumerate(zip(DILATED_BRANCHES, qkv_by_dil, ATTN_STEP)):
            o_d, lse_d = _attn_branch(qkv_d, tabs[bi], b, s, dil, rq, nq)
            o_list.append(o_d)
            lse_list.append(lse_d)
        brt = jnp.broadcast_to(b_router[layer][:, None], (N_EXPERTS, LANES))
        x1, h2, pos, gates, pc = _mixout(
            x2d, u, o_list, lse_list, w_pool[layer].astype(BF16), pool_scale[layer][None],
            w_out[layer].astype(BF16), g_ffn[layer][None], w_router[layer].T, brt, s)

        pc16 = pc[:, :, 0].astype(I32)
        pcnt, loff, gstart, tails, seg_end = _routing_tables(pc16)
        n_rows = t * TOP_K + N_EXPERTS * (t // TOKEN_TILE) * ROW_ALIGN + N_EXPERTS * (EXPERT_TILE - 1)
        n_rows = (n_rows + EXPERT_TILE - 1) // EXPERT_TILE * EXPERT_TILE
        n_tiles = n_rows // EXPERT_TILE
        tile_start = jnp.arange(n_tiles, dtype=I32) * EXPERT_TILE
        tile_expert = jnp.minimum(jnp.sum(seg_end[None, :] <= tile_start[:, None], axis=1), N_EXPERTS - 1).astype(I32)
        n_used = (seg_end[-1] // EXPERT_TILE).astype(I32)[None]
        xs = _dispatch(loff, pcnt, gstart, tails, h2, pos, n_rows)
        has_rows = jnp.diff(seg_end, prepend=0) > 0
        experts = jnp.arange(N_EXPERTS, dtype=I32)
        stage_slot = ((jnp.cumsum(has_rows) - 1) % 2).astype(I32)
        next_expert = jnp.min(jnp.where(has_rows[None, :] & (experts[None, :] > experts[:, None]),
                                        experts[None, :], N_EXPERTS), axis=1).astype(I32)
        ys = _experts(tile_expert, n_used, stage_slot, next_expert, xs, w_gate_up[layer],
                      b_gate_up[layer][:, None, :], w_down[layer], b_down[layer][:, None, :])

        assert layer == w_in.shape[0] - 1, "single-layer pipeline: the final norm is fused into combine"
        x2d = _combine(loff, pcnt, gstart, x1, pos, gates, p[layer].reshape(t, PLE_DIM), g_ple[layer][None],
                       w_ple_gate[layer].astype(BF16), w_ple_proj[layer].astype(BF16), g_final[None], ys)
    return x2d.reshape(b, s, d)
```

```python
import functools
import math

import jax
import jax.numpy as jnp
from jax import lax
from jax.experimental import pallas as pl
from jax.experimental.pallas import tpu as pltpu

F32 = jnp.float32
BF16 = jnp.bfloat16
I32 = jnp.int32

D_MODEL = 1024
POOL_WIDTH = 512
POOL_WINDOWS = (2, 4, 8, 16)
POOL_GROUP_DIM = 128
ATTN_WIDTH = 512
QKV_WIDTH = 3 * ATTN_WIDTH
HEAD_DIM = 64
N_HEADS = 8
DILATED_BRANCHES = ((128, 1), (512, 4), (2048, 16))
ATTN_BLOCK = 128
N_REL_BUCKETS = 32
REL_MAX_EXACT = 16
REL_MAX_DISTANCE = 2048
N_EXPERTS = 32
TOP_K = 4
D_EXPERT = 1024
SWIGLU_LIMIT = 7.0
SWIGLU_ALPHA = 1.702
PLE_DIM = 256
NORM_EPS = 1e-6
NEG_INF = -1e30
LOG2_E = math.log2(math.e)
QUERY_SCALE = LOG2_E / math.sqrt(HEAD_DIM)

LANES = 128
SUBLANES = 8
POOL_HALO = 16
TOKEN_TILE = 512
ROW_ALIGN = 16
EXPERT_TILE = 512
EXPERT_CHUNK = 256
LOCAL_ROWS = TOKEN_TILE * TOP_K + N_EXPERTS * ROW_ALIGN
SORT_BLOCK = 256
GATHER_BLOCK = 512
ATTN_STEP = ((1, 8), (1, 8), (4, 2))
VMEM_LIMIT = 56 * 1024 * 1024


def _rms(x, g):
    return x * lax.rsqrt(jnp.mean(x * x, axis=-1, keepdims=True) + NORM_EPS) * g


def _inproj_kernel(x_ref, g_ref, w_ref, u_ref, nat_ref, d4_ref, d16_ref, z_scr, g_scr):
    h = _rms(x_ref[...], g_ref[...]).astype(BF16)
    part = ATTN_WIDTH
    slabs = part // LANES

    def project(i):
        return jnp.dot(h, w_ref[:, pl.ds(i * part, part)], preferred_element_type=F32)

    def regroup(z, i):
        col0 = (i - 1) * part
        nat_ref[:, pl.ds(col0, part)] = z.astype(BF16)
        for c in range(slabs):
            slab = (i - 1) * slabs + c
            cols = col0 + c * LANES
            z_scr[slab] = z[:, c * LANES:(c + 1) * LANES]
            for rl in range(4):
                grp = z_scr[slab, pl.ds(rl, TOKEN_TILE // 4, stride=4), :]
                d4_ref[:, pl.ds(rl * QKV_WIDTH + cols, LANES)] = grp.astype(BF16)
                g_scr[rl * n_slabs + slab] = grp
            for rl in range(4):
                for rh in range(4):
                    d16_ref[:, pl.ds((4 * rh + rl) * QKV_WIDTH + cols, LANES)] = (
                        g_scr[rl * n_slabs + slab, pl.ds(rh, TOKEN_TILE // 16, stride=4), :].astype(BF16))

    assert POOL_WIDTH == part and QKV_WIDTH == 3 * part and [d for _, d in DILATED_BRANCHES] == [1, 4, 16]
    n_slabs = QKV_WIDTH // LANES
    z_prev = project(0)
    for i in range(1, 4):
        z_next = project(i)
        if i == 1:
            u_ref[...] = z_prev
        else:
            regroup(z_prev, i - 1)
        z_prev = z_next
    regroup(z_prev, 3)


def _inproj(x2d, g, w_bf16):
    t = x2d.shape[0]
    in_w = w_bf16.shape[1]
    tt = TOKEN_TILE
    return pl.pallas_call(
        _inproj_kernel,
        grid=(t // tt,),
        in_specs=[
            pl.BlockSpec((tt, D_MODEL), lambda i: (i, 0)),
            pl.BlockSpec((1, D_MODEL), lambda i: (0, 0)),
            pl.BlockSpec((D_MODEL, in_w), lambda i: (0, 0)),
        ],
        out_specs=[
            pl.BlockSpec((tt, POOL_WIDTH), lambda i: (i, 0)),
            pl.BlockSpec((tt, QKV_WIDTH), lambda i: (i, 0)),
            pl.BlockSpec((tt // 4, 4 * QKV_WIDTH), lambda i: (i, 0)),
            pl.BlockSpec((tt // 16, 16 * QKV_WIDTH), lambda i: (i, 0)),
        ],
        out_shape=[
            jax.ShapeDtypeStruct((t, POOL_WIDTH), F32),
            jax.ShapeDtypeStruct((t, QKV_WIDTH), BF16),
            jax.ShapeDtypeStruct((t // 4, 4 * QKV_WIDTH), BF16),
            jax.ShapeDtypeStruct((t // 16, 16 * QKV_WIDTH), BF16),
        ],
        scratch_shapes=[pltpu.VMEM((QKV_WIDTH // LANES, tt, LANES), F32),
                        pltpu.VMEM((4 * QKV_WIDTH // LANES, tt // 4, LANES), F32)],
        compiler_params=pltpu.CompilerParams(
            dimension_semantics=("parallel",), vmem_limit_bytes=VMEM_LIMIT),
        name="inproj",
    )(x2d, g, w_bf16)


def _attn_kernel(main_ref, prev_ref, tab_ref, o_ref, lse_ref, k_scr, vt_scr, *, rq, nq):
    blk = ATTN_BLOCK
    first_group = pl.program_id(2) == 0
    nt = (((1,), (1,)), ((), ()))

    for ri in range(rq):
        base = ri * QKV_WIDTH
        k_scr[0:blk, :] = prev_ref[:, pl.ds(base + ATTN_WIDTH, ATTN_WIDTH)]
        k_scr[blk:, :] = main_ref[:, pl.ds(base + ATTN_WIDTH, ATTN_WIDTH)]
        vt_scr[:, 0:blk] = prev_ref[:, pl.ds(base + 2 * ATTN_WIDTH, ATTN_WIDTH)].astype(F32).T.astype(BF16)
        for j in range(nq):
            vt_scr[:, pl.ds((j + 1) * blk, blk)] = (
                main_ref[pl.ds(j * blk, blk), pl.ds(base + 2 * ATTN_WIDTH, ATTN_WIDTH)].astype(F32).T.astype(BF16))
        for j in range(nq):
            rows = pl.ds(j * blk, blk)
            keys = pl.ds(j * blk, 2 * blk)
            scores = []
            for h in range(N_HEADS):
                q = main_ref[rows, pl.ds(base + h * HEAD_DIM, HEAD_DIM)]
                k2 = k_scr[keys, pl.ds(h * HEAD_DIM, HEAD_DIM)]
                scores.append(lax.dot_general(k2, q, nt, preferred_element_type=F32))
            probs, inv_l, lse_parts = [], [], []
            for h in range(N_HEADS):
                s = scores[h] + tab_ref[h]
                if j == 0:
                    pen = jnp.where(first_group, NEG_INF, 0.0).astype(F32)
                    s = jnp.concatenate([s[:blk] + pen, s[blk:]], axis=0)
                m = jnp.max(s, axis=0, keepdims=True)
                p = jnp.exp2(s - m)
                l = jnp.sum(p, axis=0, keepdims=True)
                probs.append(p.astype(BF16))
                inv_l.append(1.0 / l)
                lse_parts.append(m * math.log(2.0) + jnp.log(l))
            o_parts = []
            for h in range(N_HEADS):
                vt2 = vt_scr[pl.ds(h * HEAD_DIM, HEAD_DIM), keys]
                o_parts.append(jnp.dot(vt2, probs[h], preferred_element_type=F32) * inv_l[h])
            o_t = jnp.concatenate(o_parts, axis=0)
            o_ref[rows, pl.ds(ri * ATTN_WIDTH, ATTN_WIDTH)] = o_t.T.astype(BF16)
            lse_t = jnp.concatenate(lse_parts + [jnp.zeros((LANES - N_HEADS, blk), F32)], axis=0)
            lse_ref[rows, pl.ds(ri * LANES, LANES)] = lse_t.T


def _attn_branch(qkv_d, tab, batch, seq, dil, rq, nq):
    sub = seq // dil
    nb = sub // ATTN_BLOCK
    groups = nb // nq
    return pl.pallas_call(
        functools.partial(_attn_kernel, rq=rq, nq=nq),
        grid=(batch, dil // rq, groups),
        in_specs=[
            pl.BlockSpec((nq * ATTN_BLOCK, rq * QKV_WIDTH), lambda b, r, g: (b * groups + g, r)),
            pl.BlockSpec((ATTN_BLOCK, rq * QKV_WIDTH), lambda b, r, g: (b * nb + jnp.maximum(g * nq - 1, 0), r)),
            pl.BlockSpec((N_HEADS, 2 * ATTN_BLOCK, ATTN_BLOCK), lambda b, r, g: (0, 0, 0)),
        ],
        out_specs=[
            pl.BlockSpec((nq * ATTN_BLOCK, rq * ATTN_WIDTH), lambda b, r, g: (b * groups + g, r)),
            pl.BlockSpec((nq * ATTN_BLOCK, rq * LANES), lambda b, r, g: (b * groups + g, r)),
        ],
        out_shape=[
            jax.ShapeDtypeStruct((batch * sub, dil * ATTN_WIDTH), BF16),
            jax.ShapeDtypeStruct((batch * sub, dil * LANES), F32),
        ],
        scratch_shapes=[
            pltpu.VMEM(((nq + 1) * ATTN_BLOCK, ATTN_WIDTH), BF16),
            pltpu.VMEM((ATTN_WIDTH, (nq + 1) * ATTN_BLOCK), BF16),
        ],
        compiler_params=pltpu.CompilerParams(
            dimension_semantics=("parallel", "parallel", "parallel"), vmem_limit_bytes=VMEM_LIMIT),
        name="attn",
    )(qkv_d, qkv_d, tab)


def _t5_bucket(dist):
    n = jnp.maximum(dist, 1).astype(F32)
    large = REL_MAX_EXACT + (jnp.log(n / REL_MAX_EXACT) / math.log(REL_MAX_DISTANCE / REL_MAX_EXACT)
                             * (N_REL_BUCKETS - REL_MAX_EXACT)).astype(I32)
    large = jnp.minimum(large, N_REL_BUCKETS - 1)
    return jnp.where(dist < REL_MAX_EXACT, dist, large)


def _shifted_rows(w, n):
    lead = w.shape[:-1]
    width = w.shape[-1]
    flat = jnp.tile(w, (1,) * len(lead) + (n + 1,))[..., :n * (width + 1)]
    return flat.reshape(lead + (n, width + 1))[..., :n]


def _bias_tables(rel_bias):
    blk = ATTN_BLOCK
    assert all(window // dil == blk for window, dil in DILATED_BRANCHES)
    n_br = len(DILATED_BRANCHES)
    dist = jnp.stack([jnp.arange(blk + 1) * dil for _, dil in DILATED_BRANCHES])
    f = jnp.swapaxes(rel_bias[_t5_bucket(dist)], 1, 2).astype(F32)
    neg = jnp.full((n_br, N_HEADS, blk - 1), NEG_INF, F32)
    neg1 = jnp.full((n_br, N_HEADS, 1), NEG_INF, F32)
    w_prev = jnp.concatenate([neg, f[..., :0:-1], neg1], axis=-1)
    prev = _shifted_rows(w_prev, blk)[..., ::-1]
    w_cur = jnp.concatenate([neg, f[..., :blk], neg1], axis=-1)
    cur = _shifted_rows(w_cur, blk)[..., ::-1, :]
    return jnp.concatenate([prev, cur], axis=-2) * LOG2_E


def _mixout_kernel(x_ref, u_ref, uh_ref, o1_ref, o4_ref, o16_ref, l1_ref, l4_ref, l16_ref,
                   wpool_ref, pscale_ref, wout_ref, gffn_ref, wrt_ref, brt_ref, spread_ref,
                   x1_ref, h2_ref, pos_ref, gate_ref, pc_ref,
                   ext_scr, mix_scr, o_scr, l_scr, *, tiles_per_seq):
    i = pl.program_id(0)
    tt = TOKEN_TILE
    seq_tile = i % tiles_per_seq

    halo = uh_ref[...]
    ext_scr[0:POOL_HALO, :] = jnp.where(seq_tile == 0, jnp.zeros_like(halo), halo)
    ext_scr[POOL_HALO:, :] = u_ref[...]
    tpos = seq_tile * tt + lax.broadcasted_iota(I32, (tt, 1), 0)
    for gi, w in enumerate(POOL_WINDOWS):
        cols = pl.ds(gi * POOL_GROUP_DIM, POOL_GROUP_DIM)
        run = ext_scr[:, cols]
        shift = 1
        while shift < w:
            run = run + pltpu.roll(run, shift, axis=0)
            shift *= 2
        assert shift == w and w - 1 < POOL_HALO
        acc = run[POOL_HALO:]
        tok = ext_scr[pl.ds(POOL_HALO, tt), cols]
        cnt = jnp.minimum(tpos + 1, w).astype(F32)
        pooled = (acc / cnt - tok).astype(BF16)
        mixed = jnp.dot(pooled, wpool_ref[gi], preferred_element_type=F32) * pscale_ref[:, cols]
        mix_scr[:, cols] = mixed.astype(BF16)

    o_slabs = ATTN_WIDTH // LANES
    for bi, (dil, o_ref, l_ref) in enumerate(((4, o4_ref, l4_ref), (16, o16_ref, l16_ref))):
        rows = tt // dil
        for r in range(dil):
            l_scr[bi, pl.ds(r, rows, stride=dil), :] = l_ref[:, pl.ds(r * LANES, LANES)]
            for c in range(o_slabs):
                o_scr[bi * o_slabs + c, pl.ds(r, rows, stride=dil), :] = (
                    o_ref[:, pl.ds(r * ATTN_WIDTH + c * LANES, LANES)].astype(F32))
    l1, l4, l16 = l1_ref[...], l_scr[0], l_scr[1]
    lm = jnp.maximum(jnp.maximum(l1, l4), l16)
    e1, e4, e16 = jnp.exp(l1 - lm), jnp.exp(l4 - lm), jnp.exp(l16 - lm)
    inv = 1.0 / (e1 + e4 + e16)

    def per_feature(w):
        hi = w.astype(BF16)
        lo = (w - hi.astype(F32)).astype(BF16)
        return (jnp.dot(hi, spread_ref[...], preferred_element_type=F32)
                + jnp.dot(lo, spread_ref[...], preferred_element_type=F32))

    w1, w4, w16 = per_feature(e1 * inv), per_feature(e4 * inv), per_feature(e16 * inv)
    for c in range(o_slabs):
        sl = pl.ds(c * LANES, LANES)
        cs = slice(c * LANES, (c + 1) * LANES)
        o = w1[:, cs] * o1_ref[:, sl].astype(F32) + w4[:, cs] * o_scr[c] + w16[:, cs] * o_scr[o_slabs + c]
        mix_scr[:, pl.ds(POOL_WIDTH + c * LANES, LANES)] = o.astype(BF16)

    x1 = x_ref[...] + jnp.dot(mix_scr[...], wout_ref[...], preferred_element_type=F32)
    x1_ref[...] = x1

    h2 = _rms(x1, gffn_ref[...])
    h2_ref[...] = h2.astype(BF16)
    nt = (((1,), (1,)), ((), ()))
    wr = wrt_ref[...]
    wr_hi = wr.astype(BF16)
    wr_lo = (wr - wr_hi.astype(F32)).astype(BF16)
    h2_hi = h2.astype(BF16)
    h2_lo = (h2 - h2_hi.astype(F32)).astype(BF16)
    logits = (lax.dot_general(wr_hi, h2_hi, nt, preferred_element_type=F32)
              + lax.dot_general(wr_hi, h2_lo, nt, preferred_element_type=F32)
              + lax.dot_general(wr_lo, h2_hi, nt, preferred_element_type=F32))
    logits = logits + brt_ref[:, 0:1]
    eio = lax.broadcasted_iota(I32, (N_EXPERTS, tt), 0)
    sel, val = [], []
    for _ in range(TOP_K):
        m = jnp.max(logits, axis=0, keepdims=True)
        idx = jnp.min(jnp.where(logits == m, eio, N_EXPERTS), axis=0, keepdims=True)
        sel.append(idx)
        val.append(m)
        logits = jnp.where(eio == idx, -jnp.inf, logits)
    ex = [jnp.exp(v - val[0]) for v in val]
    den = ex[0] + ex[1] + ex[2] + ex[3]
    gates = [e / den for e in ex]

    hot = [(eio == s) for s in sel]
    onehot = (hot[0] | hot[1] | hot[2] | hot[3]).astype(F32)
    ti = lax.broadcasted_iota(I32, (tt, tt), 0)
    tj = lax.broadcasted_iota(I32, (tt, tt), 1)
    before = (ti < tj).astype(BF16)
    rank = jnp.dot(onehot.astype(BF16), before, preferred_element_type=F32)
    cnt = jnp.sum(onehot, axis=1, keepdims=True)
    pc = jnp.maximum(jnp.floor((cnt + (ROW_ALIGN - 1)) * (1.0 / ROW_ALIGN)), 1.0)
    pcb = jnp.broadcast_to(pc, (N_EXPERTS, LANES))
    pc_ref[0] = pcb
    pcm = jnp.concatenate([pcb, jnp.zeros((LANES - N_EXPERTS, LANES), F32)], axis=0).astype(BF16)
    li = lax.broadcasted_iota(I32, (N_EXPERTS, LANES), 0)
    lj = lax.broadcasted_iota(I32, (N_EXPERTS, LANES), 1)
    lower = (lj < li).astype(BF16)
    off = jnp.dot(lower, pcm, preferred_element_type=F32)[:, 0:1] * float(ROW_ALIGN)
    where_to = off + rank
    zero_i = jnp.zeros((1, tt), I32)
    zero_f = jnp.zeros((1, tt), F32)
    pos_rows = [jnp.sum(jnp.where(hk, where_to, 0.0), axis=0, keepdims=True).astype(I32) for hk in hot]
    pos_ref[0] = jnp.concatenate(pos_rows + [zero_i] * (SUBLANES - TOP_K), axis=0)
    gate_ref[0] = jnp.concatenate(gates + [zero_f] * (SUBLANES - TOP_K), axis=0)


def _mixout(x2d, u, o_list, lse_list, wpool, pscale, wout, gffn, wrt, brt, seq):
    t = x2d.shape[0]
    tt = TOKEN_TILE
    n_tt = t // tt
    halo_blocks = tt // POOL_HALO
    row = lambda i: (i, 0)
    const2 = lambda i: (0, 0)
    dils = [dil for _, dil in DILATED_BRANCHES]
    spread = (jnp.arange(LANES)[:, None] == jnp.arange(ATTN_WIDTH)[None, :] // HEAD_DIM).astype(BF16)
    return pl.pallas_call(
        functools.partial(_mixout_kernel, tiles_per_seq=seq // tt),
        grid=(n_tt,),
        in_specs=[
            pl.BlockSpec((tt, D_MODEL), row),
            pl.BlockSpec((tt, POOL_WIDTH), row),
            pl.BlockSpec((POOL_HALO, POOL_WIDTH), lambda i: (jnp.maximum(i * halo_blocks - 1, 0), 0)),
            *[pl.BlockSpec((tt // dil, dil * ATTN_WIDTH), row) for dil in dils],
            *[pl.BlockSpec((tt // dil, dil * LANES), row) for dil in dils],
            pl.BlockSpec((len(POOL_WINDOWS), POOL_GROUP_DIM, POOL_GROUP_DIM), lambda i: (0, 0, 0)),
            pl.BlockSpec((1, POOL_WIDTH), const2),
            pl.BlockSpec((D_MODEL, D_MODEL), const2),
            pl.BlockSpec((1, D_MODEL), const2),
            pl.BlockSpec((N_EXPERTS, D_MODEL), const2),
            pl.BlockSpec((N_EXPERTS, LANES), const2),
            pl.BlockSpec((LANES, ATTN_WIDTH), const2),
        ],
        out_specs=[
            pl.BlockSpec((tt, D_MODEL), row),
            pl.BlockSpec((tt, D_MODEL), row),
            pl.BlockSpec((1, SUBLANES, tt), lambda i: (i, 0, 0)),
            pl.BlockSpec((1, SUBLANES, tt), lambda i: (i, 0, 0)),
            pl.BlockSpec((1, N_EXPERTS, LANES), lambda i: (i, 0, 0)),
        ],
        out_shape=[
            jax.ShapeDtypeStruct((t, D_MODEL), F32),
            jax.ShapeDtypeStruct((t, D_MODEL), BF16),
            jax.ShapeDtypeStruct((n_tt, SUBLANES, tt), I32),
            jax.ShapeDtypeStruct((n_tt, SUBLANES, tt), F32),
            jax.ShapeDtypeStruct((n_tt, N_EXPERTS, LANES), F32),
        ],
        scratch_shapes=[
            pltpu.VMEM((tt + POOL_HALO, POOL_WIDTH), F32),
            pltpu.VMEM((tt, D_MODEL), BF16),
            pltpu.VMEM((2 * ATTN_WIDTH // LANES, tt, LANES), F32),
            pltpu.VMEM((2, tt, LANES), F32),
        ],
        compiler_params=pltpu.CompilerParams(
            dimension_semantics=("parallel",), vmem_limit_bytes=VMEM_LIMIT),
        name="mixout",
    )(x2d, u, u, *o_list, *lse_list, wpool, pscale, wout, gffn, wrt, brt, spread)


def _chunk_copies(loff_s, pcnt_s, gstart_s, tile, local_ref, global_ref, sems, to_global, slot=None):
    slot = tile % 2 if slot is None else slot
    out = []
    for e in range(N_EXPERTS):
        n = pl.multiple_of(pcnt_s[tile, e], ROW_ALIGN)
        lo = pl.multiple_of(loff_s[tile, e], ROW_ALIGN)
        go = pl.multiple_of(gstart_s[tile, e], ROW_ALIGN)
        loc = local_ref.at[slot, pl.ds(lo, n)]
        glo = global_ref.at[pl.ds(go, n)]
        src, dst = (loc, glo) if to_global else (glo, loc)
        out.append(pltpu.make_async_copy(src, dst, sems.at[slot]))
    return out


def _start_all(copies):
    for cp in copies:
        cp.start()


def _wait_chunks(loff_s, pcnt_s, tile, local_ref, global_ref, sems, to_global, slot=None):
    slot = tile % 2 if slot is None else slot
    rows = pl.multiple_of(loff_s[tile, N_EXPERTS - 1] + pcnt_s[tile, N_EXPERTS - 1], ROW_ALIGN)
    loc = local_ref.at[slot, pl.ds(0, rows)]
    glo = global_ref.at[pl.ds(0, rows)]
    src, dst = (loc, glo) if to_global else (glo, loc)
    pltpu.make_async_copy(src, dst, sems.at[slot]).wait()


def _dispatch_kernel(loff_s, pcnt_s, gstart_s, tail_s, h2_ref, pos_ref, xs_ref, loc_scr, zero_scr, sem, tail_sem,
                     spare_sem):
    i = pl.program_id(0)
    tt = TOKEN_TILE

    @pl.when(i == 0)
    def _():
        zero_scr[...] = jnp.zeros_like(zero_scr)
        tails = []
        for e in range(N_EXPERTS):
            n = pl.multiple_of(tail_s[1, e], ROW_ALIGN)
            start = pl.multiple_of(tail_s[0, e], ROW_ALIGN)
            tails.append((n > 0, pltpu.make_async_copy(zero_scr.at[pl.ds(0, n)], xs_ref.at[pl.ds(start, n)], tail_sem)))
        for cond, cp in tails:
            pl.when(cond)(cp.start)
        for cond, cp in tails:
            pl.when(cond)(cp.wait)

    def spare_tile(j):
        return pltpu.make_async_copy(
            zero_scr, xs_ref.at[pl.ds(pl.multiple_of(j * EXPERT_TILE, EXPERT_TILE), EXPERT_TILE)], spare_sem)

    n_tiles = xs_ref.shape[0] // EXPERT_TILE

    @pl.when(i == 0)
    def _():
        lax.fori_loop(tail_s[2, 0], n_tiles, lambda j, c: (spare_tile(j).start(), c)[1], 0)

    @pl.when(i == pl.num_programs(0) - 1)
    def _():
        lax.fori_loop(tail_s[2, 0], n_tiles, lambda j, c: (spare_tile(j).wait(), c)[1], 0)

    pos = pos_ref[0]
    h2 = h2_ref[...]
    slot = i % 2
    used_rows = loff_s[i, N_EXPERTS - 1] + pcnt_s[i, N_EXPERTS - 1]
    i16 = jnp.int16
    row16 = lax.broadcasted_iota(I32, (SORT_BLOCK, tt), 0).astype(i16)
    pos16 = [jnp.broadcast_to(pos[k:k + 1], (SORT_BLOCK, tt)).astype(i16) for k in range(TOP_K)]

    def one_hot(jb):
        jio = row16 + jnp.asarray(jb * SORT_BLOCK, i16)
        hit = (jio == pos16[0]) | (jio == pos16[1]) | (jio == pos16[2]) | (jio == pos16[3])
        return jnp.where(hit, jnp.asarray(1.0, BF16), jnp.asarray(0.0, BF16))

    def sort_block(jb, onehot):
        loc_scr[slot, pl.ds(jb * SORT_BLOCK, SORT_BLOCK), :] = jnp.dot(
            onehot, h2, preferred_element_type=F32).astype(BF16)

    n_sure = TOKEN_TILE * TOP_K // SORT_BLOCK
    onehot = one_hot(0)
    for jb in range(n_sure):
        nxt = one_hot(jb + 1) if jb + 1 < n_sure else None
        sort_block(jb, onehot)
        onehot = nxt
    for jb in range(n_sure, LOCAL_ROWS // SORT_BLOCK):
        pl.when(jb * SORT_BLOCK < used_rows)(lambda jb=jb: sort_block(jb, one_hot(jb)))

    @pl.when(i > 0)
    def _():
        _wait_chunks(loff_s, pcnt_s, i - 1, loc_scr, xs_ref, sem, True)

    _start_all(_chunk_copies(loff_s, pcnt_s, gstart_s, i, loc_scr, xs_ref, sem, True))

    @pl.when(i == pl.num_programs(0) - 1)
    def _():
        _wait_chunks(loff_s, pcnt_s, i, loc_scr, xs_ref, sem, True)


def _dispatch(loff, pcnt, gstart, tails, h2, pos, n_rows):
    t = h2.shape[0]
    tt = TOKEN_TILE
    grid_spec = pltpu.PrefetchScalarGridSpec(
        num_scalar_prefetch=4,
        grid=(t // tt,),
        in_specs=[
            pl.BlockSpec((tt, D_MODEL), lambda i, *_: (i, 0)),
            pl.BlockSpec((1, SUBLANES, tt), lambda i, *_: (i, 0, 0)),
        ],
        out_specs=pl.BlockSpec(memory_space=pl.ANY),
        scratch_shapes=[
            pltpu.VMEM((2, LOCAL_ROWS, D_MODEL), BF16),
            pltpu.VMEM((EXPERT_TILE, D_MODEL), BF16),
            pltpu.SemaphoreType.DMA((2,)),
            pltpu.SemaphoreType.DMA(()),
            pltpu.SemaphoreType.DMA(()),
        ],
    )
    return pl.pallas_call(
        _dispatch_kernel,
        grid_spec=grid_spec,
        out_shape=jax.ShapeDtypeStruct((n_rows, D_MODEL), BF16),
        compiler_params=pltpu.CompilerParams(
            dimension_semantics=("arbitrary",), vmem_limit_bytes=VMEM_LIMIT),
        name="dispatch",
    )(loff, pcnt, gstart, tails, h2, pos)


def _expert_kernel(te_s, nu_s, slot_s, next_s, xs_ref, wgu_hbm, bgu_ref, wd_hbm, bd_ref, ys_ref,
                   wgu_stage, wd_stage, wgu_bf, wd_bf, act_scr, wsem):
    i = pl.program_id(0)
    live = i < nu_s[0]
    expert = te_s[i]
    new_expert = (i == 0) | (expert != te_s[jnp.maximum(i - 1, 0)])

    def weight_copies(e, slot):
        return (pltpu.make_async_copy(wgu_hbm.at[e], wgu_stage.at[slot], wsem.at[0, slot]),
                pltpu.make_async_copy(wd_hbm.at[e], wd_stage.at[slot], wsem.at[1, slot]))

    @pl.when(live & new_expert)
    def _():
        slot = slot_s[expert]

        @pl.when(i == 0)
        def _():
            for cp in weight_copies(expert, slot):
                cp.start()

        for cp in weight_copies(expert, slot):
            cp.wait()
        wgu_bf[...] = wgu_stage[slot].astype(BF16)
        wd_bf[...] = wd_stage[slot].astype(BF16)
        upcoming = next_s[expert]

        @pl.when(upcoming < N_EXPERTS)
        def _():
            for cp in weight_copies(upcoming, 1 - slot):
                cp.start()

    @pl.when(live)
    def _():
        x = xs_ref[...]
        n_chunks = D_EXPERT // EXPERT_CHUNK

        def gate_up(c):
            gc = pl.ds(c * EXPERT_CHUNK, EXPERT_CHUNK)
            uc = pl.ds(D_EXPERT + c * EXPERT_CHUNK, EXPERT_CHUNK)
            return (jnp.dot(x, wgu_bf[:, gc], preferred_element_type=F32) + bgu_ref[0, :, gc],
                    jnp.dot(x, wgu_bf[:, uc], preferred_element_type=F32) + bgu_ref[0, :, uc])

        g, u = gate_up(0)
        for c in range(n_chunks):
            nxt = gate_up(c + 1) if c + 1 < n_chunks else None
            g = jnp.minimum(g, SWIGLU_LIMIT)
            u = jnp.clip(u, -SWIGLU_LIMIT, SWIGLU_LIMIT)
            act_scr[:, pl.ds(c * EXPERT_CHUNK, EXPERT_CHUNK)] = (
                (u + 1.0) * (g * jax.nn.sigmoid(SWIGLU_ALPHA * g))).astype(BF16)
            if nxt is not None:
                g, u = nxt
        y = jnp.dot(act_scr[...], wd_bf[...], preferred_element_type=F32) + bd_ref[0]
        ys_ref[...] = y.astype(BF16)

    @pl.when(jnp.logical_not(live))
    def _():
        ys_ref[...] = jnp.zeros_like(ys_ref)


def _experts(tile_expert, n_used, stage_slot, next_expert, xs, wgu, bgu, wd, bd):
    n_rows = xs.shape[0]
    tm = EXPERT_TILE
    live = lambda i, te, nu: jnp.minimum(i, nu[0] - 1)
    grid_spec = pltpu.PrefetchScalarGridSpec(
        num_scalar_prefetch=4,
        grid=(n_rows // tm,),
        in_specs=[
            pl.BlockSpec((tm, D_MODEL), lambda i, te, nu, *_: (live(i, te, nu), 0)),
            pl.BlockSpec(memory_space=pl.ANY),
            pl.BlockSpec((1, 1, 2 * D_EXPERT), lambda i, te, nu, *_: (te[live(i, te, nu)], 0, 0)),
            pl.BlockSpec(memory_space=pl.ANY),
            pl.BlockSpec((1, 1, D_MODEL), lambda i, te, nu, *_: (te[live(i, te, nu)], 0, 0)),
        ],
        out_specs=pl.BlockSpec((tm, D_MODEL), lambda i, te, nu, *_: (i, 0)),
        scratch_shapes=[
            pltpu.VMEM((2, D_MODEL, 2 * D_EXPERT), F32),
            pltpu.VMEM((2, D_EXPERT, D_MODEL), F32),
            pltpu.VMEM((D_MODEL, 2 * D_EXPERT), BF16),
            pltpu.VMEM((D_EXPERT, D_MODEL), BF16),
            pltpu.VMEM((tm, D_EXPERT), BF16),
            pltpu.SemaphoreType.DMA((2, 2)),
        ],
    )
    return pl.pallas_call(
        _expert_kernel,
        grid_spec=grid_spec,
        out_shape=jax.ShapeDtypeStruct((n_rows, D_MODEL), BF16),
        compiler_params=pltpu.CompilerParams(
            dimension_semantics=("arbitrary",), vmem_limit_bytes=VMEM_LIMIT),
        name="experts",
    )(tile_expert, n_used, stage_slot, next_expert, xs, wgu, bgu, wd, bd)


def _combine_kernel(loff_s, pcnt_s, gstart_s, x1_ref, pos_ref, gate_ref, p_ref, gple_ref, wg_ref, wp_ref,
                    gfin_ref, ys_ref, out_ref, loc_scr, sem):
    i = pl.program_id(0)
    tt = TOKEN_TILE

    @pl.when(i == 0)
    def _():
        loc_scr[...] = jnp.zeros_like(loc_scr)
        _start_all(_chunk_copies(loff_s, pcnt_s, gstart_s, i, loc_scr, ys_ref, sem, False))

    _wait_chunks(loff_s, pcnt_s, i, loc_scr, ys_ref, sem, False)
    slot = i % 2
    last = pl.num_programs(0) - 1
    ahead = jnp.minimum(i + 1, last)
    requests = _chunk_copies(loff_s, pcnt_s, gstart_s, ahead, loc_scr, ys_ref, sem, False, slot=1 - slot)

    def to_columns(rows):
        return jnp.concatenate([rows, jnp.zeros((LANES - rows.shape[0], tt), F32)], axis=0).T

    post = to_columns(pos_ref[0].astype(F32))
    gatet = to_columns(gate_ref[0])

    def gate_block(jb):
        jio = (lax.broadcasted_iota(I32, (tt, GATHER_BLOCK), 1) + jb * GATHER_BLOCK).astype(F32)
        w = jnp.zeros((tt, GATHER_BLOCK), F32)
        for k in range(TOP_K):
            w = jnp.where(jio == post[:, k:k + 1], gatet[:, k:k + 1], w)
        return w.astype(BF16)

    n_blocks = LOCAL_ROWS // GATHER_BLOCK
    moe = jnp.zeros((tt, D_MODEL), F32)
    w = gate_block(0)
    per_block = -(-len(requests) // n_blocks)
    for jb in range(n_blocks):
        nxt = gate_block(jb + 1) if jb + 1 < n_blocks else None
        y = loc_scr[slot, pl.ds(jb * GATHER_BLOCK, GATHER_BLOCK), :]
        moe = moe + jnp.dot(w, y, preferred_element_type=F32)
        w = nxt
        _start_all(requests[jb * per_block:(jb + 1) * per_block])

    x2 = x1_ref[...] + moe
    gate = jax.nn.sigmoid(jnp.dot(_rms(x2, gple_ref[...]).astype(BF16), wg_ref[...], preferred_element_type=F32))
    emb = jnp.dot(p_ref[...].astype(BF16), wp_ref[...], preferred_element_type=F32)
    out_ref[...] = _rms(x2 + emb * gate, gfin_ref[...])

    @pl.when(i == last)
    def _():
        _wait_chunks(loff_s, pcnt_s, ahead, loc_scr, ys_ref, sem, False, slot=1 - slot)


def _combine(loff, pcnt, gstart, x1, pos, gates, p2d, gple, wg, wp, gfin, ys):
    t = x1.shape[0]
    tt = TOKEN_TILE
    row = lambda i, *_: (i, 0)
    const2 = lambda i, *_: (0, 0)
    grid_spec = pltpu.PrefetchScalarGridSpec(
        num_scalar_prefetch=3,
        grid=(t // tt,),
        in_specs=[
            pl.BlockSpec((tt, D_MODEL), row),
            pl.BlockSpec((1, SUBLANES, tt), lambda i, *_: (i, 0, 0)),
            pl.BlockSpec((1, SUBLANES, tt), lambda i, *_: (i, 0, 0)),
            pl.BlockSpec((tt, PLE_DIM), row),
            pl.BlockSpec((1, D_MODEL), const2),
            pl.BlockSpec((D_MODEL, D_MODEL), const2),
            pl.BlockSpec((PLE_DIM, D_MODEL), const2),
            pl.BlockSpec((1, D_MODEL), const2),
            pl.BlockSpec(memory_space=pl.ANY),
        ],
        out_specs=pl.BlockSpec((tt, D_MODEL), row),
        scratch_shapes=[pltpu.VMEM((2, LOCAL_ROWS, D_MODEL), BF16), pltpu.SemaphoreType.DMA((2,))],
    )
    return pl.pallas_call(
        _combine_kernel,
        grid_spec=grid_spec,
        out_shape=jax.ShapeDtypeStruct((t, D_MODEL), F32),
        compiler_params=pltpu.CompilerParams(
            dimension_semantics=("arbitrary",), vmem_limit_bytes=VMEM_LIMIT),
        name="combine",
    )(loff, pcnt, gstart, x1, pos, gates, p2d, gple, wg, wp, gfin, ys)


def _routing_tables(pc16):
    pcnt = pc16 * ROW_ALIGN
    loff = jnp.cumsum(pcnt, axis=1) - pcnt
    seg = jnp.sum(pcnt, axis=0)
    segpad = (seg + EXPERT_TILE - 1) // EXPERT_TILE * EXPERT_TILE
    seg_end = jnp.cumsum(segpad)
    ebase = seg_end - segpad
    gstart = ebase[None, :] + jnp.cumsum(pcnt, axis=0) - pcnt
    tails = jnp.stack([ebase + seg, segpad - seg, jnp.broadcast_to(seg_end[-1] // EXPERT_TILE, seg.shape)])
    return pcnt, loff, gstart, tails, seg_end


def kernel(x, p, g_mix, w_in, w_pool, pool_scale, rel_bias, w_out, g_ffn, w_router, b_router, w_gate_up,
           b_gate_up, w_down, b_down, g_ple, w_ple_gate, w_ple_proj, g_final):
    b, s, d = x.shape
    t = b * s
    x2d = x.reshape(t, d)
    for layer in range(w_in.shape[0]):
        col = jnp.arange(w_in.shape[2])
        is_q = (col >= POOL_WIDTH) & (col < POOL_WIDTH + ATTN_WIDTH)
        w_in_scaled = (w_in[layer] * jnp.where(is_q, QUERY_SCALE, 1.0)).astype(BF16)
        u, *qkv_by_dil = _inproj(x2d, g_mix[layer][None], w_in_scaled)
        o_list, lse_list = [], []
        tabs = _bias_tables(rel_bias)
        for bi, ((_, dil), qkv_d, (rq, nq)) in enumerate(zip(DILATED_BRANCHES, qkv_by_dil, ATTN_STEP)):
            o_d, lse_d = _attn_branch(qkv_d, tabs[bi], b, s, dil, rq, nq)
            o_list.append(o_d)
            lse_list.append(lse_d)
        brt = jnp.broadcast_to(b_router[layer][:, None], (N_EXPERTS, LANES))
        x1, h2, pos, gates, pc = _mixout(
            x2d, u, o_list, lse_list, w_pool[layer].astype(BF16), pool_scale[layer][None],
            w_out[layer].astype(BF16), g_ffn[layer][None], w_router[layer].T, brt, s)

        pc16 = pc[:, :, 0].astype(I32)
        pcnt, loff, gstart, tails, seg_end = _routing_tables(pc16)
        n_rows = t * TOP_K + N_EXPERTS * (t // TOKEN_TILE) * ROW_ALIGN + N_EXPERTS * (EXPERT_TILE - 1)
        n_rows = (n_rows + EXPERT_TILE - 1) // EXPERT_TILE * EXPERT_TILE
        n_tiles = n_rows // EXPERT_TILE
        tile_start = jnp.arange(n_tiles, dtype=I32) * EXPERT_TILE
        tile_expert = jnp.minimum(jnp.sum(seg_end[None, :] <= tile_start[:, None], axis=1), N_EXPERTS - 1).astype(I32)
        n_used = (seg_end[-1] // EXPERT_TILE).astype(I32)[None]
        xs = _dispatch(loff, pcnt, gstart, tails, h2, pos, n_rows)
        has_rows = jnp.diff(seg_end, prepend=0) > 0
        experts = jnp.arange(N_EXPERTS, dtype=I32)
        stage_slot = ((jnp.cumsum(has_rows) - 1) % 2).astype(I32)
        next_expert = jnp.min(jnp.where(has_rows[None, :] & (experts[None, :] > experts[:, None]),
                                        experts[None, :], N_EXPERTS), axis=1).astype(I32)
        ys = _experts(tile_expert, n_used, stage_slot, next_expert, xs, w_gate_up[layer],
                      b_gate_up[layer][:, None, :], w_down[layer], b_down[layer][:, None, :])

        assert layer == w_in.shape[0] - 1, "single-layer pipeline: the final norm is fused into combine"
        x2d = _combine(loff, pcnt, gstart, x1, pos, gates, p[layer].reshape(t, PLE_DIM), g_ple[layer][None],
                       w_ple_gate[layer].astype(BF16), w_ple_proj[layer].astype(BF16), g_final[None], ys)
    return x2d.reshape(b, s, d)
```
